```python
import math, functools
import jax, jax.numpy as jnp
from jax import lax
import numpy as np

D_MODEL = 1024
BATCH = 16
SEQ = 2048
DEPTH = 2
DEC_BATCH = 32
DEC_SEQ = 8
PAST_LEN = 16384
PAGE_SIZE = 128

HEAD_DIM = 64
RET_WIDTH = D_MODEL // 4
NSA_WIDTH = D_MODEL // 2
GDN_WIDTH = D_MODEL - RET_WIDTH - NSA_WIDTH
MIX_WIDTH = RET_WIDTH + NSA_WIDTH + GDN_WIDTH
RET_HEADS = RET_WIDTH // HEAD_DIM
NSA_HEADS = NSA_WIDTH // HEAD_DIM
NSA_KV_HEADS = 2
NSA_GROUP = NSA_HEADS // NSA_KV_HEADS
GDN_HEADS = GDN_WIDTH // HEAD_DIM
CMP_BLOCK = 32
CMP_STRIDE = 16
SEL_BLOCK = 64
N_SELECT = 16
WINDOW = 512
RET_CHUNK = 64
GDN_CHUNK = 64
CONV_WIDTH = 4
SEL_QBLK = 32
WIN_QBLK = 128
ROPE_THETA = 10000.0
EPS = 1e-6
N_CACHE_SLOTS = 4

PROJ_SPLITS = (
    ('ret_q', RET_WIDTH), ('ret_k', RET_WIDTH), ('ret_v', RET_WIDTH), ('ret_g', RET_WIDTH),
    ('nsa_q', NSA_WIDTH), ('nsa_kv', 6 * NSA_KV_HEADS * HEAD_DIM), ('nsa_gate', 3 * NSA_HEADS), ('nsa_g', NSA_WIDTH),
    ('gdn_qkv', 3 * GDN_WIDTH), ('gdn_beta', GDN_HEADS), ('gdn_a', GDN_HEADS), ('gdn_g', GDN_WIDTH),
)
PROJ_WIDTH = sum(w for _, w in PROJ_SPLITS)

kernel_name = 'hymba_retention_nsa_gdn_step'


def _split_proj(u):
    offs = [int(o) for o in np.cumsum([w for _, w in PROJ_SPLITS])[:-1]]
    return dict(zip([n for n, _ in PROJ_SPLITS], jnp.split(u, offs, axis=-1)))


def _rmsnorm(x, g):
    xf = x.astype(jnp.float32)
    y = xf * lax.rsqrt(jnp.mean(xf * xf, axis=-1, keepdims=True) + EPS)
    return (y * g.astype(jnp.float32)).astype(x.dtype)


def _head_layernorm(o, g):
    B, T = o.shape[:2]
    of = o.astype(jnp.float32)
    mu = jnp.mean(of, axis=-1, keepdims=True)
    var = jnp.mean(jnp.square(of - mu), axis=-1, keepdims=True)
    return ((of - mu) * lax.rsqrt(var + EPS)).reshape(B, T, -1) * g.astype(jnp.float32)


def _head_rmsnorm(o, g):
    B, T = o.shape[:2]
    of = o.astype(jnp.float32)
    y = of * lax.rsqrt(jnp.mean(of * of, axis=-1, keepdims=True) + EPS) * g.astype(jnp.float32)
    return y.reshape(B, T, -1)


def _l2norm(x):
    return x * lax.rsqrt(jnp.sum(x * x, axis=-1, keepdims=True) + EPS)


def _rope(x, pos):
    half = HEAD_DIM // 2
    inv = ROPE_THETA ** (-jnp.arange(half, dtype=jnp.float32) / half)
    ang = pos.astype(jnp.float32)[:, None] * inv[None, :]
    cos = jnp.cos(ang)[:, None, :]
    sin = jnp.sin(ang)[:, None, :]
    xf = x.astype(jnp.float32)
    x1, x2 = xf[..., :half], xf[..., half:]
    return jnp.concatenate([x1 * cos - x2 * sin, x2 * cos + x1 * sin], axis=-1).astype(x.dtype)


def _masked_softmax(s, mask):
    s = jnp.where(mask, s, -jnp.inf)
    m = jnp.max(s, axis=-1, keepdims=True)
    m = jnp.where(jnp.isfinite(m), m, 0.0)
    p = jnp.exp(s - m)
    return p / jnp.maximum(jnp.sum(p, axis=-1, keepdims=True), 1e-30)


def _retention(q, k, v, S0):
    B, T, H, D = q.shape
    c = math.gcd(T, RET_CHUNK)
    n = T // c
    lg = jnp.log1p(-jnp.exp2(-5.0 - jnp.arange(H, dtype=jnp.float32)))
    ii = jnp.arange(c, dtype=jnp.float32)
    rel = ii[:, None] - ii[None, :]
    dm = jnp.exp(jnp.where(rel[None] >= 0, rel[None] * lg[:, None, None], -jnp.inf))
    q_dec = jnp.exp((ii + 1.0)[:, None] * lg[None, :])
    k_dec = jnp.exp((c - 1.0 - ii)[:, None] * lg[None, :])
    s_dec = jnp.exp(c * lg)
    ch = lambda a: a.astype(jnp.float32).reshape(B, n, c, H, D).swapaxes(0, 1)

    def step(S, inp):
        qc, kc, vc = inp
        att = jnp.einsum('bihd,bjhd->bhij', qc, kc) * dm
        o = jnp.einsum('bhij,bjhe->bihe', att, vc) + jnp.einsum('bihd,bhde->bihe', qc, S) * q_dec[None, :, :, None]
        S = S * s_dec[None, :, None, None] + jnp.einsum('bjhd,bjhe->bhde', kc * k_dec[None, :, :, None], vc)
        return S, o

    S, o = lax.scan(step, S0.astype(jnp.float32), (ch(q), ch(k), ch(v)))
    return o.swapaxes(0, 1).reshape(B, T, H, D), S


def _gdn_chunked(q, k, v, beta, g, S0):
    B, T, H, D = q.shape
    c = math.gcd(T, GDN_CHUNK)
    n = T // c
    ch = lambda a: a.reshape(B, n, c, *a.shape[2:]).swapaxes(0, 1)
    ii = jnp.arange(c)
    tri = ii[:, None] >= ii[None, :]
    strict = ii[:, None] > ii[None, :]
    eye = jnp.eye(c, dtype=jnp.float32)

    def step(S, inp):
        qc, kc, vc, bc, gc = inp
        G = jnp.cumsum(gc, axis=1)
        Gh = jnp.swapaxes(G, 1, 2)
        L = jnp.exp(jnp.where(tri, Gh[..., :, None] - Gh[..., None, :], -jnp.inf))
        kb = kc * bc[..., None]
        A = eye + jnp.where(strict, jnp.einsum('bihd,bjhd->bhij', kb, kc) * L, 0.0)
        Tinv = lax.linalg.triangular_solve(A, jnp.broadcast_to(eye, A.shape), left_side=True, lower=True)
        u = jnp.einsum('bhij,bjhe->bihe', Tinv, vc * bc[..., None])
        w = jnp.einsum('bhij,bjhd->bihd', Tinv, kb * jnp.exp(G)[..., None])
        v_new = u - jnp.einsum('bihd,bhde->bihe', w, S)
        o = (jnp.einsum('bihd,bhde->bihe', qc * jnp.exp(G)[..., None], S)
             + jnp.einsum('bhij,bjhe->bihe', jnp.einsum('bihd,bjhd->bhij', qc, kc) * L, v_new))
        G_last = G[:, -1]
        S = (S * jnp.exp(G_last)[..., None, None]
             + jnp.einsum('bjhd,bjhe->bhde', kc * jnp.exp(G_last[:, None] - G)[..., None], v_new))
        return S, o

    S, o = lax.scan(step, S0.astype(jnp.float32), (ch(q), ch(k), ch(v), ch(beta), ch(g)))
    return o.swapaxes(0, 1).reshape(B, T, H, D), S


def _gated_delta(qkv_raw, beta_raw, a_raw, conv_buf, S0, conv_w, a_log, dt_bias):
    B, T, C = qkv_raw.shape
    xcat = jnp.concatenate([conv_buf.astype(qkv_raw.dtype), qkv_raw], axis=1)
    y = lax.conv_general_dilated(xcat, conv_w[:, None, :].astype(xcat.dtype), (1,), 'VALID',
                                 dimension_numbers=('NWC', 'WIO', 'NWC'), feature_group_count=C)
    y = jax.nn.silu(y.astype(jnp.float32))
    q, k, v = [a.reshape(B, T, GDN_HEADS, HEAD_DIM) for a in jnp.split(y, 3, axis=-1)]
    q = _l2norm(q) * (HEAD_DIM ** -0.5)
    k = _l2norm(k)
    beta = jax.nn.sigmoid(beta_raw.astype(jnp.float32))
    g = -jnp.exp(a_log.astype(jnp.float32)) * jax.nn.softplus(a_raw.astype(jnp.float32) + dt_bias.astype(jnp.float32))
    o, S = _gdn_chunked(q, k, v, beta, g, S0)
    return o, S, xcat[:, -(CONV_WIDTH - 1):]


def _compress(rows, w1, b1, w2):
    B, L, G, D = rows.shape
    n_seg = L // CMP_STRIDE
    r = CMP_BLOCK // CMP_STRIDE
    n_cmp = n_seg - r + 1
    seg = rows[:, :n_seg * CMP_STRIDE].reshape(B, n_seg, CMP_STRIDE, G, D)
    h = b1
    for i in range(r):
        h = h + jnp.einsum('bnsgd,sde->bnge', seg[:, i:i + n_cmp], w1[i * CMP_STRIDE:(i + 1) * CMP_STRIDE])
    return jnp.einsum('bnge,ef->bngf', jax.nn.silu(h), w2)


def _nsa_compressed(q, qpos, k_c, v_c):
    B, T, H, D = q.shape
    nc = k_c.shape[1]
    qg = q.reshape(B, T, NSA_KV_HEADS, NSA_GROUP, D)
    s = jnp.einsum('btgrd,bngd->btgrn', qg, k_c).astype(jnp.float32) * (HEAD_DIM ** -0.5)
    end = jnp.arange(nc) * CMP_STRIDE + (CMP_BLOCK - 1)
    mask = (end[None, :] <= qpos[:, None])[None, :, None, None, :]
    p = _masked_softmax(s, mask)
    o = jnp.einsum('btgrn,bngd->btgrd', p.astype(v_c.dtype), v_c).reshape(B, T, H, D)
    return o, jnp.sum(p, axis=3)


def _nsa_select(imp, qpos, n_blk):
    nc = imp.shape[-1]
    ci = jnp.arange(nc)[:, None] * CMP_STRIDE
    sj = jnp.arange(n_blk)[None, :] * SEL_BLOCK
    overlap = ((ci < sj + SEL_BLOCK) & (ci + CMP_BLOCK > sj)).astype(jnp.float32)
    score = jnp.einsum('btgn,nj->btgj', imp, overlap)
    jt = (qpos // SEL_BLOCK)[:, None]
    jj = jnp.arange(n_blk)[None, :]
    valid = jj <= jt
    forced = valid & ((jj == 0) | (jj == jt) | (jj == jt - 1))
    score = jnp.where(forced[None, :, None, :], jnp.inf, jnp.where(valid[None, :, None, :], score, -jnp.inf))
    _, idx = lax.top_k(score, min(N_SELECT, n_blk))
    return idx


def _block_positions(idx):
    return idx[..., None] * SEL_BLOCK + jnp.arange(SEL_BLOCK)


def _nsa_sel_attend(q, qpos, kg, vg, kpos):
    B, T, H, D = q.shape
    G = NSA_KV_HEADS
    kg = kg.reshape(B, T, G, -1, D)
    vg = vg.reshape(B, T, G, -1, D)
    kpos = kpos.reshape(B, T, G, -1)
    qg = q.reshape(B, T, G, NSA_GROUP, D)
    s = jnp.einsum('btgrd,btgld->btgrl', qg, kg).astype(jnp.float32) * (HEAD_DIM ** -0.5)
    mask = (kpos <= qpos[None, :, None, None])[:, :, :, None, :]
    p = _masked_softmax(s, mask)
    return jnp.einsum('btgrl,btgld->btgrd', p.astype(vg.dtype), vg).reshape(B, T, H, D)


def _window_attend(q, qpos, k, v, kpos):
    B, Tq, H, D = q.shape
    qg = q.reshape(B, Tq, NSA_KV_HEADS, NSA_GROUP, D)
    s = jnp.einsum('btgrd,blgd->btgrl', qg, k).astype(jnp.float32) * (HEAD_DIM ** -0.5)
    dist = qpos[:, None] - kpos[None, :]
    mask = ((dist >= 0) & (dist < WINDOW) & (kpos[None, :] >= 0))[None, :, None, None, :]
    p = _masked_softmax(s, mask)
    return jnp.einsum('btgrl,blgd->btgrd', p.astype(v.dtype), v).reshape(B, Tq, H, D)


def _nsa_prompt(q, q_rot, rows, win_rows, pos, *, cmp):
    B, T, H, D = q.shape
    G = NSA_KV_HEADS
    w1, b1, w2 = cmp
    k_c = _compress(rows[:, :, 0], w1[0], b1[0], w2[0])
    v_c = _compress(rows[:, :, 1], w1[1], b1[1], w2[1])
    o_cmp, imp = _nsa_compressed(q, pos, k_c, v_c)
    idx = _nsa_select(imp, pos, -(-T // SEL_BLOCK))
    cq = math.gcd(T, SEL_QBLK)
    nq = T // cq
    chunk = lambda a: a.reshape(B, nq, cq, *a.shape[2:]).swapaxes(0, 1)
    bi = jnp.arange(B)[:, None, None, None, None]
    gi = jnp.arange(G)[None, None, :, None, None]

    def sel_chunk(inp):
        qc, pc, ic = inp
        kpos = _block_positions(ic)
        r = jnp.minimum(kpos, T - 1)
        kg = rows[bi, r, 2, gi]
        vg = rows[bi, r, 3, gi]
        return _nsa_sel_attend(qc, pc, kg, vg, kpos)

    o_sel = lax.map(sel_chunk, (chunk(q_rot), pos.reshape(nq, cq), chunk(idx)))
    o_sel = o_sel.swapaxes(0, 1).reshape(B, T, H, D)
    cw = math.gcd(T, WIN_QBLK)
    nb = T // cw
    pad = ((0, 0), (WINDOW, 0), (0, 0), (0, 0))
    kpad = jnp.pad(win_rows[:, :, 0], pad)
    vpad = jnp.pad(win_rows[:, :, 1], pad)

    def win_chunk(inp):
        b, qc = inp
        s0 = b * cw
        kc = lax.dynamic_slice_in_dim(kpad, s0, WINDOW + cw, axis=1)
        vc = lax.dynamic_slice_in_dim(vpad, s0, WINDOW + cw, axis=1)
        kpos = s0 - WINDOW + jnp.arange(WINDOW + cw)
        return _window_attend(qc, s0 + jnp.arange(cw), kc, vc, kpos)

    qb = q_rot.reshape(B, nb, cw, H, D).swapaxes(0, 1)
    o_win = lax.map(win_chunk, (jnp.arange(nb), qb)).swapaxes(0, 1).reshape(B, T, H, D)
    return o_cmp, o_sel, o_win, win_rows[:, T - min(WINDOW, T):]


def _nsa_sample(q, q_rot, rows, win_rows, pos, *, cmp, cache, layer, page_table, win_buf):
    B, T, H, D = q.shape
    G = NSA_KV_HEADS
    w1, b1, w2 = cmp
    past_len = page_table.shape[1] * PAGE_SIZE
    past = cache[layer, page_table, :, :2]
    past = past.reshape(B, past_len, 2, G, D).astype(rows.dtype)
    cmp_rows = jnp.concatenate([past, rows[:, :, :2]], axis=1)
    k_c = _compress(cmp_rows[:, :, 0], w1[0], b1[0], w2[0])
    v_c = _compress(cmp_rows[:, :, 1], w1[1], b1[1], w2[1])
    o_cmp, imp = _nsa_compressed(q, pos, k_c, v_c)
    L = past_len + T
    idx = _nsa_select(imp, pos, -(-L // SEL_BLOCK))
    kpos = _block_positions(idx)
    bi = jnp.arange(B)[:, None, None, None, None]
    gi = jnp.arange(G)[None, None, :, None, None]
    pp = jnp.minimum(kpos, past_len - 1)
    phys = page_table[bi, pp // PAGE_SIZE]
    off = pp % PAGE_SIZE
    k_past = cache[layer, phys, off, 2, gi].astype(rows.dtype)
    v_past = cache[layer, phys, off, 3, gi].astype(rows.dtype)
    nr = jnp.clip(kpos - past_len, 0, T - 1)
    is_new = (kpos >= past_len)[..., None]
    kg = jnp.where(is_new, rows[bi, nr, 2, gi], k_past)
    vg = jnp.where(is_new, rows[bi, nr, 3, gi], v_past)
    o_sel = _nsa_sel_attend(q_rot, pos, kg, vg, kpos)
    wb = win_buf.shape[1]
    keys = jnp.concatenate([win_buf.astype(win_rows.dtype), win_rows], axis=1)
    kpos_w = past_len - wb + jnp.arange(wb + T)
    o_win = _window_attend(q_rot, pos, keys[:, :, 0], keys[:, :, 1], kpos_w)
    return o_cmp, o_sel, o_win, keys[:, -wb:]


def _layer(x, pos, lp, nsa_fn, ret_state, gdn_state, conv_buf):
    B, T, _ = x.shape
    h = _rmsnorm(x, lp['norm_pre'])
    pr = _split_proj(h @ lp['w_in'])
    heads = lambda a: a.reshape(B, T, -1, HEAD_DIM)
    rq = _rope(heads(pr['ret_q']), pos)
    rk = _rope(heads(pr['ret_k']), pos) * (HEAD_DIM ** -0.5)
    ro, ret_new = _retention(rq, rk, heads(pr['ret_v']), ret_state)
    ro = _head_layernorm(ro, lp['ret_gn']) * jax.nn.silu(pr['ret_g'].astype(jnp.float32))
    q = heads(pr['nsa_q'])
    kv = pr['nsa_kv'].reshape(B, T, 6, NSA_KV_HEADS, HEAD_DIM)
    rows = jnp.stack([kv[:, :, 0], kv[:, :, 1], _rope(kv[:, :, 2], pos), kv[:, :, 3]], axis=2)
    win_rows = jnp.stack([_rope(kv[:, :, 4], pos), kv[:, :, 5]], axis=2)
    o_cmp, o_sel, o_win, win_new = nsa_fn(q, _rope(q, pos), rows, win_rows, pos)
    gate = jax.nn.sigmoid(pr['nsa_gate'].astype(jnp.float32)).reshape(B, T, 3, NSA_HEADS, 1)
    no = gate[:, :, 0] * o_cmp + gate[:, :, 1] * o_sel + gate[:, :, 2] * o_win
    no = no.reshape(B, T, NSA_WIDTH) * jax.nn.silu(pr['nsa_g'].astype(jnp.float32))
    go, gdn_new, conv_new = _gated_delta(pr['gdn_qkv'], pr['gdn_beta'], pr['gdn_a'], conv_buf, gdn_state,
                                         lp['gdn_conv'], lp['gdn_a_log'], lp['gdn_dt_bias'])
    go = _head_rmsnorm(go, lp['gdn_norm']) * jax.nn.silu(pr['gdn_g'].astype(jnp.float32))
    mix = jnp.concatenate([ro, no, go], axis=-1).astype(x.dtype)
    y = x + _rmsnorm(mix @ lp['w_out'], lp['norm_post'])
    return y, (rows, win_new, ret_new, gdn_new, conv_new)


def setup_inputs(seed: int = 0) -> dict:
    key = jax.random.key(seed)
    ks = jax.random.split(key, 24)
    n_pages = PAST_LEN // PAGE_SIZE
    n_used = DEC_BATCH * n_pages
    n_phys = (5 * n_used + 3) // 4
    win_buf = min(WINDOW, PAST_LEN)
    nrm = lambda k, shape, s: jax.random.normal(k, shape, jnp.float32) * s
    dt = jnp.exp(jax.random.uniform(ks[16], (DEPTH, GDN_HEADS), jnp.float32, math.log(1e-3), math.log(1e-1)))
    return {
        'x_prompt': nrm(ks[0], (BATCH, SEQ, D_MODEL), 1.0),
        'x_sample': nrm(ks[1], (DEC_BATCH, DEC_SEQ, D_MODEL), 1.0),
        'cache_nsa_kv': nrm(ks[2], (DEPTH, n_phys, PAGE_SIZE, N_CACHE_SLOTS, NSA_KV_HEADS, HEAD_DIM), 1.0),
        'page_table': jax.random.permutation(ks[3], n_phys)[:n_used].reshape(DEC_BATCH, n_pages).astype(jnp.int32),
        'state_nsa_win': nrm(ks[4], (DEPTH, DEC_BATCH, win_buf, 2, NSA_KV_HEADS, HEAD_DIM), 1.0),
        'state_ret': nrm(ks[5], (DEPTH, DEC_BATCH, RET_HEADS, HEAD_DIM, HEAD_DIM), 0.5),
        'state_gdn': nrm(ks[6], (DEPTH, DEC_BATCH, GDN_HEADS, HEAD_DIM, HEAD_DIM), 0.1),
        'state_gdn_conv': nrm(ks[7], (DEPTH, DEC_BATCH, CONV_WIDTH - 1, 3 * GDN_WIDTH), 1.0),
        'w_in': nrm(ks[8], (DEPTH, D_MODEL, PROJ_WIDTH), D_MODEL ** -0.5),
        'w_out': nrm(ks[9], (DEPTH, MIX_WIDTH, D_MODEL), MIX_WIDTH ** -0.5),
        'norm_pre': 1.0 + nrm(ks[10], (DEPTH, D_MODEL), 0.02),
        'norm_post': 1.0 + nrm(ks[11], (DEPTH, D_MODEL), 0.02),
        'ret_gn': 1.0 + nrm(ks[12], (DEPTH, RET_WIDTH), 0.02),
        'gdn_norm': 1.0 + nrm(ks[13], (DEPTH, HEAD_DIM), 0.02),
        'gdn_conv': nrm(ks[14], (DEPTH, CONV_WIDTH, 3 * GDN_WIDTH), CONV_WIDTH ** -0.5),
        'gdn_a_log': jnp.log(jax.random.uniform(ks[15], (DEPTH, GDN_HEADS), jnp.float32, 1.0, 16.0)),
        'gdn_dt_bias': dt + jnp.log(-jnp.expm1(-dt)),
        'cmp_w1': nrm(ks[17], (DEPTH, 2, CMP_BLOCK, HEAD_DIM, HEAD_DIM), (CMP_BLOCK * HEAD_DIM) ** -0.5),
        'cmp_b1': nrm(ks[18], (DEPTH, 2, HEAD_DIM), 0.02),
        'cmp_w2': nrm(ks[19], (DEPTH, 2, HEAD_DIM, HEAD_DIM), HEAD_DIM ** -0.5),
    }


def reference(x_prompt, x_sample, cache_nsa_kv, page_table, state_nsa_win, state_ret, state_gdn, state_gdn_conv,
              w_in, w_out, norm_pre, norm_post, ret_gn, gdn_norm, gdn_conv, gdn_a_log, gdn_dt_bias,
              cmp_w1, cmp_b1, cmp_w2):
    Bp, T, _ = x_prompt.shape
    Bs, Ts, _ = x_sample.shape
    past_len = page_table.shape[1] * PAGE_SIZE
    pos_p = jnp.arange(T, dtype=jnp.int32)
    pos_s = past_len + jnp.arange(Ts, dtype=jnp.int32)
    yp, ys = x_prompt, x_sample
    st_p, st_s = [], []
    for l in range(DEPTH):
        lp = {'w_in': w_in[l], 'w_out': w_out[l], 'norm_pre': norm_pre[l], 'norm_post': norm_post[l],
              'ret_gn': ret_gn[l], 'gdn_norm': gdn_norm[l], 'gdn_conv': gdn_conv[l],
              'gdn_a_log': gdn_a_log[l], 'gdn_dt_bias': gdn_dt_bias[l]}
        cmp = (cmp_w1[l], cmp_b1[l], cmp_w2[l])
        yp, sp = _layer(yp, pos_p, lp, functools.partial(_nsa_prompt, cmp=cmp),
                        jnp.zeros((Bp, RET_HEADS, HEAD_DIM, HEAD_DIM), jnp.float32),
                        jnp.zeros((Bp, GDN_HEADS, HEAD_DIM, HEAD_DIM), jnp.float32),
                        jnp.zeros((Bp, CONV_WIDTH - 1, 3 * GDN_WIDTH), x_prompt.dtype))
        ys, ss = _layer(ys, pos_s, lp,
                        functools.partial(_nsa_sample, cmp=cmp, cache=cache_nsa_kv, layer=l,
                                          page_table=page_table, win_buf=state_nsa_win[l]),
                        state_ret[l], state_gdn[l], state_gdn_conv[l])
        st_p.append(sp)
        st_s.append(ss)
    stk = lambda sts, i: jnp.stack([s[i] for s in sts])
    return (yp, ys, stk(st_p, 0), stk(st_s, 0), stk(st_p, 1), stk(st_s, 1), stk(st_p, 2), stk(st_s, 2),
            stk(st_p, 3), stk(st_s, 3), stk(st_p, 4), stk(st_s, 4))
```

```python
import functools
import math

import numpy as np
import jax
import jax.numpy as jnp
from jax import lax
from jax.experimental import pallas as pl
from jax.experimental.pallas import tpu as pltpu

f32 = jnp.float32
bf16 = jnp.bfloat16

D_MODEL = 1024
HEAD_DIM = 64
RET_WIDTH = 256
NSA_WIDTH = 512
GDN_WIDTH = 256
RET_HEADS = 4
NSA_HEADS = 8
NSA_KV_HEADS = 2
NSA_GROUP = 4
GDN_HEADS = 4
CMP_BLOCK = 32
CMP_STRIDE = 16
SEL_BLOCK = 64
N_SELECT = 16
WINDOW = 512
RET_CHUNK = 64
GDN_CHUNK = 64
CONV_WIDTH = 4
PAGE_SIZE = 128
ROPE_THETA = 10000.0
EPS = 1e-6

LANES = 128
VMEM_LIMIT = 56 * 1024 * 1024

_SPLITS = (('ret', 4 * RET_WIDTH), ('nsa_q', NSA_WIDTH), ('nsa_kv', 6 * NSA_KV_HEADS * HEAD_DIM),
           ('nsa_gate', 3 * NSA_HEADS), ('nsa_g', NSA_WIDTH), ('gdn_qkv', 3 * GDN_WIDTH),
           ('gdn_ba', 2 * GDN_HEADS), ('gdn_g', GDN_WIDTH))
_OFF = {}
_o = 0
for _n, _w in _SPLITS:
    _OFF[_n] = (_o, _o + _w)
    _o += _w
PROJ_WIDTH = _o
_PAD_ORDER = ('ret', 'nsa_q', 'nsa_kv', 'nsa_g', 'gdn_qkv', 'gdn_g', 'nsa_gate', 'gdn_ba')
_POFF = {}
_o = 0
for _n in _PAD_ORDER:
    _w = _OFF[_n][1] - _OFF[_n][0]
    _POFF[_n] = _o
    _o += _w
SMALL_OFF = _POFF['nsa_gate']
PROJ_PAD = SMALL_OFF + LANES
GATE_COL = 0
BETA_COL = 3 * NSA_HEADS
A_COL = BETA_COL + GDN_HEADS

_NT = (((1,), (1,)), ((), ()))
_TN = (((0,), (0,)), ((), ()))
_HI = lax.Precision.HIGHEST


def _silu(x):
    return x * jax.nn.sigmoid(x)


def _cparams(sem):
    return pltpu.CompilerParams(dimension_semantics=sem, vmem_limit_bytes=VMEM_LIMIT)


def _proj_body(x_ref, g_ref, w_ref, cos_ref, sin_ref,
               ret_ref, q_ref, qr_ref, kv_ref, win_ref, ng_ref, gq_ref, gg_ref, sm_ref):
    x = x_ref[...]
    h = x * lax.rsqrt(jnp.mean(x * x, axis=-1, keepdims=True) + EPS) * g_ref[...]
    hb = h.astype(bf16)
    cos = cos_ref[...]
    sin = sin_ref[...]
    lane = lax.broadcasted_iota(jnp.int32, cos.shape, 1)
    low = (lane % HEAD_DIM) < HEAD_DIM // 2

    def mm(c0, width):
        return jnp.dot(hb, w_ref[:, c0:c0 + width], preferred_element_type=f32)

    def rope(v):
        sw = jnp.where(low, pltpu.roll(v, LANES - HEAD_DIM // 2, 1), pltpu.roll(v, HEAD_DIM // 2, 1))
        return v * cos + sw * sin

    r = mm(_POFF['ret'], 4 * RET_WIDTH)
    for j in range(2):
        ret_ref[:, j * LANES:(j + 1) * LANES] = rope(r[:, j * LANES:(j + 1) * LANES])
    for j in range(2, 4):
        ret_ref[:, j * LANES:(j + 1) * LANES] = rope(r[:, j * LANES:(j + 1) * LANES]) * (HEAD_DIM ** -0.5)
    ret_ref[:, 2 * RET_WIDTH:] = r[:, 2 * RET_WIDTH:]

    q = mm(_POFF['nsa_q'], NSA_WIDTH)
    q_ref[...] = q
    for j in range(NSA_WIDTH // LANES):
        qr_ref[:, j * LANES:(j + 1) * LANES] = rope(q[:, j * LANES:(j + 1) * LANES])

    kv = mm(_POFF['nsa_kv'], 6 * LANES)
    kv_ref[:, 0:2 * LANES] = kv[:, 0:2 * LANES]
    kv_ref[:, 2 * LANES:3 * LANES] = rope(kv[:, 2 * LANES:3 * LANES])
    kv_ref[:, 3 * LANES:4 * LANES] = kv[:, 3 * LANES:4 * LANES]
    win_ref[:, 0:LANES] = rope(kv[:, 4 * LANES:5 * LANES])
    win_ref[:, LANES:2 * LANES] = kv[:, 5 * LANES:6 * LANES]

    ng_ref[...] = mm(_POFF['nsa_g'], NSA_WIDTH)
    gq_ref[...] = mm(_POFF['gdn_qkv'], 3 * GDN_WIDTH)
    gg_ref[...] = mm(_POFF['gdn_g'], GDN_WIDTH)
    sm_ref[...] = mm(SMALL_OFF, LANES)


def _proj(x2d, gain, wpad, cosf, sinf, tm):
    M = x2d.shape[0]
    n_pos = cosf.shape[0] // tm
    widths = (4 * RET_WIDTH, NSA_WIDTH, NSA_WIDTH, 4 * LANES, 2 * LANES, NSA_WIDTH, 3 * GDN_WIDTH, GDN_WIDTH, LANES)
    row = lambda w: pl.BlockSpec((tm, w), lambda i: (i, 0))
    return pl.pallas_call(
        _proj_body,
        grid=(M // tm,),
        in_specs=[row(D_MODEL),
                  pl.BlockSpec((1, D_MODEL), lambda i: (0, 0)),
                  pl.BlockSpec((D_MODEL, PROJ_PAD), lambda i: (0, 0)),
                  pl.BlockSpec((tm, LANES), lambda i: (i % n_pos, 0)),
                  pl.BlockSpec((tm, LANES), lambda i: (i % n_pos, 0))],
        out_specs=[row(w) for w in widths],
        out_shape=[jax.ShapeDtypeStruct((M, w), f32) for w in widths],
        compiler_params=_cparams(("parallel",)),
        name="proj",
    )(x2d, gain, wpad, cosf, sinf)


def _ret_body(ret_ref, dm_ref, qd_ref, kd_ref, gn_ref, s0_ref, o_ref, s_ref, *, c, nchunk, sdec):
    @pl.when(pl.program_id(1) == 0)
    def _():
        s_ref[...] = s0_ref[...]

    for ci in range(nchunk):
        blk = ret_ref[0, ci * c:(ci + 1) * c, :]
        outs = []
        for h in range(RET_HEADS):
            sl = lambda base: blk[:, base + h * HEAD_DIM: base + (h + 1) * HEAD_DIM]
            q, k, v, g = sl(0), sl(RET_WIDTH), sl(2 * RET_WIDTH), sl(3 * RET_WIDTH)
            S = s_ref[0, h]
            qb, vb = q.astype(bf16), v.astype(bf16)
            att = lax.dot_general(qb, k.astype(bf16), _NT, preferred_element_type=f32) * dm_ref[h]
            o = (jnp.dot(att.astype(bf16), vb, preferred_element_type=f32)
                 + jnp.dot(qb, S.astype(bf16), preferred_element_type=f32) * qd_ref[:, h:h + 1])
            kd = (k * kd_ref[:, h:h + 1]).astype(bf16)
            s_ref[0, h] = S * sdec[h] + lax.dot_general(kd, vb, _TN, preferred_element_type=f32)
            mu = jnp.mean(o, axis=-1, keepdims=True)
            d = o - mu
            var = jnp.mean(d * d, axis=-1, keepdims=True)
            y = d * lax.rsqrt(var + EPS) * gn_ref[:, h * HEAD_DIM:(h + 1) * HEAD_DIM] * _silu(g)
            outs.append(y)
        o_ref[0, ci * c:(ci + 1) * c, :] = jnp.concatenate(outs, axis=1)


def _retention(ret3, s0, gn, T):
    B = ret3.shape[0]
    c = math.gcd(T, RET_CHUNK)
    nchunk = min(T // c, 4)
    tb = c * nchunk
    hh = jnp.arange(RET_HEADS, dtype=f32)
    lg = jnp.log1p(-jnp.exp2(-5.0 - hh))
    ii = jnp.arange(c, dtype=f32)
    rel = ii[:, None] - ii[None, :]
    dm = jnp.exp(jnp.where(rel[None] >= 0, rel[None] * lg[:, None, None], -jnp.inf))
    qd = jnp.exp((ii + 1.0)[:, None] * lg[None, :])
    kd = jnp.exp((c - 1.0 - ii)[:, None] * lg[None, :])
    sdec = tuple(float((1.0 - 2.0 ** (-5.0 - h)) ** c) for h in range(RET_HEADS))
    return pl.pallas_call(
        functools.partial(_ret_body, c=c, nchunk=nchunk, sdec=sdec),
        grid=(B, T // tb),
        in_specs=[pl.BlockSpec((1, tb, 4 * RET_WIDTH), lambda b, j: (b, j, 0)),
                  pl.BlockSpec((RET_HEADS, c, c), lambda b, j: (0, 0, 0)),
                  pl.BlockSpec((c, RET_HEADS), lambda b, j: (0, 0)),
                  pl.BlockSpec((c, RET_HEADS), lambda b, j: (0, 0)),
                  pl.BlockSpec((1, RET_WIDTH), lambda b, j: (0, 0)),
                  pl.BlockSpec((1, RET_HEADS, HEAD_DIM, HEAD_DIM), lambda b, j: (b, 0, 0, 0))],
        out_specs=[pl.BlockSpec((1, tb, RET_WIDTH), lambda b, j: (b, j, 0)),
                   pl.BlockSpec((1, RET_HEADS, HEAD_DIM, HEAD_DIM), lambda b, j: (b, 0, 0, 0))],
        out_shape=[jax.ShapeDtypeStruct((B, T, RET_WIDTH), f32),
                   jax.ShapeDtypeStruct((B, RET_HEADS, HEAD_DIM, HEAD_DIM), f32)],
        compiler_params=_cparams(("parallel", "arbitrary")),
        name="retention",
    )(ret3, dm, qd, kd, gn, s0)


def _softplus(x):
    return jnp.maximum(x, 0.0) + jnp.log1p(jnp.exp(-jnp.abs(x)))


def _unit_lower_inverse(n_strict, c):
    rr = lax.broadcasted_iota(jnp.int32, (c, c), 0)
    cc = lax.broadcasted_iota(jnp.int32, (c, c), 1)
    eye = (rr == cc).astype(f32)
    pw = -n_strict
    res = eye + pw
    span = 2
    while span < c:
        pw = jnp.dot(pw, pw, preferred_element_type=f32, precision=_HI)
        res = res + jnp.dot(res, pw, preferred_element_type=f32, precision=_HI)
        span *= 2
    return res


def _gdn_body(x_ref, sm_ref, gg_ref, cb_ref, cw_ref, al_ref, dtb_ref, gnorm_ref, s0_ref,
              o_ref, s_ref, xc_ref, *, c):
    @pl.when(pl.program_id(1) == 0)
    def _():
        s_ref[...] = s0_ref[...]
        xc_ref[5:8, :] = cb_ref[0]

    xc_ref[8:8 + c, :] = x_ref[0]
    y = cw_ref[0:1, :] * xc_ref[pl.ds(5, c), :]
    for j in range(1, CONV_WIDTH):
        y = y + cw_ref[j:j + 1, :] * xc_ref[pl.ds(5 + j, c), :]
    tail = xc_ref[pl.ds(5 + c, 3), :]
    xc_ref[5:8, :] = tail
    y = _silu(y)

    sm = sm_ref[0]
    beta_all = jax.nn.sigmoid(sm)
    g_all = -jnp.exp(al_ref[...]) * _softplus(sm + dtb_ref[...])
    rr = lax.broadcasted_iota(jnp.int32, (c, c), 0)
    cc = lax.broadcasted_iota(jnp.int32, (c, c), 1)
    tri = rr >= cc
    strict = rr > cc
    G_cols = jnp.dot(tri.astype(f32), g_all, preferred_element_type=f32, precision=_HI)
    G_rows = jnp.dot(g_all.T, (rr <= cc).astype(f32), preferred_element_type=f32, precision=_HI)
    gg = gg_ref[0]

    outs = []
    for h in range(GDN_HEADS):
        hs = lambda base: y[:, base + h * HEAD_DIM: base + (h + 1) * HEAD_DIM]
        q, k, v = hs(0), hs(GDN_WIDTH), hs(2 * GDN_WIDTH)
        q = q * lax.rsqrt(jnp.sum(q * q, axis=-1, keepdims=True) + EPS) * (HEAD_DIM ** -0.5)
        k = k * lax.rsqrt(jnp.sum(k * k, axis=-1, keepdims=True) + EPS)
        beta = beta_all[:, BETA_COL + h: BETA_COL + h + 1]
        Gc = G_cols[:, A_COL + h: A_COL + h + 1]
        Gr = G_rows[A_COL + h: A_COL + h + 1, :]
        L = jnp.exp(jnp.where(tri, Gc - Gr, -jnp.inf))
        S = s_ref[0, h]
        Sb = S.astype(bf16)
        kb = k * beta
        kbf = k.astype(bf16)
        n_strict = jnp.where(strict, lax.dot_general(kb.astype(bf16), kbf, _NT, preferred_element_type=f32) * L, 0.0)
        tinv = _unit_lower_inverse(n_strict, c).astype(bf16)
        eg = jnp.exp(Gc)
        u = jnp.dot(tinv, (v * beta).astype(bf16), preferred_element_type=f32)
        w = jnp.dot(tinv, (kb * eg).astype(bf16), preferred_element_type=f32)
        v_new = u - jnp.dot(w.astype(bf16), Sb, preferred_element_type=f32)
        vnb = v_new.astype(bf16)
        qk = lax.dot_general(q.astype(bf16), kbf, _NT, preferred_element_type=f32) * L
        o = (jnp.dot((q * eg).astype(bf16), Sb, preferred_element_type=f32)
             + jnp.dot(qk.astype(bf16), vnb, preferred_element_type=f32))
        g_last = Gc[c - 1:c, :]
        kdec = (k * jnp.exp(g_last - Gc)).astype(bf16)
        s_ref[0, h] = S * jnp.exp(g_last) + lax.dot_general(kdec, vnb, _TN, preferred_element_type=f32)
        yo = o * lax.rsqrt(jnp.mean(o * o, axis=-1, keepdims=True) + EPS) * gnorm_ref[...]
        outs.append(yo * _silu(gg[:, h * HEAD_DIM:(h + 1) * HEAD_DIM]))
    o_ref[0] = jnp.concatenate(outs, axis=1)


def _gdn(gq3, sm3, gg3, conv_buf, conv_w, a_log, dt_bias, gnorm, s0, T):
    B = gq3.shape[0]
    c = math.gcd(T, GDN_CHUNK)
    C = 3 * GDN_WIDTH
    pad = lambda v: jnp.zeros((1, LANES), f32).at[0, A_COL:A_COL + GDN_HEADS].set(v.astype(f32))
    return pl.pallas_call(
        functools.partial(_gdn_body, c=c),
        grid=(B, T // c),
        in_specs=[pl.BlockSpec((1, c, C), lambda b, j: (b, j, 0)),
                  pl.BlockSpec((1, c, LANES), lambda b, j: (b, j, 0)),
                  pl.BlockSpec((1, c, GDN_WIDTH), lambda b, j: (b, j, 0)),
                  pl.BlockSpec((1, CONV_WIDTH - 1, C), lambda b, j: (b, 0, 0)),
                  pl.BlockSpec((CONV_WIDTH, C), lambda b, j: (0, 0)),
                  pl.BlockSpec((1, LANES), lambda b, j: (0, 0)),
                  pl.BlockSpec((1, LANES), lambda b, j: (0, 0)),
                  pl.BlockSpec((1, HEAD_DIM), lambda b, j: (0, 0)),
                  pl.BlockSpec((1, GDN_HEADS, HEAD_DIM, HEAD_DIM), lambda b, j: (b, 0, 0, 0))],
        out_specs=[pl.BlockSpec((1, c, GDN_WIDTH), lambda b, j: (b, j, 0)),
                   pl.BlockSpec((1, GDN_HEADS, HEAD_DIM, HEAD_DIM), lambda b, j: (b, 0, 0, 0))],
        out_shape=[jax.ShapeDtypeStruct((B, T, GDN_WIDTH), f32),
                   jax.ShapeDtypeStruct((B, GDN_HEADS, HEAD_DIM, HEAD_DIM), f32)],
        scratch_shapes=[pltpu.VMEM((8 + c + 8, C), f32)],
        compiler_params=_cparams(("parallel", "arbitrary")),
        name="gdn",
    )(gq3, sm3, gg3, conv_buf, conv_w, pad(a_log), pad(dt_bias), gnorm, s0)


def _cmp_weights(w1, b1, w2):
    z = jnp.zeros((CMP_STRIDE, HEAD_DIM, HEAD_DIM), f32)
    nq = 2 * NSA_KV_HEADS
    def half(i):
        rows = []
        for qi in range(nq):
            blk = w1[qi // NSA_KV_HEADS, i * CMP_STRIDE:(i + 1) * CMP_STRIDE]
            rows.append(jnp.concatenate([blk if qj == qi else z for qj in range(nq)], axis=2))
        return jnp.concatenate(rows, axis=1)
    wc = jnp.concatenate([half(0), half(1)], axis=2).astype(bf16)
    b1t = jnp.concatenate([b1[qi // NSA_KV_HEADS] for qi in range(nq)])[None, :]
    z2 = jnp.zeros((HEAD_DIM, HEAD_DIM), f32)
    w2bd = jnp.concatenate([jnp.concatenate([w2[qi // NSA_KV_HEADS] if qj == qi else z2 for qj in range(nq)], axis=1)
                            for qi in range(nq)], axis=0).astype(bf16)
    return wc, b1t, w2bd


def _cmp_prompt_body(x_ref, wc_ref, b1_ref, w2_ref, o_ref, *, n_seg):
    acc = jnp.zeros((n_seg, 8 * HEAD_DIM), f32)
    for s in range(CMP_STRIDE):
        xs = x_ref[0, :, s * 8 * HEAD_DIM: s * 8 * HEAD_DIM + 4 * HEAD_DIM].astype(bf16)
        acc = acc + jnp.dot(xs, wc_ref[s], preferred_element_type=f32)
    first = acc[:, :4 * HEAD_DIM]
    second = pltpu.roll(acc[:, 4 * HEAD_DIM:], n_seg - 1, 0)
    hid = _silu(b1_ref[...] + first + second)
    out = jnp.dot(hid.astype(bf16), w2_ref[...], preferred_element_type=f32)
    for qi in range(2 * NSA_KV_HEADS):
        o_ref[0, qi] = out[:, qi * HEAD_DIM:(qi + 1) * HEAD_DIM]


def _compress_prompt(kv3, wc, b1t, w2bd):
    B, T, _ = kv3.shape
    n_seg = T // CMP_STRIDE
    xseg = kv3.reshape(B, n_seg, CMP_STRIDE * 8 * HEAD_DIM)
    return pl.pallas_call(
        functools.partial(_cmp_prompt_body, n_seg=n_seg),
        grid=(B,),
        in_specs=[pl.BlockSpec((1, n_seg, CMP_STRIDE * 8 * HEAD_DIM), lambda b: (b, 0, 0)),
                  pl.BlockSpec((CMP_STRIDE, 4 * HEAD_DIM, 8 * HEAD_DIM), lambda b: (0, 0, 0)),
                  pl.BlockSpec((1, 4 * HEAD_DIM), lambda b: (0, 0)),
                  pl.BlockSpec((4 * HEAD_DIM, 4 * HEAD_DIM), lambda b: (0, 0))],
        out_specs=pl.BlockSpec((1, 4, n_seg, HEAD_DIM), lambda b: (b, 0, 0, 0)),
        out_shape=jax.ShapeDtypeStruct((B, 4, n_seg, HEAD_DIM), f32),
        compiler_params=_cparams(("parallel",)),
        name="compress_prompt",
    )(xseg, wc, b1t, w2bd)


def _topk_rows(score, k):
    n = score.shape[0]
    idx = lax.broadcasted_iota(jnp.int32, score.shape, 0)
    taken = jnp.zeros(score.shape, jnp.bool_)
    for _ in range(k):
        work = jnp.where(taken, -jnp.inf, score)
        m = jnp.max(work, axis=0, keepdims=True)
        cand = jnp.where((work == m) & jnp.logical_not(taken), idx, n)
        first = jnp.min(cand, axis=0, keepdims=True)
        taken = taken | (idx == first)
    return jnp.where(taken, 1.0, 0.0)


def _softmax_rows(s, mask):
    s = jnp.where(mask, s, -jnp.inf)
    m = jnp.max(s, axis=-1, keepdims=True)
    m = jnp.where(m == -jnp.inf, 0.0, m)
    p = jnp.exp(s - m)
    return p / jnp.maximum(jnp.sum(p, axis=-1, keepdims=True), 1e-30)


def _flash(qs, k_ref, v_ref, c0, lo, hi, tk, tq, mask_fn):
    R = qs.shape[0]

    def step(kt, carry):
        m, l, acc = carry
        k0 = pl.multiple_of(kt * tk, tk)
        kblk = k_ref[0, pl.ds(k0, tk), :][:, c0:c0 + HEAD_DIM].astype(bf16)
        vblk = v_ref[0, pl.ds(k0, tk), :][:, c0:c0 + HEAD_DIM].astype(bf16)
        s = lax.dot_general(qs, kblk, _NT, preferred_element_type=f32)
        s = jnp.where(mask_fn(kt, k0)[None], s.reshape(NSA_GROUP, tq, tk), -jnp.inf).reshape(R, tk)
        m_new = jnp.maximum(m, jnp.max(s, axis=-1, keepdims=True))
        m_safe = jnp.where(m_new == -jnp.inf, 0.0, m_new)
        alpha = jnp.exp(m - m_safe)
        p = jnp.exp(s - m_safe)
        l = alpha * l + jnp.sum(p, axis=-1, keepdims=True)
        acc = alpha * acc + jnp.dot(p.astype(bf16), vblk, preferred_element_type=f32)
        return m_new, l, acc

    init = (jnp.full((R, 1), -jnp.inf, f32), jnp.zeros((R, 1), f32), jnp.zeros((R, HEAD_DIM), f32))
    m, l, acc = lax.fori_loop(lo, hi, step, init)
    return acc / jnp.maximum(l, 1e-30)


def _nsa_prompt_body(q_ref, qr_ref, kc_ref, ks_ref, vs_ref, kw_ref, vw_ref, sm_ref, ov_ref, ex_ref,
                     o_ref, mask_ref, *, tq, tk, tkw, T, n_cmp, n_blk):
    i = pl.program_id(1)
    t0 = i * tq
    scale = HEAD_DIM ** -0.5
    stack = lambda x: jnp.concatenate([x[:, r * HEAD_DIM:(r + 1) * HEAD_DIM] for r in range(NSA_GROUP)], axis=0)
    gates = jax.nn.sigmoid(sm_ref[0])
    n_pad = kc_ref.shape[2]
    tpos_r = t0 + lax.broadcasted_iota(jnp.int32, (tq, 1), 0)
    tpos_c = t0 + lax.broadcasted_iota(jnp.int32, (1, tq), 1)
    eye = (lax.broadcasted_iota(jnp.int32, (tq, tq), 0) == lax.broadcasted_iota(jnp.int32, (tq, tq), 1)).astype(bf16)

    for g in range(NSA_KV_HEADS):
        q = q_ref[0, :, g * NSA_GROUP * HEAD_DIM:(g + 1) * NSA_GROUP * HEAD_DIM]
        qr = qr_ref[0, :, g * NSA_GROUP * HEAD_DIM:(g + 1) * NSA_GROUP * HEAD_DIM]
        qs = (stack(q) * scale).astype(bf16)
        qrs = (stack(qr) * scale).astype(bf16)

        kc = kc_ref[0, g].astype(bf16)
        vc = kc_ref[0, NSA_KV_HEADS + g].astype(bf16)
        s = lax.dot_general(qs, kc, _NT, preferred_element_type=f32)
        nn = lax.broadcasted_iota(jnp.int32, (1, n_pad), 1)
        cmask = (nn * CMP_STRIDE + (CMP_BLOCK - 1) <= tpos_r) & (nn < n_cmp)
        p = _softmax_rows(s.reshape(NSA_GROUP, tq, n_pad), cmask[None])
        o_cmp = jnp.dot(p.reshape(NSA_GROUP * tq, n_pad).astype(bf16), vc, preferred_element_type=f32)
        imp = jnp.sum(p, axis=0)

        score = lax.dot_general(ov_ref[...], imp, _NT, preferred_element_type=f32, precision=_HI)[:n_blk]
        jj = lax.broadcasted_iota(jnp.int32, (n_blk, tq), 0)
        jt = tpos_c // SEL_BLOCK
        valid = jj <= jt
        forced = valid & ((jj == 0) | (jj == jt) | (jj == jt - 1))
        score = jnp.where(forced, jnp.inf, jnp.where(valid, score, -jnp.inf))
        sel_t = _topk_rows(score, min(N_SELECT, n_blk)).astype(bf16)
        if n_blk < LANES:
            sel_t = jnp.concatenate([sel_t, jnp.zeros((LANES - n_blk, tq), bf16)], axis=0)
        sel = lax.dot_general(eye, sel_t, _NT, preferred_element_type=f32)
        full = jnp.dot(sel.astype(bf16), ex_ref[...], preferred_element_type=f32)
        for kt in range(T // tk):
            mask_ref[kt] = full[:, kt * tk:(kt + 1) * tk]

        def sel_mask(kt, k0):
            kpos = k0 + lax.broadcasted_iota(jnp.int32, (1, tk), 1)
            return (mask_ref[kt] > 0.5) & (kpos <= tpos_r)

        def win_mask(kt, k0):
            dist = tpos_r - (k0 + lax.broadcasted_iota(jnp.int32, (1, tkw), 1))
            return (dist >= 0) & (dist < WINDOW)

        o_sel = _flash(qrs, ks_ref, vs_ref, g * HEAD_DIM, 0, (t0 + tq + tk - 1) // tk, tk, tq, sel_mask)
        w_lo = jnp.maximum(t0 - WINDOW, 0) // tkw
        o_win = _flash(qrs, kw_ref, vw_ref, g * HEAD_DIM, w_lo, (t0 + tq + tkw - 1) // tkw, tkw, tq, win_mask)

        outs = []
        for r in range(NSA_GROUP):
            hcol = GATE_COL + g * NSA_GROUP + r
            rows = slice(r * tq, (r + 1) * tq)
            outs.append(gates[:, hcol:hcol + 1] * o_cmp[rows]
                        + gates[:, NSA_HEADS + hcol:NSA_HEADS + hcol + 1] * o_sel[rows]
                        + gates[:, 2 * NSA_HEADS + hcol:2 * NSA_HEADS + hcol + 1] * o_win[rows])
        o_ref[0, :, g * NSA_GROUP * HEAD_DIM:(g + 1) * NSA_GROUP * HEAD_DIM] = jnp.concatenate(outs, axis=1)


def _overlap_t(n_blk_pad, n_cmp, n_pad):
    ci = np.arange(n_pad)[None, :] * CMP_STRIDE
    sj = np.arange(n_blk_pad)[:, None] * SEL_BLOCK
    ov = (ci < sj + SEL_BLOCK) & (ci + CMP_BLOCK > sj) & (np.arange(n_pad)[None, :] < n_cmp)
    return jnp.asarray(ov.astype(np.float32))


def _nsa_prompt(q3, qr3, kc, kv3, win3, sm3):
    B, T, _ = q3.shape
    tq, tk, tkw = 128, 256, 128
    n_seg = T // CMP_STRIDE
    n_cmp = n_seg - CMP_BLOCK // CMP_STRIDE + 1
    n_blk = -(-T // SEL_BLOCK)
    ov = _overlap_t(LANES, n_cmp, n_seg)
    ex = jnp.asarray((np.arange(LANES)[:, None] == (np.arange(T)[None, :] // SEL_BLOCK)).astype(np.float32)).astype(bf16)
    qspec = pl.BlockSpec((1, tq, NSA_WIDTH), lambda b, i: (b, i, 0))
    col = lambda j: pl.BlockSpec((1, T, LANES), lambda b, i: (b, 0, j))
    return pl.pallas_call(
        functools.partial(_nsa_prompt_body, tq=tq, tk=tk, tkw=tkw, T=T, n_cmp=n_cmp, n_blk=n_blk),
        grid=(B, T // tq),
        in_specs=[qspec, qspec,
                  pl.BlockSpec((1, 4, n_seg, HEAD_DIM), lambda b, i: (b, 0, 0, 0)),
                  col(2), col(3), col(0), col(1),
                  pl.BlockSpec((1, tq, LANES), lambda b, i: (b, i, 0)),
                  pl.BlockSpec((LANES, n_seg), lambda b, i: (0, 0)),
                  pl.BlockSpec((LANES, T), lambda b, i: (0, 0))],
        out_specs=qspec,
        out_shape=jax.ShapeDtypeStruct((B, T, NSA_WIDTH), f32),
        scratch_shapes=[pltpu.VMEM((T // tk, tq, tk), f32)],
        compiler_params=_cparams(("parallel", "arbitrary")),
        name="nsa_prompt",
    )(q3, qr3, kc, kv3, kv3, win3, win3, sm3, ov, ex)


def _out_body(x_ref, ro_ref, no_ref, ng_ref, go_ref, w_ref, g_ref, y_ref):
    no = no_ref[...] * _silu(ng_ref[...])
    z = (jnp.dot(ro_ref[...].astype(bf16), w_ref[0:RET_WIDTH, :], preferred_element_type=f32)
         + jnp.dot(no.astype(bf16), w_ref[RET_WIDTH:RET_WIDTH + NSA_WIDTH, :], preferred_element_type=f32)
         + jnp.dot(go_ref[...].astype(bf16), w_ref[RET_WIDTH + NSA_WIDTH:, :], preferred_element_type=f32))
    y_ref[...] = x_ref[...] + z * lax.rsqrt(jnp.mean(z * z, axis=-1, keepdims=True) + EPS) * g_ref[...]


def _out(x2d, ro, no, ng, go, w_out_b, gain, tm):
    M = x2d.shape[0]
    row = lambda w: pl.BlockSpec((tm, w), lambda i: (i, 0))
    return pl.pallas_call(
        _out_body,
        grid=(M // tm,),
        in_specs=[row(D_MODEL), row(RET_WIDTH), row(NSA_WIDTH), row(NSA_WIDTH), row(GDN_WIDTH),
                  pl.BlockSpec((D_MODEL, D_MODEL), lambda i: (0, 0)),
                  pl.BlockSpec((1, D_MODEL), lambda i: (0, 0))],
        out_specs=row(D_MODEL),
        out_shape=jax.ShapeDtypeStruct((M, D_MODEL), f32),
        compiler_params=_cparams(("parallel",)),
        name="out",
    )(x2d, ro, no, ng, go, w_out_b, gain)


def _masked_softmax(s, mask):
    s = jnp.where(mask, s, -jnp.inf)
    m = jnp.max(s, axis=-1, keepdims=True)
    m = jnp.where(jnp.isfinite(m), m, 0.0)
    p = jnp.exp(s - m)
    return p / jnp.maximum(jnp.sum(p, axis=-1, keepdims=True), 1e-30)


def _compress_jax(rows, w1, b1, w2):
    B, L, G, D = rows.shape
    n_seg = L // CMP_STRIDE
    r = CMP_BLOCK // CMP_STRIDE
    n_cmp = n_seg - r + 1
    seg = rows[:, :n_seg * CMP_STRIDE].reshape(B, n_seg, CMP_STRIDE, G, D)
    h = b1
    for i in range(r):
        h = h + jnp.einsum('bnsgd,sde->bnge', seg[:, i:i + n_cmp], w1[i * CMP_STRIDE:(i + 1) * CMP_STRIDE])
    return jnp.einsum('bnge,ef->bngf', jax.nn.silu(h), w2)


def _sel_attend_jax(q, qpos, kg, vg, kpos):
    B, T, H, D = q.shape
    G = NSA_KV_HEADS
    kg = kg.reshape(B, T, G, -1, D)
    vg = vg.reshape(B, T, G, -1, D)
    kpos = kpos.reshape(B, T, G, -1)
    qg = q.reshape(B, T, G, NSA_GROUP, D)
    s = jnp.einsum('btgrd,btgld->btgrl', qg, kg).astype(f32) * (HEAD_DIM ** -0.5)
    mask = (kpos <= qpos[None, :, None, None])[:, :, :, None, :]
    p = _masked_softmax(s, mask)
    return jnp.einsum('btgrl,btgld->btgrd', p, vg).reshape(B, T, H, D)


def _window_attend_jax(q, qpos, k, v, kpos):
    B, Tq, H, D = q.shape
    qg = q.reshape(B, Tq, NSA_KV_HEADS, NSA_GROUP, D)
    s = jnp.einsum('btgrd,blgd->btgrl', qg, k).astype(f32) * (HEAD_DIM ** -0.5)
    dist = qpos[:, None] - kpos[None, :]
    mask = ((dist >= 0) & (dist < WINDOW) & (kpos[None, :] >= 0))[None, :, None, None, :]
    p = _masked_softmax(s, mask)
    return jnp.einsum('btgrl,blgd->btgrd', p, v).reshape(B, Tq, H, D)


def _nsa_sample_jax(q, q_rot, rows, win_rows, pos, cmp, cache, layer, page_table, win_buf):
    B, T, H, D = q.shape
    G = NSA_KV_HEADS
    w1, b1, w2 = cmp
    past_len = page_table.shape[1] * PAGE_SIZE
    past = cache[layer, page_table, :, :2].reshape(B, past_len, 2, G, D)
    cmp_rows = jnp.concatenate([past, rows[:, :, :2]], axis=1)
    k_c = _compress_jax(cmp_rows[:, :, 0], w1[0], b1[0], w2[0])
    v_c = _compress_jax(cmp_rows[:, :, 1], w1[1], b1[1], w2[1])
    nc = k_c.shape[1]
    qg = q.reshape(B, T, G, NSA_GROUP, D)
    s = jnp.einsum('btgrd,bngd->btgrn', qg, k_c).astype(f32) * (HEAD_DIM ** -0.5)
    end = jnp.arange(nc) * CMP_STRIDE + (CMP_BLOCK - 1)
    p = _masked_softmax(s, (end[None, :] <= pos[:, None])[None, :, None, None, :])
    o_cmp = jnp.einsum('btgrn,bngd->btgrd', p, v_c).reshape(B, T, H, D)
    imp = jnp.sum(p, axis=3)
    L = past_len + T
    n_blk = -(-L // SEL_BLOCK)
    ci = jnp.arange(nc)[:, None] * CMP_STRIDE
    sj = jnp.arange(n_blk)[None, :] * SEL_BLOCK
    overlap = ((ci < sj + SEL_BLOCK) & (ci + CMP_BLOCK > sj)).astype(f32)
    score = jnp.einsum('btgn,nj->btgj', imp, overlap)
    jt = (pos // SEL_BLOCK)[:, None]
    jj = jnp.arange(n_blk)[None, :]
    valid = jj <= jt
    forced = valid & ((jj == 0) | (jj == jt) | (jj == jt - 1))
    score = jnp.where(forced[None, :, None, :], jnp.inf, jnp.where(valid[None, :, None, :], score, -jnp.inf))
    _, idx = lax.top_k(score, min(N_SELECT, n_blk))
    kpos = idx[..., None] * SEL_BLOCK + jnp.arange(SEL_BLOCK)
    bi = jnp.arange(B)[:, None, None, None, None]
    gi = jnp.arange(G)[None, None, :, None, None]
    pp = jnp.minimum(kpos, past_len - 1)
    phys = page_table[bi, pp // PAGE_SIZE]
    off = pp % PAGE_SIZE
    k_past = cache[layer, phys, off, 2, gi]
    v_past = cache[layer, phys, off, 3, gi]
    nr = jnp.clip(kpos - past_len, 0, T - 1)
    is_new = (kpos >= past_len)[..., None]
    kg = jnp.where(is_new, rows[bi, nr, 2, gi], k_past)
    vg = jnp.where(is_new, rows[bi, nr, 3, gi], v_past)
    o_sel = _sel_attend_jax(q_rot, pos, kg, vg, kpos)
    wb = win_buf.shape[1]
    keys = jnp.concatenate([win_buf, win_rows], axis=1)
    kpos_w = past_len - wb + jnp.arange(wb + T)
    o_win = _window_attend_jax(q_rot, pos, keys[:, :, 0], keys[:, :, 1], kpos_w)
    return o_cmp, o_sel, o_win, keys[:, -wb:]


def _rope_tables(pos):
    half = HEAD_DIM // 2
    inv = ROPE_THETA ** (-jnp.arange(half, dtype=f32) / half)
    ang = pos.astype(f32)[:, None] * inv[None, :]
    cos, sin = jnp.cos(ang), jnp.sin(ang)
    return jnp.concatenate([cos, cos, cos, cos], axis=1), jnp.concatenate([-sin, sin, -sin, sin], axis=1)


def _prep_w_in(w):
    cols = [w[:, _OFF[n][0]:_OFF[n][1]] for n in _PAD_ORDER]
    used = sum(c.shape[1] for c in cols)
    cols.append(jnp.zeros((w.shape[0], PROJ_PAD - used), w.dtype))
    return jnp.concatenate(cols, axis=1).astype(bf16)


def _layer(x, pos, prm, l, nsa_fn, ret_state, gdn_state, conv_buf, tm):
    B, T, _ = x.shape
    M = B * T
    cosf, sinf = _rope_tables(pos)
    if tm > T:
        cosf, sinf = jnp.tile(cosf, (tm // T, 1)), jnp.tile(sinf, (tm // T, 1))
    x2d = x.reshape(M, D_MODEL)
    ret, q, qr, kv, win, ng, gq, gg, sm = _proj(x2d, prm['norm_pre'][l][None], prm['w_in_p'][l], cosf, sinf, tm)
    r3 = lambda a: a.reshape(B, T, a.shape[-1])
    ro, ret_new = _retention(r3(ret), ret_state, prm['ret_gn'][l][None], T)
    go, gdn_new = _gdn(r3(gq), r3(sm), r3(gg), conv_buf, prm['gdn_conv'][l], prm['gdn_a_log'][l],
                       prm['gdn_dt_bias'][l], prm['gdn_norm'][l][None], gdn_state, T)
    assert T >= CONV_WIDTH - 1
    conv_new = r3(gq)[:, T - (CONV_WIDTH - 1):]
    no, win_new = nsa_fn(r3(q), r3(qr), r3(kv), r3(win), r3(sm))
    y = _out(x2d, ro.reshape(M, -1), no.reshape(M, -1), ng, go.reshape(M, -1), prm['w_out_b'][l],
             prm['norm_post'][l][None], tm)
    rows = kv.reshape(B, T, 4, NSA_KV_HEADS, HEAD_DIM)
    return y.reshape(B, T, D_MODEL), (rows, win_new, ret_new, gdn_new, conv_new)


def kernel(x_prompt, x_sample, cache_nsa_kv, page_table, state_nsa_win, state_ret, state_gdn, state_gdn_conv,
           w_in, w_out, norm_pre, norm_post, ret_gn, gdn_norm, gdn_conv, gdn_a_log, gdn_dt_bias,
           cmp_w1, cmp_b1, cmp_w2):
    Bp, T, _ = x_prompt.shape
    Bs, Ts, _ = x_sample.shape
    depth = w_in.shape[0]
    past_len = page_table.shape[1] * PAGE_SIZE
    pos_p = jnp.arange(T, dtype=jnp.int32)
    pos_s = past_len + jnp.arange(Ts, dtype=jnp.int32)
    prm = {'w_in_p': [_prep_w_in(w_in[l]) for l in range(depth)],
           'w_out_b': [w_out[l].astype(bf16) for l in range(depth)],
           'norm_pre': norm_pre, 'norm_post': norm_post, 'ret_gn': ret_gn, 'gdn_norm': gdn_norm,
           'gdn_conv': gdn_conv, 'gdn_a_log': gdn_a_log, 'gdn_dt_bias': gdn_dt_bias}
    yp, ys = x_prompt, x_sample
    st_p, st_s = [], []
    for l in range(depth):
        wc, b1t, w2bd = _cmp_weights(cmp_w1[l], cmp_b1[l], cmp_w2[l])

        def nsa_prompt(q3, qr3, kv3, win3, sm3):
            kc = _compress_prompt(kv3, wc, b1t, w2bd)
            no = _nsa_prompt(q3, qr3, kc, kv3, win3, sm3)
            win_rows = win3.reshape(Bp, T, 2, NSA_KV_HEADS, HEAD_DIM)
            return no, win_rows[:, T - min(WINDOW, T):]

        def nsa_sample(q3, qr3, kv3, win3, sm3):
            heads = lambda a: a.reshape(Bs, Ts, -1, HEAD_DIM)
            rows = kv3.reshape(Bs, Ts, 4, NSA_KV_HEADS, HEAD_DIM)
            win_rows = win3.reshape(Bs, Ts, 2, NSA_KV_HEADS, HEAD_DIM)
            o_cmp, o_sel, o_win, win_new = _nsa_sample_jax(
                heads(q3), heads(qr3), rows, win_rows, pos_s, (cmp_w1[l], cmp_b1[l], cmp_w2[l]),
                cache_nsa_kv, l, page_table, state_nsa_win[l])
            gate = jax.nn.sigmoid(sm3[:, :, GATE_COL:GATE_COL + 3 * NSA_HEADS]).reshape(Bs, Ts, 3, NSA_HEADS, 1)
            no = gate[:, :, 0] * o_cmp + gate[:, :, 1] * o_sel + gate[:, :, 2] * o_win
            return no.reshape(Bs, Ts, NSA_WIDTH), win_new

        yp, sp = _layer(yp, pos_p, prm, l, nsa_prompt,
                        jnp.zeros((Bp, RET_HEADS, HEAD_DIM, HEAD_DIM), f32),
                        jnp.zeros((Bp, GDN_HEADS, HEAD_DIM, HEAD_DIM), f32),
                        jnp.zeros((Bp, CONV_WIDTH - 1, 3 * GDN_WIDTH), f32), 256)
        ys, ss = _layer(ys, pos_s, prm, l, nsa_sample, state_ret[l], state_gdn[l], state_gdn_conv[l], Bs * Ts)
        st_p.append(sp)
        st_s.append(ss)
    stk = lambda sts, i: jnp.stack([s[i] for s in sts])
    return (yp, ys, stk(st_p, 0), stk(st_s, 0), stk(st_p, 1), stk(st_s, 1), stk(st_p, 2), stk(st_s, 2),
            stk(st_p, 3), stk(st_s, 3), stk(st_p, 4), stk(st_s, 4))
```

```python
import functools
import math

import numpy as np
import jax
import jax.numpy as jnp
from jax import lax
from jax.experimental import pallas as pl
from jax.experimental.pallas import tpu as pltpu

f32 = jnp.float32
bf16 = jnp.bfloat16

D_MODEL = 1024
HEAD_DIM = 64
RET_WIDTH = 256
NSA_WIDTH = 512
GDN_WIDTH = 256
RET_HEADS = 4
NSA_HEADS = 8
NSA_KV_HEADS = 2
NSA_GROUP = 4
GDN_HEADS = 4
CMP_BLOCK = 32
CMP_STRIDE = 16
SEL_BLOCK = 64
N_SELECT = 16
WINDOW = 512
RET_CHUNK = 64
GDN_CHUNK = 64
GDN_CHUNKS_PER_STEP = 2
CONV_WIDTH = 4
PAGE_SIZE = 128
ROPE_THETA = 10000.0
EPS = 1e-6

LANES = 128
VMEM_LIMIT = 56 * 1024 * 1024

_SPLITS = (('ret', 4 * RET_WIDTH), ('nsa_q', NSA_WIDTH), ('nsa_kv', 6 * NSA_KV_HEADS * HEAD_DIM),
           ('nsa_gate', 3 * NSA_HEADS), ('nsa_g', NSA_WIDTH), ('gdn_qkv', 3 * GDN_WIDTH),
           ('gdn_ba', 2 * GDN_HEADS), ('gdn_g', GDN_WIDTH))
_OFF = {}
_o = 0
for _n, _w in _SPLITS:
    _OFF[_n] = (_o, _o + _w)
    _o += _w
PROJ_WIDTH = _o
_PAD_ORDER = ('ret', 'nsa_q', 'nsa_kv', 'nsa_g', 'gdn_qkv', 'gdn_g', 'nsa_gate', 'gdn_ba')
_POFF = {}
_o = 0
for _n in _PAD_ORDER:
    _w = _OFF[_n][1] - _OFF[_n][0]
    _POFF[_n] = _o
    _o += _w
SMALL_OFF = _POFF['nsa_gate']
PROJ_PAD = SMALL_OFF + LANES
GATE_COL = 0
BETA_COL = 3 * NSA_HEADS
A_COL = BETA_COL + GDN_HEADS

_NT = (((1,), (1,)), ((), ()))
_TN = (((0,), (0,)), ((), ()))
_HI = lax.Precision.HIGHEST


def _silu(x):
    return x * jax.nn.sigmoid(x)


def _cparams(sem):
    return pltpu.CompilerParams(dimension_semantics=sem, vmem_limit_bytes=VMEM_LIMIT)


def _proj_body(x_ref, g_ref, w_ref, cos_ref, sin_ref,
               ret_ref, q_ref, qr_ref, kv_ref, win_ref, ng_ref, gq_ref, gg_ref, sm_ref):
    x = x_ref[...]
    h = x * lax.rsqrt(jnp.mean(x * x, axis=-1, keepdims=True) + EPS) * g_ref[...]
    hb = h.astype(bf16)
    cos = cos_ref[...]
    sin = sin_ref[...]
    lane = lax.broadcasted_iota(jnp.int32, cos.shape, 1)
    low = (lane % HEAD_DIM) < HEAD_DIM // 2

    def mm(c0, width):
        return jnp.dot(hb, w_ref[:, c0:c0 + width], preferred_element_type=f32)

    def rope(v):
        sw = jnp.where(low, pltpu.roll(v, LANES - HEAD_DIM // 2, 1), pltpu.roll(v, HEAD_DIM // 2, 1))
        return v * cos + sw * sin

    r = mm(_POFF['ret'], 4 * RET_WIDTH)
    for j in range(2):
        ret_ref[:, j * LANES:(j + 1) * LANES] = rope(r[:, j * LANES:(j + 1) * LANES])
    for j in range(2, 4):
        ret_ref[:, j * LANES:(j + 1) * LANES] = rope(r[:, j * LANES:(j + 1) * LANES]) * (HEAD_DIM ** -0.5)
    ret_ref[:, 2 * RET_WIDTH:] = r[:, 2 * RET_WIDTH:]

    q = mm(_POFF['nsa_q'], NSA_WIDTH)
    q_ref[...] = q
    for j in range(NSA_WIDTH // LANES):
        qr_ref[:, j * LANES:(j + 1) * LANES] = rope(q[:, j * LANES:(j + 1) * LANES])

    kv = mm(_POFF['nsa_kv'], 6 * LANES)
    kv_ref[:, 0:2 * LANES] = kv[:, 0:2 * LANES]
    kv_ref[:, 2 * LANES:3 * LANES] = rope(kv[:, 2 * LANES:3 * LANES])
    kv_ref[:, 3 * LANES:4 * LANES] = kv[:, 3 * LANES:4 * LANES]
    win_ref[:, 0:LANES] = rope(kv[:, 4 * LANES:5 * LANES])
    win_ref[:, LANES:2 * LANES] = kv[:, 5 * LANES:6 * LANES]

    ng_ref[...] = mm(_POFF['nsa_g'], NSA_WIDTH)
    gq_ref[...] = mm(_POFF['gdn_qkv'], 3 * GDN_WIDTH)
    gg_ref[...] = mm(_POFF['gdn_g'], GDN_WIDTH)
    sm_ref[...] = mm(SMALL_OFF, LANES)


def _proj(x2d, gain, wpad, cosf, sinf, tm):
    M = x2d.shape[0]
    n_pos = cosf.shape[0] // tm
    widths = (4 * RET_WIDTH, NSA_WIDTH, NSA_WIDTH, 4 * LANES, 2 * LANES, NSA_WIDTH, 3 * GDN_WIDTH, GDN_WIDTH, LANES)
    row = lambda w: pl.BlockSpec((tm, w), lambda i: (i, 0))
    return pl.pallas_call(
        _proj_body,
        grid=(M // tm,),
        in_specs=[row(D_MODEL),
                  pl.BlockSpec((1, D_MODEL), lambda i: (0, 0)),
                  pl.BlockSpec((D_MODEL, PROJ_PAD), lambda i: (0, 0)),
                  pl.BlockSpec((tm, LANES), lambda i: (i % n_pos, 0)),
                  pl.BlockSpec((tm, LANES), lambda i: (i % n_pos, 0))],
        out_specs=[row(w) for w in widths],
        out_shape=[jax.ShapeDtypeStruct((M, w), f32) for w in widths],
        compiler_params=_cparams(("parallel",)),
        name="proj",
    )(x2d, gain, wpad, cosf, sinf)


def _ret_body(ret_ref, dm_ref, qd_ref, kd_ref, gn_ref, s0_ref, o_ref, s_ref, *, c, nchunk, sdec):
    @pl.when(pl.program_id(1) == 0)
    def _():
        s_ref[...] = s0_ref[...]

    for ci in range(nchunk):
        blk = ret_ref[0, ci * c:(ci + 1) * c, :]
        outs = []
        for h in range(RET_HEADS):
            sl = lambda base: blk[:, base + h * HEAD_DIM: base + (h + 1) * HEAD_DIM]
            q, k, v, g = sl(0), sl(RET_WIDTH), sl(2 * RET_WIDTH), sl(3 * RET_WIDTH)
            S = s_ref[0, h]
            qb, vb = q.astype(bf16), v.astype(bf16)
            att = lax.dot_general(qb, k.astype(bf16), _NT, preferred_element_type=f32) * dm_ref[h]
            o = (jnp.dot(att.astype(bf16), vb, preferred_element_type=f32)
                 + jnp.dot(qb, S.astype(bf16), preferred_element_type=f32) * qd_ref[:, h:h + 1])
            kd = (k * kd_ref[:, h:h + 1]).astype(bf16)
            s_ref[0, h] = S * sdec[h] + lax.dot_general(kd, vb, _TN, preferred_element_type=f32)
            mu = jnp.mean(o, axis=-1, keepdims=True)
            d = o - mu
            var = jnp.mean(d * d, axis=-1, keepdims=True)
            y = d * lax.rsqrt(var + EPS) * gn_ref[:, h * HEAD_DIM:(h + 1) * HEAD_DIM] * _silu(g)
            outs.append(y)
        o_ref[0, ci * c:(ci + 1) * c, :] = jnp.concatenate(outs, axis=1)


def _retention(ret3, s0, gn, T):
    B = ret3.shape[0]
    c = math.gcd(T, RET_CHUNK)
    nchunk = min(T // c, 4)
    tb = c * nchunk
    hh = jnp.arange(RET_HEADS, dtype=f32)
    lg = jnp.log1p(-jnp.exp2(-5.0 - hh))
    ii = jnp.arange(c, dtype=f32)
    rel = ii[:, None] - ii[None, :]
    dm = jnp.exp(jnp.where(rel[None] >= 0, rel[None] * lg[:, None, None], -jnp.inf))
    qd = jnp.exp((ii + 1.0)[:, None] * lg[None, :])
    kd = jnp.exp((c - 1.0 - ii)[:, None] * lg[None, :])
    sdec = tuple(float((1.0 - 2.0 ** (-5.0 - h)) ** c) for h in range(RET_HEADS))
    return pl.pallas_call(
        functools.partial(_ret_body, c=c, nchunk=nchunk, sdec=sdec),
        grid=(B, T // tb),
        in_specs=[pl.BlockSpec((1, tb, 4 * RET_WIDTH), lambda b, j: (b, j, 0)),
                  pl.BlockSpec((RET_HEADS, c, c), lambda b, j: (0, 0, 0)),
                  pl.BlockSpec((c, RET_HEADS), lambda b, j: (0, 0)),
                  pl.BlockSpec((c, RET_HEADS), lambda b, j: (0, 0)),
                  pl.BlockSpec((1, RET_WIDTH), lambda b, j: (0, 0)),
                  pl.BlockSpec((1, RET_HEADS, HEAD_DIM, HEAD_DIM), lambda b, j: (b, 0, 0, 0))],
        out_specs=[pl.BlockSpec((1, tb, RET_WIDTH), lambda b, j: (b, j, 0)),
                   pl.BlockSpec((1, RET_HEADS, HEAD_DIM, HEAD_DIM), lambda b, j: (b, 0, 0, 0))],
        out_shape=[jax.ShapeDtypeStruct((B, T, RET_WIDTH), f32),
                   jax.ShapeDtypeStruct((B, RET_HEADS, HEAD_DIM, HEAD_DIM), f32)],
        compiler_params=_cparams(("parallel", "arbitrary")),
        name="retention",
    )(ret3, dm, qd, kd, gn, s0)


def _softplus(x):
    return jnp.maximum(x, 0.0) + jnp.log1p(jnp.exp(-jnp.abs(x)))


def _unit_lower_inverse(n_list, c):
    rr = lax.broadcasted_iota(jnp.int32, (c, c), 0)
    cc = lax.broadcasted_iota(jnp.int32, (c, c), 1)
    eye = (rr == cc).astype(f32)
    pws = [-n for n in n_list]
    ress = [eye + pw for pw in pws]
    span = 2
    while span < c:
        pws = [jnp.dot(pw, pw, preferred_element_type=f32, precision=_HI) for pw in pws]
        ress = [res + jnp.dot(res, pw, preferred_element_type=f32, precision=_HI) for res, pw in zip(ress, pws)]
        span *= 2
    return ress


def _gdn_body(x_ref, sm_ref, gg_ref, cb_ref, cw_ref, al_ref, dtb_ref, gnorm_ref, s0_ref,
              o_ref, s_ref, xc_ref, *, c, nchunk):
    tb = c * nchunk

    @pl.when(pl.program_id(1) == 0)
    def _():
        s_ref[...] = s0_ref[...]
        xc_ref[5:8, :] = cb_ref[0]

    xc_ref[8:8 + tb, :] = x_ref[0]
    y = cw_ref[0:1, :] * xc_ref[pl.ds(5, tb), :]
    for j in range(1, CONV_WIDTH):
        y = y + cw_ref[j:j + 1, :] * xc_ref[pl.ds(5 + j, tb), :]
    tail = xc_ref[pl.ds(5 + tb, 3), :]
    xc_ref[5:8, :] = tail
    y = _silu(y)

    sm = sm_ref[0]
    beta_all = jax.nn.sigmoid(sm)
    g_all = -jnp.exp(al_ref[...]) * _softplus(sm + dtb_ref[...])
    rr = lax.broadcasted_iota(jnp.int32, (c, c), 0)
    cc = lax.broadcasted_iota(jnp.int32, (c, c), 1)
    tri = rr >= cc
    strict = rr > cc
    trif = tri.astype(f32)
    triu = (rr <= cc).astype(f32)
    gg = gg_ref[0]
    H = range(GDN_HEADS)
    P = [(ci, h) for ci in range(nchunk) for h in H]

    qs, ks, vs, betas, Gcs, Ls, kbs, kbfs, n_list = {}, {}, {}, {}, {}, {}, {}, {}, []
    for ci in range(nchunk):
        rows = slice(ci * c, (ci + 1) * c)
        gch = g_all[rows]
        G_cols = jnp.dot(trif, gch, preferred_element_type=f32, precision=_HI)
        G_rows = jnp.dot(gch.T, triu, preferred_element_type=f32, precision=_HI)
        for h in H:
            p = (ci, h)
            hs = lambda base: y[rows, base + h * HEAD_DIM: base + (h + 1) * HEAD_DIM]
            q, k = hs(0), hs(GDN_WIDTH)
            qs[p] = q * lax.rsqrt(jnp.sum(q * q, axis=-1, keepdims=True) + EPS) * (HEAD_DIM ** -0.5)
            k = k * lax.rsqrt(jnp.sum(k * k, axis=-1, keepdims=True) + EPS)
            ks[p] = k
            vs[p] = hs(2 * GDN_WIDTH)
            betas[p] = beta_all[rows, BETA_COL + h: BETA_COL + h + 1]
            Gcs[p] = G_cols[:, A_COL + h: A_COL + h + 1]
            Gr = G_rows[A_COL + h: A_COL + h + 1, :]
            Ls[p] = jnp.exp(jnp.where(tri, Gcs[p] - Gr, -jnp.inf))
            kbs[p] = k * betas[p]
            kbfs[p] = k.astype(bf16)
            n_list.append(jnp.where(strict, lax.dot_general(kbs[p].astype(bf16), kbfs[p], _NT,
                                                            preferred_element_type=f32) * Ls[p], 0.0))
    tinvs = dict(zip(P, [t.astype(bf16) for t in _unit_lower_inverse(n_list, c)]))
    egs = {p: jnp.exp(Gcs[p]) for p in P}
    us = {p: jnp.dot(tinvs[p], (vs[p] * betas[p]).astype(bf16), preferred_element_type=f32) for p in P}
    ws = {p: jnp.dot(tinvs[p], (kbs[p] * egs[p]).astype(bf16), preferred_element_type=f32).astype(bf16) for p in P}
    qks = {p: (lax.dot_general(qs[p].astype(bf16), kbfs[p], _NT, preferred_element_type=f32) * Ls[p]).astype(bf16) for p in P}
    qes = {p: (qs[p] * egs[p]).astype(bf16) for p in P}

    Ss = [s_ref[0, h] for h in H]
    for ci in range(nchunk):
        Sbs = [S.astype(bf16) for S in Ss]
        qss = [jnp.dot(qes[ci, h], Sbs[h], preferred_element_type=f32) for h in H]
        vnbs = [(us[ci, h] - jnp.dot(ws[ci, h], Sbs[h], preferred_element_type=f32)).astype(bf16) for h in H]
        os_ = [qss[h] + jnp.dot(qks[ci, h], vnbs[h], preferred_element_type=f32) for h in H]
        new_s = []
        for h in H:
            g_last = Gcs[ci, h][c - 1:c, :]
            kdec = (ks[ci, h] * jnp.exp(g_last - Gcs[ci, h])).astype(bf16)
            new_s.append(Ss[h] * jnp.exp(g_last) + lax.dot_general(kdec, vnbs[h], _TN, preferred_element_type=f32))
        Ss = new_s
        outs = []
        for h in H:
            o = os_[h]
            yo = o * lax.rsqrt(jnp.mean(o * o, axis=-1, keepdims=True) + EPS) * gnorm_ref[...]
            outs.append(yo * _silu(gg[ci * c:(ci + 1) * c, h * HEAD_DIM:(h + 1) * HEAD_DIM]))
        o_ref[0, ci * c:(ci + 1) * c, :] = jnp.concatenate(outs, axis=1)
    for h in H:
        s_ref[0, h] = Ss[h]


def _gdn(gq3, sm3, gg3, conv_buf, conv_w, a_log, dt_bias, gnorm, s0, T):
    B = gq3.shape[0]
    c = math.gcd(T, GDN_CHUNK)
    nchunk = min(T // c, GDN_CHUNKS_PER_STEP)
    tb = c * nchunk
    C = 3 * GDN_WIDTH
    pad = lambda v: jnp.zeros((1, LANES), f32).at[0, A_COL:A_COL + GDN_HEADS].set(v.astype(f32))
    return pl.pallas_call(
        functools.partial(_gdn_body, c=c, nchunk=nchunk),
        grid=(B, T // tb),
        in_specs=[pl.BlockSpec((1, tb, C), lambda b, j: (b, j, 0)),
                  pl.BlockSpec((1, tb, LANES), lambda b, j: (b, j, 0)),
                  pl.BlockSpec((1, tb, GDN_WIDTH), lambda b, j: (b, j, 0)),
                  pl.BlockSpec((1, CONV_WIDTH - 1, C), lambda b, j: (b, 0, 0)),
                  pl.BlockSpec((CONV_WIDTH, C), lambda b, j: (0, 0)),
                  pl.BlockSpec((1, LANES), lambda b, j: (0, 0)),
                  pl.BlockSpec((1, LANES), lambda b, j: (0, 0)),
                  pl.BlockSpec((1, HEAD_DIM), lambda b, j: (0, 0)),
                  pl.BlockSpec((1, GDN_HEADS, HEAD_DIM, HEAD_DIM), lambda b, j: (b, 0, 0, 0))],
        out_specs=[pl.BlockSpec((1, tb, GDN_WIDTH), lambda b, j: (b, j, 0)),
                   pl.BlockSpec((1, GDN_HEADS, HEAD_DIM, HEAD_DIM), lambda b, j: (b, 0, 0, 0))],
        out_shape=[jax.ShapeDtypeStruct((B, T, GDN_WIDTH), f32),
                   jax.ShapeDtypeStruct((B, GDN_HEADS, HEAD_DIM, HEAD_DIM), f32)],
        scratch_shapes=[pltpu.VMEM((8 + tb + 8, C), f32)],
        compiler_params=_cparams(("parallel", "arbitrary")),
        name="gdn",
    )(gq3, sm3, gg3, conv_buf, conv_w, pad(a_log), pad(dt_bias), gnorm, s0)


def _cmp_weights(w1, b1, w2):
    z = jnp.zeros((CMP_STRIDE, HEAD_DIM, HEAD_DIM), f32)
    nq = 2 * NSA_KV_HEADS
    def half(i):
        rows = []
        for qi in range(nq):
            blk = w1[qi // NSA_KV_HEADS, i * CMP_STRIDE:(i + 1) * CMP_STRIDE]
            rows.append(jnp.concatenate([blk if qj == qi else z for qj in range(nq)], axis=2))
        return jnp.concatenate(rows, axis=1)
    wc = jnp.concatenate([half(0), half(1)], axis=2).astype(bf16)
    b1t = jnp.concatenate([b1[qi // NSA_KV_HEADS] for qi in range(nq)])[None, :]
    z2 = jnp.zeros((HEAD_DIM, HEAD_DIM), f32)
    w2bd = jnp.concatenate([jnp.concatenate([w2[qi // NSA_KV_HEADS] if qj == qi else z2 for qj in range(nq)], axis=1)
                            for qi in range(nq)], axis=0).astype(bf16)
    return wc, b1t, w2bd


def _cmp_prompt_body(x_ref, wc_ref, b1_ref, w2_ref, o_ref, *, n_seg):
    acc = jnp.zeros((n_seg, 8 * HEAD_DIM), f32)
    for s in range(CMP_STRIDE):
        xs = x_ref[0, :, s * 8 * HEAD_DIM: s * 8 * HEAD_DIM + 4 * HEAD_DIM].astype(bf16)
        acc = acc + jnp.dot(xs, wc_ref[s], preferred_element_type=f32)
    first = acc[:, :4 * HEAD_DIM]
    second = pltpu.roll(acc[:, 4 * HEAD_DIM:], n_seg - 1, 0)
    hid = _silu(b1_ref[...] + first + second)
    out = jnp.dot(hid.astype(bf16), w2_ref[...], preferred_element_type=f32)
    for qi in range(2 * NSA_KV_HEADS):
        o_ref[0, qi] = out[:, qi * HEAD_DIM:(qi + 1) * HEAD_DIM]


def _compress_prompt(kv3, wc, b1t, w2bd):
    B, T, _ = kv3.shape
    n_seg = T // CMP_STRIDE
    xseg = kv3.reshape(B, n_seg, CMP_STRIDE * 8 * HEAD_DIM)
    return pl.pallas_call(
        functools.partial(_cmp_prompt_body, n_seg=n_seg),
        grid=(B,),
        in_specs=[pl.BlockSpec((1, n_seg, CMP_STRIDE * 8 * HEAD_DIM), lambda b: (b, 0, 0)),
                  pl.BlockSpec((CMP_STRIDE, 4 * HEAD_DIM, 8 * HEAD_DIM), lambda b: (0, 0, 0)),
                  pl.BlockSpec((1, 4 * HEAD_DIM), lambda b: (0, 0)),
                  pl.BlockSpec((4 * HEAD_DIM, 4 * HEAD_DIM), lambda b: (0, 0))],
        out_specs=pl.BlockSpec((1, 4, n_seg, HEAD_DIM), lambda b: (b, 0, 0, 0)),
        out_shape=jax.ShapeDtypeStruct((B, 4, n_seg, HEAD_DIM), f32),
        compiler_params=_cparams(("parallel",)),
        name="compress_prompt",
    )(xseg, wc, b1t, w2bd)


def _topk_rows(score, k):
    n = score.shape[0]
    idx = lax.broadcasted_iota(jnp.int32, score.shape, 0)
    taken = jnp.zeros(score.shape, jnp.bool_)
    for _ in range(k):
        work = jnp.where(taken, -jnp.inf, score)
        m = jnp.max(work, axis=0, keepdims=True)
        cand = jnp.where((work == m) & jnp.logical_not(taken), idx, n)
        first = jnp.min(cand, axis=0, keepdims=True)
        taken = taken | (idx == first)
    return jnp.where(taken, 1.0, 0.0)


def _softmax_rows(s, mask):
    s = jnp.where(mask, s, -jnp.inf)
    m = jnp.max(s, axis=-1, keepdims=True)
    m = jnp.where(m == -jnp.inf, 0.0, m)
    p = jnp.exp(s - m)
    return p / jnp.maximum(jnp.sum(p, axis=-1, keepdims=True), 1e-30)


def _attend_tile(qs, kbs, vbs, biases, carries, tq):
    n = range(len(qs))
    R = qs[0].shape[0]
    tk = kbs[0].shape[0]
    ss = [lax.dot_general(qs[i], kbs[i], _NT, preferred_element_type=f32) for i in n]
    ss = [(ss[i].reshape(NSA_GROUP, tq, tk) + biases[i][None]).reshape(R, tk) for i in n]
    m_news = [jnp.maximum(carries[i][0], jnp.max(ss[i], axis=-1, keepdims=True)) for i in n]
    m_safes = [jnp.where(m == -jnp.inf, 0.0, m) for m in m_news]
    alphas = [jnp.exp(carries[i][0] - m_safes[i]) for i in n]
    ps = [jnp.exp(ss[i] - m_safes[i]) for i in n]
    ls = [alphas[i] * carries[i][1] + jnp.sum(ps[i], axis=-1, keepdims=True) for i in n]
    accs = [alphas[i] * carries[i][2] + jnp.dot(ps[i].astype(bf16), vbs[i], preferred_element_type=f32) for i in n]
    return [(m_news[i], ls[i], accs[i]) for i in n]


def _flash_init(R):
    return (jnp.full((R, 1), -jnp.inf, f32), jnp.zeros((R, 1), f32), jnp.zeros((R, HEAD_DIM), f32))


def _nsa_prompt_body(q_ref, qr_ref, kc_ref, ks_ref, vs_ref, kw_ref, vw_ref, sm_ref, ov_ref, ex_ref,
                     o_ref, sbias_ref, wbias_ref, *, tq, tk, tkw, n_wt, T, n_cmp, n_blk):
    i = pl.program_id(1)
    t0 = i * tq
    scale = HEAD_DIM ** -0.5
    G = range(NSA_KV_HEADS)
    R = NSA_GROUP * tq
    gates = jax.nn.sigmoid(sm_ref[0])
    n_pad = kc_ref.shape[2]
    tpos_r = t0 + lax.broadcasted_iota(jnp.int32, (tq, 1), 0)
    tpos_c = t0 + lax.broadcasted_iota(jnp.int32, (1, tq), 1)
    eye = (lax.broadcasted_iota(jnp.int32, (tq, tq), 0) == lax.broadcasted_iota(jnp.int32, (tq, tq), 1)).astype(bf16)
    heads = lambda x, g: [(x[:, (g * NSA_GROUP + r) * HEAD_DIM:(g * NSA_GROUP + r + 1) * HEAD_DIM] * scale).astype(bf16)
                          for r in range(NSA_GROUP)]

    o_cmps, imps = [], []
    nn = lax.broadcasted_iota(jnp.int32, (1, n_pad), 1)
    cmask = (nn * CMP_STRIDE + (CMP_BLOCK - 1) <= tpos_r) & (nn < n_cmp)
    for g in G:
        qh = heads(q_ref[0], g)
        kc = kc_ref[0, g].astype(bf16)
        vc = kc_ref[0, NSA_KV_HEADS + g].astype(bf16)
        ps = [_softmax_rows(lax.dot_general(qh[r], kc, _NT, preferred_element_type=f32), cmask) for r in range(NSA_GROUP)]
        o_cmps.append([jnp.dot(p.astype(bf16), vc, preferred_element_type=f32) for p in ps])
        imps.append(ps[0] + ps[1] + ps[2] + ps[3])

    score = jnp.concatenate([lax.dot_general(ov_ref[...], imps[g], _NT, preferred_element_type=f32, precision=_HI)[:n_blk]
                             for g in G], axis=1)
    jj = lax.broadcasted_iota(jnp.int32, score.shape, 0)
    jt = jnp.concatenate([tpos_c] * NSA_KV_HEADS, axis=1) // SEL_BLOCK
    valid = jj <= jt
    forced = valid & ((jj == 0) | (jj == jt) | (jj == jt - 1))
    score = jnp.where(forced, jnp.inf, jnp.where(valid, score, -jnp.inf))
    sel_t = _topk_rows(score, min(N_SELECT, n_blk)).astype(bf16)
    if n_blk < LANES:
        sel_t = jnp.concatenate([sel_t, jnp.zeros((LANES - n_blk, NSA_KV_HEADS * tq), bf16)], axis=0)
    for g in G:
        sel = lax.dot_general(eye, sel_t[:, g * tq:(g + 1) * tq], _NT, preferred_element_type=f32)
        full = jnp.dot(sel.astype(bf16), ex_ref[...], preferred_element_type=f32)
        for kt in range(T // tk):
            kpos = kt * tk + lax.broadcasted_iota(jnp.int32, (1, tk), 1)
            ok = (full[:, kt * tk:(kt + 1) * tk] > 0.5) & (kpos <= tpos_r)
            sbias_ref[g, kt] = jnp.where(ok, 0.0, -jnp.inf)

    w0 = pl.multiple_of(jnp.clip(t0 - WINDOW, 0, T - n_wt * tkw), tq)
    for j in range(n_wt):
        dist = tpos_r - (w0 + j * tkw + lax.broadcasted_iota(jnp.int32, (1, tkw), 1))
        wbias_ref[j] = jnp.where((dist >= 0) & (dist < WINDOW), 0.0, -jnp.inf)

    stack = lambda x, g: (jnp.concatenate([x[:, (g * NSA_GROUP + r) * HEAD_DIM:(g * NSA_GROUP + r + 1) * HEAD_DIM]
                                           for r in range(NSA_GROUP)], axis=0) * scale).astype(bf16)
    qrs = [stack(qr_ref[0], g) for g in G]
    kv_tile = lambda ref, k0, n, g: ref[0, pl.ds(k0, n), g * HEAD_DIM:(g + 1) * HEAD_DIM].astype(bf16)

    def sel_step(kt, carries):
        k0 = pl.multiple_of(kt * tk, tk)
        return tuple(_attend_tile(qrs, [kv_tile(ks_ref, k0, tk, g) for g in G], [kv_tile(vs_ref, k0, tk, g) for g in G],
                                  [sbias_ref[g, kt] for g in G], carries, tq))

    sel = lax.fori_loop(0, (t0 + tq + tk - 1) // tk, sel_step, tuple(_flash_init(R) for _ in G))

    win = [_flash_init(R) for _ in G]
    for j in range(n_wt):
        k0 = pl.multiple_of(w0 + j * tkw, tq)
        win = _attend_tile(qrs, [kv_tile(kw_ref, k0, tkw, g) for g in G], [kv_tile(vw_ref, k0, tkw, g) for g in G],
                           [wbias_ref[j]] * NSA_KV_HEADS, win, tq)

    for g in G:
        o_sel = sel[g][2] / jnp.maximum(sel[g][1], 1e-30)
        o_win = win[g][2] / jnp.maximum(win[g][1], 1e-30)
        outs = []
        for r in range(NSA_GROUP):
            hcol = GATE_COL + g * NSA_GROUP + r
            rows = slice(r * tq, (r + 1) * tq)
            outs.append(gates[:, hcol:hcol + 1] * o_cmps[g][r]
                        + gates[:, NSA_HEADS + hcol:NSA_HEADS + hcol + 1] * o_sel[rows]
                        + gates[:, 2 * NSA_HEADS + hcol:2 * NSA_HEADS + hcol + 1] * o_win[rows])
        o_ref[0, :, g * NSA_GROUP * HEAD_DIM:(g + 1) * NSA_GROUP * HEAD_DIM] = jnp.concatenate(outs, axis=1)


def _overlap_t(n_blk_pad, n_cmp, n_pad):
    ci = np.arange(n_pad)[None, :] * CMP_STRIDE
    sj = np.arange(n_blk_pad)[:, None] * SEL_BLOCK
    ov = (ci < sj + SEL_BLOCK) & (ci + CMP_BLOCK > sj) & (np.arange(n_pad)[None, :] < n_cmp)
    return jnp.asarray(ov.astype(np.float32))


def _nsa_prompt(q3, qr3, kc, kv3, win3, sm3):
    B, T, _ = q3.shape
    tq = 128
    tk = 256
    tkw = 256 if T >= 1024 else 128
    n_wt = -(-min(T, WINDOW + tq) // tkw)
    assert T % tk == 0 and n_wt * tkw <= T
    n_seg = T // CMP_STRIDE
    n_cmp = n_seg - CMP_BLOCK // CMP_STRIDE + 1
    n_blk = -(-T // SEL_BLOCK)
    ov = _overlap_t(LANES, n_cmp, n_seg)
    ex = jnp.asarray((np.arange(LANES)[:, None] == (np.arange(T)[None, :] // SEL_BLOCK)).astype(np.float32)).astype(bf16)
    qspec = pl.BlockSpec((1, tq, NSA_WIDTH), lambda b, i: (b, i, 0))
    col = lambda j: pl.BlockSpec((1, T, LANES), lambda b, i: (b, 0, j))
    return pl.pallas_call(
        functools.partial(_nsa_prompt_body, tq=tq, tk=tk, tkw=tkw, n_wt=n_wt, T=T, n_cmp=n_cmp, n_blk=n_blk),
        grid=(B, T // tq),
        in_specs=[qspec, qspec,
                  pl.BlockSpec((1, 4, n_seg, HEAD_DIM), lambda b, i: (b, 0, 0, 0)),
                  col(2), col(3), col(0), col(1),
                  pl.BlockSpec((1, tq, LANES), lambda b, i: (b, i, 0)),
                  pl.BlockSpec((LANES, n_seg), lambda b, i: (0, 0)),
                  pl.BlockSpec((LANES, T), lambda b, i: (0, 0))],
        out_specs=qspec,
        out_shape=jax.ShapeDtypeStruct((B, T, NSA_WIDTH), f32),
        scratch_shapes=[pltpu.VMEM((NSA_KV_HEADS, T // tk, tq, tk), f32), pltpu.VMEM((n_wt, tq, tkw), f32)],
        compiler_params=_cparams(("parallel", "arbitrary")),
        name="nsa_prompt",
    )(q3, qr3, kc, kv3, kv3, win3, win3, sm3, ov, ex)


def _out_body(x_ref, ro_ref, no_ref, ng_ref, go_ref, w_ref, g_ref, y_ref):
    no = no_ref[...] * _silu(ng_ref[...])
    z = (jnp.dot(ro_ref[...].astype(bf16), w_ref[0:RET_WIDTH, :], preferred_element_type=f32)
         + jnp.dot(no.astype(bf16), w_ref[RET_WIDTH:RET_WIDTH + NSA_WIDTH, :], preferred_element_type=f32)
         + jnp.dot(go_ref[...].astype(bf16), w_ref[RET_WIDTH + NSA_WIDTH:, :], preferred_element_type=f32))
    y_ref[...] = x_ref[...] + z * lax.rsqrt(jnp.mean(z * z, axis=-1, keepdims=True) + EPS) * g_ref[...]


def _out(x2d, ro, no, ng, go, w_out_b, gain, tm):
    M = x2d.shape[0]
    row = lambda w: pl.BlockSpec((tm, w), lambda i: (i, 0))
    return pl.pallas_call(
        _out_body,
        grid=(M // tm,),
        in_specs=[row(D_MODEL), row(RET_WIDTH), row(NSA_WIDTH), row(NSA_WIDTH), row(GDN_WIDTH),
                  pl.BlockSpec((D_MODEL, D_MODEL), lambda i: (0, 0)),
                  pl.BlockSpec((1, D_MODEL), lambda i: (0, 0))],
        out_specs=row(D_MODEL),
        out_shape=jax.ShapeDtypeStruct((M, D_MODEL), f32),
        compiler_params=_cparams(("parallel",)),
        name="out",
    )(x2d, ro, no, ng, go, w_out_b, gain)


PAGES_PER_STEP = 16


def _page_specs(l, pg, half):
    return [pl.BlockSpec((None, None, 4 * HEAD_DIM, PAGE_SIZE),
                         lambda b, j, pt, i=i: (l, pt[b, j * pg + i], half, 0)) for i in range(pg)]


def _cmp_sample_body(pt_ref, *refs, pg):
    page_refs, wc_ref, o_ref, xt_ref = refs[:pg], refs[pg], refs[pg + 1], refs[pg + 2]
    for i in range(pg):
        zt = page_refs[i][...].T
        xt_ref[0, i * PAGE_SIZE:(i + 1) * PAGE_SIZE, :] = zt[:, 0:LANES]
        xt_ref[1, i * PAGE_SIZE:(i + 1) * PAGE_SIZE, :] = zt[:, LANES:2 * LANES]
    n_row = pg * PAGE_SIZE // CMP_STRIDE
    acc = jnp.zeros((n_row, 8 * HEAD_DIM), f32)
    for s in range(CMP_STRIDE):
        xs = jnp.concatenate([xt_ref[0, pl.ds(s, n_row, stride=CMP_STRIDE), :],
                              xt_ref[1, pl.ds(s, n_row, stride=CMP_STRIDE), :]], axis=1).astype(bf16)
        acc = acc + jnp.dot(xs, wc_ref[s], preferred_element_type=f32)
    o_ref[0] = acc


def _cmp_sample(cache_t, l, page_table, wc):
    B, n_pages = page_table.shape
    pg = PAGES_PER_STEP
    n_row = pg * PAGE_SIZE // CMP_STRIDE
    return pl.pallas_call(
        functools.partial(_cmp_sample_body, pg=pg),
        grid_spec=pltpu.PrefetchScalarGridSpec(
            num_scalar_prefetch=1,
            grid=(B, n_pages // pg),
            in_specs=_page_specs(l, pg, 0) + [pl.BlockSpec((CMP_STRIDE, 4 * HEAD_DIM, 8 * HEAD_DIM), lambda b, j, pt: (0, 0, 0))],
            out_specs=pl.BlockSpec((1, n_row, 8 * HEAD_DIM), lambda b, j, pt: (b, j, 0)),
            scratch_shapes=[pltpu.VMEM((2, pg * PAGE_SIZE, LANES), f32)]),
        out_shape=jax.ShapeDtypeStruct((B, n_pages * PAGE_SIZE // CMP_STRIDE, 8 * HEAD_DIM), f32),
        compiler_params=_cparams(("parallel", "arbitrary")),
        name="compress_sample",
    )(page_table, *([cache_t] * pg), wc)


def _stack_heads(x, g):
    return jnp.concatenate([x[:, (g * NSA_GROUP + r) * HEAD_DIM:(g * NSA_GROUP + r + 1) * HEAD_DIM]
                            for r in range(NSA_GROUP)], axis=0)


def _nsa_sample_mid_body(p_ref, b1_ref, w2_ref, q_ref, qr_ref, win_ref, st_ref, ov_ref,
                         ocmp_ref, owin_ref, selt_ref, *, Ts, n_cmp, past_len, wb):
    P = p_ref[0]
    n_seg = P.shape[0]
    hid = _silu(b1_ref[...] + P[:, :4 * HEAD_DIM] + pltpu.roll(P[:, 4 * HEAD_DIM:], n_seg - 1, 0))
    kcv = jnp.dot(hid.astype(bf16), w2_ref[...], preferred_element_type=f32)
    scale = HEAD_DIM ** -0.5
    R = NSA_GROUP * Ts
    trow = lax.broadcasted_iota(jnp.int32, (R, 1), 0) % Ts
    imps = []
    for g in range(NSA_KV_HEADS):
        qs = (_stack_heads(q_ref[0], g) * scale).astype(bf16)
        kc = kcv[:, g * HEAD_DIM:(g + 1) * HEAD_DIM].astype(bf16)
        vc = kcv[:, (NSA_KV_HEADS + g) * HEAD_DIM:(NSA_KV_HEADS + g + 1) * HEAD_DIM].astype(bf16)
        s = lax.dot_general(qs, kc, _NT, preferred_element_type=f32)
        nn = lax.broadcasted_iota(jnp.int32, (1, n_seg), 1)
        cmask = (nn * CMP_STRIDE + (CMP_BLOCK - 1) <= past_len + trow) & (nn < n_cmp)
        p = _softmax_rows(s, cmask)
        ocmp_ref[0, g] = jnp.dot(p.astype(bf16), vc, preferred_element_type=f32)
        imps.append(jnp.sum(p.reshape(NSA_GROUP, Ts, n_seg), axis=0))
        qrs = (_stack_heads(qr_ref[0], g) * scale).astype(bf16)
        kwt = st_ref[g * HEAD_DIM:(g + 1) * HEAD_DIM, :].astype(bf16)
        vwt = st_ref[(NSA_KV_HEADS + g) * HEAD_DIM:(NSA_KV_HEADS + g + 1) * HEAD_DIM, :].astype(bf16)
        wn = win_ref[0]
        knew = wn[:, g * HEAD_DIM:(g + 1) * HEAD_DIM].astype(bf16)
        vnew = wn[:, (NSA_KV_HEADS + g) * HEAD_DIM:(NSA_KV_HEADS + g + 1) * HEAD_DIM].astype(bf16)
        s_b = jnp.dot(qrs, kwt, preferred_element_type=f32)
        s_n = lax.dot_general(qrs, knew, _NT, preferred_element_type=f32)
        jb = lax.broadcasted_iota(jnp.int32, (1, wb), 1)
        dist_b = trow + wb - jb
        s_b = jnp.where((dist_b >= 0) & (dist_b < WINDOW) & (past_len - wb + jb >= 0), s_b, -jnp.inf)
        dist_n = trow - lax.broadcasted_iota(jnp.int32, (1, Ts), 1)
        s_n = jnp.where((dist_n >= 0) & (dist_n < WINDOW), s_n, -jnp.inf)
        m = jnp.maximum(jnp.max(s_b, axis=-1, keepdims=True), jnp.max(s_n, axis=-1, keepdims=True))
        m = jnp.where(m == -jnp.inf, 0.0, m)
        pb = jnp.exp(s_b - m)
        pn = jnp.exp(s_n - m)
        den = jnp.sum(pb, axis=-1, keepdims=True) + jnp.sum(pn, axis=-1, keepdims=True)
        o = (lax.dot_general(pb.astype(bf16), vwt, _NT, preferred_element_type=f32)
             + jnp.dot(pn.astype(bf16), vnew, preferred_element_type=f32))
        owin_ref[0, g] = o / jnp.maximum(den, 1e-30)

    imp2 = jnp.concatenate(imps, axis=0)
    score = lax.dot_general(ov_ref[...], imp2, _NT, preferred_element_type=f32, precision=_HI)
    jj = lax.broadcasted_iota(jnp.int32, score.shape, 0)
    jt = (past_len + lax.broadcasted_iota(jnp.int32, score.shape, 1) % Ts) // SEL_BLOCK
    valid = jj <= jt
    forced = valid & ((jj == 0) | (jj == jt) | (jj == jt - 1))
    score = jnp.where(forced, jnp.inf, jnp.where(valid, score, -jnp.inf))
    selt_ref[0] = _topk_rows(score, N_SELECT)


def _nsa_sample_mid(P, b1t, w2bd, q3, qr3, win3, st_t, l, past_len):
    B, Ts, _ = q3.shape
    n_seg = P.shape[1]
    n_cmp = n_seg - CMP_BLOCK // CMP_STRIDE + 1
    n_blk = -(-(past_len + Ts) // SEL_BLOCK)
    nb_pad = -(-n_blk // 8) * 8
    wb = st_t.shape[-1]
    ov = _overlap_t(nb_pad, n_cmp, n_seg)
    R = NSA_GROUP * Ts
    tok = lambda w: pl.BlockSpec((1, Ts, w), lambda b: (b, 0, 0))
    return pl.pallas_call(
        functools.partial(_nsa_sample_mid_body, Ts=Ts, n_cmp=n_cmp, past_len=past_len, wb=wb),
        grid=(B,),
        in_specs=[pl.BlockSpec((1, n_seg, 8 * HEAD_DIM), lambda b: (b, 0, 0)),
                  pl.BlockSpec((1, 4 * HEAD_DIM), lambda b: (0, 0)),
                  pl.BlockSpec((4 * HEAD_DIM, 4 * HEAD_DIM), lambda b: (0, 0)),
                  tok(NSA_WIDTH), tok(NSA_WIDTH), tok(4 * HEAD_DIM),
                  pl.BlockSpec((None, None, 4 * HEAD_DIM, wb), lambda b: (l, b, 0, 0)),
                  pl.BlockSpec((nb_pad, n_seg), lambda b: (0, 0))],
        out_specs=[pl.BlockSpec((1, NSA_KV_HEADS, R, HEAD_DIM), lambda b: (b, 0, 0, 0)),
                   pl.BlockSpec((1, NSA_KV_HEADS, R, HEAD_DIM), lambda b: (b, 0, 0, 0)),
                   pl.BlockSpec((1, nb_pad, NSA_KV_HEADS * Ts), lambda b: (b, 0, 0))],
        out_shape=[jax.ShapeDtypeStruct((B, NSA_KV_HEADS, R, HEAD_DIM), f32),
                   jax.ShapeDtypeStruct((B, NSA_KV_HEADS, R, HEAD_DIM), f32),
                   jax.ShapeDtypeStruct((B, nb_pad, NSA_KV_HEADS * Ts), f32)],
        compiler_params=_cparams(("parallel",)),
        name="nsa_sample_mid",
    )(P, b1t, w2bd, q3, qr3, win3, st_t, ov)


def _nsa_sample_sel_body(pt_ref, *refs, pg, Ts, n_steps):
    page_refs = refs[:pg]
    qr_ref, kvn_ref, selt_ref, ex_ref, ocmp_ref, owin_ref, sm_ref, o_ref, m_ref, l_ref, acc_ref = refs[pg:]
    j = pl.program_id(1)
    scale = HEAD_DIM ** -0.5
    R = NSA_GROUP * Ts
    GT = NSA_KV_HEADS * Ts
    nb = 2 * pg
    trow = lax.broadcasted_iota(jnp.int32, (R, 1), 0) % Ts
    eye = (lax.broadcasted_iota(jnp.int32, (GT, GT), 0) == lax.broadcasted_iota(jnp.int32, (GT, GT), 1)).astype(bf16)
    sel_j = selt_ref[0, pl.ds(pl.multiple_of(j * nb, nb), nb), :].astype(bf16)
    sel_j = jnp.concatenate([sel_j, jnp.zeros((LANES - nb, GT), bf16)], axis=0)
    sel_rows = lax.dot_general(eye, sel_j, _NT, preferred_element_type=f32)
    full = jnp.dot(sel_rows.astype(bf16), ex_ref[...], preferred_element_type=f32)

    for g in range(NSA_KV_HEADS):
        qrs = (_stack_heads(qr_ref[0], g) * scale).astype(bf16)

        @pl.when(j == 0)
        def _():
            kvn = kvn_ref[0]
            knew = kvn[:, (2 * NSA_KV_HEADS + g) * HEAD_DIM:(2 * NSA_KV_HEADS + g + 1) * HEAD_DIM].astype(bf16)
            vnew = kvn[:, (3 * NSA_KV_HEADS + g) * HEAD_DIM:(3 * NSA_KV_HEADS + g + 1) * HEAD_DIM].astype(bf16)
            s_n = lax.dot_general(qrs, knew, _NT, preferred_element_type=f32)
            s_n = jnp.where(lax.broadcasted_iota(jnp.int32, (1, Ts), 1) <= trow, s_n, -jnp.inf)
            m0 = jnp.max(s_n, axis=-1, keepdims=True)
            p0 = jnp.exp(s_n - m0)
            m_ref[g] = m0
            l_ref[g] = jnp.sum(p0, axis=-1, keepdims=True)
            acc_ref[g] = jnp.dot(p0.astype(bf16), vnew, preferred_element_type=f32)

        kt = jnp.concatenate([page_refs[i][g * HEAD_DIM:(g + 1) * HEAD_DIM, :] for i in range(pg)], axis=1).astype(bf16)
        vt = jnp.concatenate([page_refs[i][(NSA_KV_HEADS + g) * HEAD_DIM:(NSA_KV_HEADS + g + 1) * HEAD_DIM, :]
                              for i in range(pg)], axis=1).astype(bf16)
        s = jnp.dot(qrs, kt, preferred_element_type=f32)
        mg = full[g * Ts:(g + 1) * Ts]
        s = jnp.where(jnp.concatenate([mg] * NSA_GROUP, axis=0) > 0.5, s, -jnp.inf)
        m_old = m_ref[g]
        m_new = jnp.maximum(m_old, jnp.max(s, axis=-1, keepdims=True))
        alpha = jnp.exp(m_old - m_new)
        p = jnp.exp(s - m_new)
        l_ref[g] = alpha * l_ref[g] + jnp.sum(p, axis=-1, keepdims=True)
        acc_ref[g] = alpha * acc_ref[g] + lax.dot_general(p.astype(bf16), vt, _NT, preferred_element_type=f32)
        m_ref[g] = m_new

    @pl.when(j == n_steps - 1)
    def _():
        gates = jax.nn.sigmoid(sm_ref[0])
        outs = []
        for g in range(NSA_KV_HEADS):
            o_sel = acc_ref[g] / jnp.maximum(l_ref[g], 1e-30)
            o_cmp = ocmp_ref[0, g]
            o_win = owin_ref[0, g]
            for r in range(NSA_GROUP):
                hcol = GATE_COL + g * NSA_GROUP + r
                rows = slice(r * Ts, (r + 1) * Ts)
                outs.append(gates[:, hcol:hcol + 1] * o_cmp[rows]
                            + gates[:, NSA_HEADS + hcol:NSA_HEADS + hcol + 1] * o_sel[rows]
                            + gates[:, 2 * NSA_HEADS + hcol:2 * NSA_HEADS + hcol + 1] * o_win[rows])
        o_ref[0] = jnp.concatenate(outs, axis=1)


def _nsa_sample_sel(cache_t, l, page_table, qr3, kv3, selt, ocmp, owin, sm3):
    B, n_pages = page_table.shape
    Ts = qr3.shape[1]
    pg = PAGES_PER_STEP
    n_steps = n_pages // pg
    R = NSA_GROUP * Ts
    ex = jnp.asarray((np.arange(LANES)[:, None] == (np.arange(pg * PAGE_SIZE)[None, :] // SEL_BLOCK)).astype(np.float32)).astype(bf16)
    tok = lambda w: pl.BlockSpec((1, Ts, w), lambda b, j, pt: (b, 0, 0))
    stk = pl.BlockSpec((1, NSA_KV_HEADS, R, HEAD_DIM), lambda b, j, pt: (b, 0, 0, 0))
    return pl.pallas_call(
        functools.partial(_nsa_sample_sel_body, pg=pg, Ts=Ts, n_steps=n_steps),
        grid_spec=pltpu.PrefetchScalarGridSpec(
            num_scalar_prefetch=1,
            grid=(B, n_steps),
            in_specs=_page_specs(l, pg, 1) + [
                tok(NSA_WIDTH), tok(8 * HEAD_DIM),
                pl.BlockSpec((1, selt.shape[1], selt.shape[2]), lambda b, j, pt: (b, 0, 0)),
                pl.BlockSpec((LANES, pg * PAGE_SIZE), lambda b, j, pt: (0, 0)),
                stk, stk, tok(LANES)],
            out_specs=tok(NSA_WIDTH),
            scratch_shapes=[pltpu.VMEM((NSA_KV_HEADS, R, 1), f32), pltpu.VMEM((NSA_KV_HEADS, R, 1), f32),
                            pltpu.VMEM((NSA_KV_HEADS, R, HEAD_DIM), f32)]),
        out_shape=jax.ShapeDtypeStruct((B, Ts, NSA_WIDTH), f32),
        compiler_params=_cparams(("parallel", "arbitrary")),
        name="nsa_sample_sel",
    )(page_table, *([cache_t] * pg), qr3, kv3, selt, ex, ocmp, owin, sm3)


def _nsa_sample(q3, qr3, kv3, win3, sm3, cache_t, st_t, l, page_table, wc, b1t, w2bd):
    Ts = q3.shape[1]
    n_pages = page_table.shape[1]
    past_len = n_pages * PAGE_SIZE
    assert (past_len + Ts) // CMP_STRIDE * CMP_STRIDE <= past_len
    assert past_len % SEL_BLOCK == 0 and Ts <= SEL_BLOCK and n_pages % PAGES_PER_STEP == 0
    P = _cmp_sample(cache_t, l, page_table, wc)
    ocmp, owin, selt = _nsa_sample_mid(P, b1t, w2bd, q3, qr3, win3, st_t, l, past_len)
    return _nsa_sample_sel(cache_t, l, page_table, qr3, kv3, selt, ocmp, owin, sm3)


def _rope_tables(pos):
    half = HEAD_DIM // 2
    inv = ROPE_THETA ** (-jnp.arange(half, dtype=f32) / half)
    ang = pos.astype(f32)[:, None] * inv[None, :]
    cos, sin = jnp.cos(ang), jnp.sin(ang)
    return jnp.concatenate([cos, cos, cos, cos], axis=1), jnp.concatenate([-sin, sin, -sin, sin], axis=1)


def _prep_w_in(w):
    cols = [w[:, _OFF[n][0]:_OFF[n][1]] for n in _PAD_ORDER]
    used = sum(c.shape[1] for c in cols)
    cols.append(jnp.zeros((w.shape[0], PROJ_PAD - used), w.dtype))
    return jnp.concatenate(cols, axis=1).astype(bf16)


def _layer(x, pos, prm, l, nsa_fn, ret_state, gdn_state, conv_buf, tm):
    B, T, _ = x.shape
    M = B * T
    cosf, sinf = _rope_tables(pos)
    if tm > T:
        cosf, sinf = jnp.tile(cosf, (tm // T, 1)), jnp.tile(sinf, (tm // T, 1))
    x2d = x.reshape(M, D_MODEL)
    ret, q, qr, kv, win, ng, gq, gg, sm = _proj(x2d, prm['norm_pre'][l][None], prm['w_in_p'][l], cosf, sinf, tm)
    r3 = lambda a: a.reshape(B, T, a.shape[-1])
    ro, ret_new = _retention(r3(ret), ret_state, prm['ret_gn'][l][None], T)
    go, gdn_new = _gdn(r3(gq), r3(sm), r3(gg), conv_buf, prm['gdn_conv'][l], prm['gdn_a_log'][l],
                       prm['gdn_dt_bias'][l], prm['gdn_norm'][l][None], gdn_state, T)
    assert T >= CONV_WIDTH - 1
    conv_new = r3(gq)[:, T - (CONV_WIDTH - 1):]
    no, win_new = nsa_fn(r3(q), r3(qr), r3(kv), r3(win), r3(sm))
    y = _out(x2d, ro.reshape(M, -1), no.reshape(M, -1), ng, go.reshape(M, -1), prm['w_out_b'][l],
             prm['norm_post'][l][None], tm)
    rows = kv.reshape(B, T, 4, NSA_KV_HEADS, HEAD_DIM)
    return y.reshape(B, T, D_MODEL), (rows, win_new, ret_new, gdn_new, conv_new)


def kernel(x_prompt, x_sample, cache_nsa_kv, page_table, state_nsa_win, state_ret, state_gdn, state_gdn_conv,
           w_in, w_out, norm_pre, norm_post, ret_gn, gdn_norm, gdn_conv, gdn_a_log, gdn_dt_bias,
           cmp_w1, cmp_b1, cmp_w2):
    Bp, T, _ = x_prompt.shape
    Bs, Ts, _ = x_sample.shape
    depth = w_in.shape[0]
    past_len = page_table.shape[1] * PAGE_SIZE
    pos_p = jnp.arange(T, dtype=jnp.int32)
    pos_s = past_len + jnp.arange(Ts, dtype=jnp.int32)
    prm = {'w_in_p': [_prep_w_in(w_in[l]) for l in range(depth)],
           'w_out_b': [w_out[l].astype(bf16) for l in range(depth)],
           'norm_pre': norm_pre, 'norm_post': norm_post, 'ret_gn': ret_gn, 'gdn_norm': gdn_norm,
           'gdn_conv': gdn_conv, 'gdn_a_log': gdn_a_log, 'gdn_dt_bias': gdn_dt_bias}
    yp, ys = x_prompt, x_sample
    cache_t = jnp.transpose(cache_nsa_kv, (0, 1, 3, 4, 5, 2)).reshape(
        depth, cache_nsa_kv.shape[1], 8 * HEAD_DIM, PAGE_SIZE)
    st_t = jnp.transpose(state_nsa_win, (0, 1, 3, 4, 5, 2)).reshape(
        depth, Bs, 4 * HEAD_DIM, state_nsa_win.shape[2])
    st_p, st_s = [], []
    for l in range(depth):
        wc, b1t, w2bd = _cmp_weights(cmp_w1[l], cmp_b1[l], cmp_w2[l])

        def nsa_prompt(q3, qr3, kv3, win3, sm3):
            kc = _compress_prompt(kv3, wc, b1t, w2bd)
            no = _nsa_prompt(q3, qr3, kc, kv3, win3, sm3)
            win_rows = win3.reshape(Bp, T, 2, NSA_KV_HEADS, HEAD_DIM)
            return no, win_rows[:, T - min(WINDOW, T):]

        def nsa_sample(q3, qr3, kv3, win3, sm3):
            no = _nsa_sample(q3, qr3, kv3, win3, sm3, cache_t, st_t, l, page_table, wc, b1t, w2bd)
            win_rows = win3.reshape(Bs, Ts, 2, NSA_KV_HEADS, HEAD_DIM)
            keys = jnp.concatenate([state_nsa_win[l], win_rows], axis=1)
            return no, keys[:, -state_nsa_win.shape[2]:]

        yp, sp = _layer(yp, pos_p, prm, l, nsa_prompt,
                        jnp.zeros((Bp, RET_HEADS, HEAD_DIM, HEAD_DIM), f32),
                        jnp.zeros((Bp, GDN_HEADS, HEAD_DIM, HEAD_DIM), f32),
                        jnp.zeros((Bp, CONV_WIDTH - 1, 3 * GDN_WIDTH), f32), 256)
        ys, ss = _layer(ys, pos_s, prm, l, nsa_sample, state_ret[l], state_gdn[l], state_gdn_conv[l], Bs * Ts)
        st_p.append(sp)
        st_s.append(ss)
    stk = lambda sts, i: jnp.stack([s[i] for s in sts])
    return (yp, ys, stk(st_p, 0), stk(st_s, 0), stk(st_p, 1), stk(st_s, 1), stk(st_p, 2), stk(st_s, 2),
            stk(st_p, 3), stk(st_s, 3), stk(st_p, 4), stk(st_s, 4))
```

```python
import functools
import math

import numpy as np
import jax
import jax.numpy as jnp
from jax import lax
from jax.experimental import pallas as pl
from jax.experimental.pallas import tpu as pltpu

f32 = jnp.float32
bf16 = jnp.bfloat16

D_MODEL = 1024
HEAD_DIM = 64
RET_WIDTH = 256
NSA_WIDTH = 512
GDN_WIDTH = 256
RET_HEADS = 4
NSA_HEADS = 8
NSA_KV_HEADS = 2
NSA_GROUP = 4
GDN_HEADS = 4
CMP_BLOCK = 32
CMP_STRIDE = 16
SEL_BLOCK = 64
N_SELECT = 16
WINDOW = 512
RET_CHUNK = 64
GDN_CHUNK = 64
GDN_CHUNKS_PER_STEP = 4
CONV_WIDTH = 4
PAGE_SIZE = 128
ROPE_THETA = 10000.0
EPS = 1e-6

LANES = 128
VMEM_LIMIT = 56 * 1024 * 1024

_SPLITS = (('ret', 4 * RET_WIDTH), ('nsa_q', NSA_WIDTH), ('nsa_kv', 6 * NSA_KV_HEADS * HEAD_DIM),
           ('nsa_gate', 3 * NSA_HEADS), ('nsa_g', NSA_WIDTH), ('gdn_qkv', 3 * GDN_WIDTH),
           ('gdn_ba', 2 * GDN_HEADS), ('gdn_g', GDN_WIDTH))
_OFF = {}
_o = 0
for _n, _w in _SPLITS:
    _OFF[_n] = (_o, _o + _w)
    _o += _w
PROJ_WIDTH = _o
_PAD_ORDER = ('ret', 'nsa_q', 'nsa_kv', 'nsa_g', 'gdn_qkv', 'gdn_g', 'nsa_gate', 'gdn_ba')
_POFF = {}
_o = 0
for _n in _PAD_ORDER:
    _w = _OFF[_n][1] - _OFF[_n][0]
    _POFF[_n] = _o
    _o += _w
SMALL_OFF = _POFF['nsa_gate']
PROJ_PAD = SMALL_OFF + LANES
GATE_COL = 0
BETA_COL = 3 * NSA_HEADS
A_COL = BETA_COL + GDN_HEADS

_NT = (((1,), (1,)), ((), ()))
_TN = (((0,), (0,)), ((), ()))
_HI = lax.Precision.HIGHEST


def _silu(x):
    return x * jax.nn.sigmoid(x)


def _cparams(sem):
    return pltpu.CompilerParams(dimension_semantics=sem, vmem_limit_bytes=VMEM_LIMIT)


def _proj_body(x_ref, g_ref, w_ref, cos_ref, sin_ref,
               ret_ref, q_ref, qr_ref, kv_ref, win_ref, ng_ref, gq_ref, gg_ref, sm_ref):
    x = x_ref[...]
    h = x * lax.rsqrt(jnp.mean(x * x, axis=-1, keepdims=True) + EPS) * g_ref[...]
    hb = h.astype(bf16)
    cos = cos_ref[...]
    sin = sin_ref[...]
    lane = lax.broadcasted_iota(jnp.int32, cos.shape, 1)
    low = (lane % HEAD_DIM) < HEAD_DIM // 2

    def mm(c0, width):
        return jnp.dot(hb, w_ref[:, c0:c0 + width], preferred_element_type=f32)

    def rope(v):
        sw = jnp.where(low, pltpu.roll(v, LANES - HEAD_DIM // 2, 1), pltpu.roll(v, HEAD_DIM // 2, 1))
        return v * cos + sw * sin

    r = mm(_POFF['ret'], 4 * RET_WIDTH)
    for j in range(2):
        ret_ref[:, j * LANES:(j + 1) * LANES] = rope(r[:, j * LANES:(j + 1) * LANES])
    for j in range(2, 4):
        ret_ref[:, j * LANES:(j + 1) * LANES] = rope(r[:, j * LANES:(j + 1) * LANES]) * (HEAD_DIM ** -0.5)
    ret_ref[:, 2 * RET_WIDTH:] = r[:, 2 * RET_WIDTH:]

    q = mm(_POFF['nsa_q'], NSA_WIDTH)
    q_ref[...] = q
    for j in range(NSA_WIDTH // LANES):
        qr_ref[:, j * LANES:(j + 1) * LANES] = rope(q[:, j * LANES:(j + 1) * LANES])

    kv = mm(_POFF['nsa_kv'], 6 * LANES)
    kv_ref[:, 0:2 * LANES] = kv[:, 0:2 * LANES]
    kv_ref[:, 2 * LANES:3 * LANES] = rope(kv[:, 2 * LANES:3 * LANES])
    kv_ref[:, 3 * LANES:4 * LANES] = kv[:, 3 * LANES:4 * LANES]
    win_ref[:, 0:LANES] = rope(kv[:, 4 * LANES:5 * LANES])
    win_ref[:, LANES:2 * LANES] = kv[:, 5 * LANES:6 * LANES]

    ng_ref[...] = mm(_POFF['nsa_g'], NSA_WIDTH)
    gq_ref[...] = mm(_POFF['gdn_qkv'], 3 * GDN_WIDTH)
    gg_ref[...] = mm(_POFF['gdn_g'], GDN_WIDTH)
    sm_ref[...] = mm(SMALL_OFF, LANES)


def _proj(x2d, gain, wpad, cosf, sinf, tm):
    M = x2d.shape[0]
    n_pos = cosf.shape[0] // tm
    widths = (4 * RET_WIDTH, NSA_WIDTH, NSA_WIDTH, 4 * LANES, 2 * LANES, NSA_WIDTH, 3 * GDN_WIDTH, GDN_WIDTH, LANES)
    row = lambda w: pl.BlockSpec((tm, w), lambda i: (i, 0))
    return pl.pallas_call(
        _proj_body,
        grid=(M // tm,),
        in_specs=[row(D_MODEL),
                  pl.BlockSpec((1, D_MODEL), lambda i: (0, 0)),
                  pl.BlockSpec((D_MODEL, PROJ_PAD), lambda i: (0, 0)),
                  pl.BlockSpec((tm, LANES), lambda i: (i % n_pos, 0)),
                  pl.BlockSpec((tm, LANES), lambda i: (i % n_pos, 0))],
        out_specs=[row(w) for w in widths],
        out_shape=[jax.ShapeDtypeStruct((M, w), f32) for w in widths],
        compiler_params=_cparams(("parallel",)),
        name="proj",
    )(x2d, gain, wpad, cosf, sinf)


def _ret_body(ret_ref, dm_ref, qd_ref, kd_ref, gn_ref, s0_ref, o_ref, s_ref, *, c, nchunk, sdec):
    @pl.when(pl.program_id(1) == 0)
    def _():
        s_ref[...] = s0_ref[...]

    for ci in range(nchunk):
        blk = ret_ref[0, ci * c:(ci + 1) * c, :]
        outs = []
        for h in range(RET_HEADS):
            sl = lambda base: blk[:, base + h * HEAD_DIM: base + (h + 1) * HEAD_DIM]
            q, k, v, g = sl(0), sl(RET_WIDTH), sl(2 * RET_WIDTH), sl(3 * RET_WIDTH)
            S = s_ref[0, h]
            qb, vb = q.astype(bf16), v.astype(bf16)
            att = lax.dot_general(qb, k.astype(bf16), _NT, preferred_element_type=f32) * dm_ref[h]
            o = (jnp.dot(att.astype(bf16), vb, preferred_element_type=f32)
                 + jnp.dot(qb, S.astype(bf16), preferred_element_type=f32) * qd_ref[:, h:h + 1])
            kd = (k * kd_ref[:, h:h + 1]).astype(bf16)
            s_ref[0, h] = S * sdec[h] + lax.dot_general(kd, vb, _TN, preferred_element_type=f32)
            mu = jnp.mean(o, axis=-1, keepdims=True)
            d = o - mu
            var = jnp.mean(d * d, axis=-1, keepdims=True)
            y = d * lax.rsqrt(var + EPS) * gn_ref[:, h * HEAD_DIM:(h + 1) * HEAD_DIM] * _silu(g)
            outs.append(y)
        o_ref[0, ci * c:(ci + 1) * c, :] = jnp.concatenate(outs, axis=1)


def _retention(ret3, s0, gn, T):
    B = ret3.shape[0]
    c = math.gcd(T, RET_CHUNK)
    nchunk = min(T // c, 4)
    tb = c * nchunk
    hh = jnp.arange(RET_HEADS, dtype=f32)
    lg = jnp.log1p(-jnp.exp2(-5.0 - hh))
    ii = jnp.arange(c, dtype=f32)
    rel = ii[:, None] - ii[None, :]
    dm = jnp.exp(jnp.where(rel[None] >= 0, rel[None] * lg[:, None, None], -jnp.inf))
    qd = jnp.exp((ii + 1.0)[:, None] * lg[None, :])
    kd = jnp.exp((c - 1.0 - ii)[:, None] * lg[None, :])
    sdec = tuple(float((1.0 - 2.0 ** (-5.0 - h)) ** c) for h in range(RET_HEADS))
    return pl.pallas_call(
        functools.partial(_ret_body, c=c, nchunk=nchunk, sdec=sdec),
        grid=(B, T // tb),
        in_specs=[pl.BlockSpec((1, tb, 4 * RET_WIDTH), lambda b, j: (b, j, 0)),
                  pl.BlockSpec((RET_HEADS, c, c), lambda b, j: (0, 0, 0)),
                  pl.BlockSpec((c, RET_HEADS), lambda b, j: (0, 0)),
                  pl.BlockSpec((c, RET_HEADS), lambda b, j: (0, 0)),
                  pl.BlockSpec((1, RET_WIDTH), lambda b, j: (0, 0)),
                  pl.BlockSpec((1, RET_HEADS, HEAD_DIM, HEAD_DIM), lambda b, j: (b, 0, 0, 0))],
        out_specs=[pl.BlockSpec((1, tb, RET_WIDTH), lambda b, j: (b, j, 0)),
                   pl.BlockSpec((1, RET_HEADS, HEAD_DIM, HEAD_DIM), lambda b, j: (b, 0, 0, 0))],
        out_shape=[jax.ShapeDtypeStruct((B, T, RET_WIDTH), f32),
                   jax.ShapeDtypeStruct((B, RET_HEADS, HEAD_DIM, HEAD_DIM), f32)],
        compiler_params=_cparams(("parallel", "arbitrary")),
        name="retention",
    )(ret3, dm, qd, kd, gn, s0)


def _softplus(x):
    return jnp.maximum(x, 0.0) + jnp.log1p(jnp.exp(-jnp.abs(x)))


def _split_bf16(x):
    hi = x.astype(bf16)
    return hi, (x - hi.astype(f32)).astype(bf16)


def _dot_split(a, b):
    d = lambda x, y: jnp.dot(x, y, preferred_element_type=f32)
    return d(a[0], b[0]) + d(a[0], b[1]) + d(a[1], b[0])


def _unit_lower_inverse(n_list, c):
    rr = lax.broadcasted_iota(jnp.int32, (c, c), 0)
    cc = lax.broadcasted_iota(jnp.int32, (c, c), 1)
    eye = (rr == cc).astype(f32)
    ress = [eye - n for n in n_list]
    pws = [_split_bf16(-n) for n in n_list]
    span = 2
    while span < c:
        pws = [_split_bf16(_dot_split(pw, pw)) for pw in pws]
        ress = [res + _dot_split(_split_bf16(res), pw) for res, pw in zip(ress, pws)]
        span *= 2
    return ress


def _gdn_body(x_ref, sm_ref, gg_ref, cb_ref, cw_ref, al_ref, dtb_ref, gnorm_ref, s0_ref,
              o_ref, s_ref, xc_ref, *, c, nchunk):
    tb = c * nchunk

    @pl.when(pl.program_id(1) == 0)
    def _():
        s_ref[...] = s0_ref[...]
        xc_ref[5:8, :] = cb_ref[0]

    xc_ref[8:8 + tb, :] = x_ref[0]
    y = cw_ref[0:1, :] * xc_ref[pl.ds(5, tb), :]
    for j in range(1, CONV_WIDTH):
        y = y + cw_ref[j:j + 1, :] * xc_ref[pl.ds(5 + j, tb), :]
    tail = xc_ref[pl.ds(5 + tb, 3), :]
    xc_ref[5:8, :] = tail
    y = _silu(y)

    sm = sm_ref[0]
    beta_all = jax.nn.sigmoid(sm)
    g_all = -jnp.exp(al_ref[...]) * _softplus(sm + dtb_ref[...])
    rr = lax.broadcasted_iota(jnp.int32, (c, c), 0)
    cc = lax.broadcasted_iota(jnp.int32, (c, c), 1)
    tri = rr >= cc
    strict = rr > cc
    trif = tri.astype(f32)
    triu = (rr <= cc).astype(f32)
    gg = gg_ref[0]
    H = range(GDN_HEADS)
    P = [(ci, h) for ci in range(nchunk) for h in H]

    qs, ks, vs, betas, Gcs, Ls, kbs, kbfs, n_list = {}, {}, {}, {}, {}, {}, {}, {}, []
    for ci in range(nchunk):
        rows = slice(ci * c, (ci + 1) * c)
        gch = g_all[rows]
        G_cols = jnp.dot(trif, gch, preferred_element_type=f32, precision=_HI)
        G_rows = jnp.dot(gch.T, triu, preferred_element_type=f32, precision=_HI)
        for h in H:
            p = (ci, h)
            hs = lambda base: y[rows, base + h * HEAD_DIM: base + (h + 1) * HEAD_DIM]
            q, k = hs(0), hs(GDN_WIDTH)
            qs[p] = q * lax.rsqrt(jnp.sum(q * q, axis=-1, keepdims=True) + EPS) * (HEAD_DIM ** -0.5)
            k = k * lax.rsqrt(jnp.sum(k * k, axis=-1, keepdims=True) + EPS)
            ks[p] = k
            vs[p] = hs(2 * GDN_WIDTH)
            betas[p] = beta_all[rows, BETA_COL + h: BETA_COL + h + 1]
            Gcs[p] = G_cols[:, A_COL + h: A_COL + h + 1]
            Gr = G_rows[A_COL + h: A_COL + h + 1, :]
            Ls[p] = jnp.exp(jnp.where(tri, Gcs[p] - Gr, -jnp.inf))
            kbs[p] = k * betas[p]
            kbfs[p] = k.astype(bf16)
            n_list.append(jnp.where(strict, lax.dot_general(kbs[p].astype(bf16), kbfs[p], _NT,
                                                            preferred_element_type=f32) * Ls[p], 0.0))
    tinvs = dict(zip(P, [t.astype(bf16) for t in _unit_lower_inverse(n_list, c)]))
    egs = {p: jnp.exp(Gcs[p]) for p in P}
    us = {p: jnp.dot(tinvs[p], (vs[p] * betas[p]).astype(bf16), preferred_element_type=f32) for p in P}
    ws = {p: jnp.dot(tinvs[p], (kbs[p] * egs[p]).astype(bf16), preferred_element_type=f32).astype(bf16) for p in P}
    qks = {p: (lax.dot_general(qs[p].astype(bf16), kbfs[p], _NT, preferred_element_type=f32) * Ls[p]).astype(bf16) for p in P}
    qes = {p: (qs[p] * egs[p]).astype(bf16) for p in P}

    Ss = [s_ref[0, h] for h in H]
    for ci in range(nchunk):
        Sbs = [S.astype(bf16) for S in Ss]
        qss = [jnp.dot(qes[ci, h], Sbs[h], preferred_element_type=f32) for h in H]
        vnbs = [(us[ci, h] - jnp.dot(ws[ci, h], Sbs[h], preferred_element_type=f32)).astype(bf16) for h in H]
        os_ = [qss[h] + jnp.dot(qks[ci, h], vnbs[h], preferred_element_type=f32) for h in H]
        new_s = []
        for h in H:
            g_last = Gcs[ci, h][c - 1:c, :]
            kdec = (ks[ci, h] * jnp.exp(g_last - Gcs[ci, h])).astype(bf16)
            new_s.append(Ss[h] * jnp.exp(g_last) + lax.dot_general(kdec, vnbs[h], _TN, preferred_element_type=f32))
        Ss = new_s
        outs = []
        for h in H:
            o = os_[h]
            yo = o * lax.rsqrt(jnp.mean(o * o, axis=-1, keepdims=True) + EPS) * gnorm_ref[...]
            outs.append(yo * _silu(gg[ci * c:(ci + 1) * c, h * HEAD_DIM:(h + 1) * HEAD_DIM]))
        o_ref[0, ci * c:(ci + 1) * c, :] = jnp.concatenate(outs, axis=1)
    for h in H:
        s_ref[0, h] = Ss[h]


def _gdn(gq3, sm3, gg3, conv_buf, conv_w, a_log, dt_bias, gnorm, s0, T):
    B = gq3.shape[0]
    c = math.gcd(T, GDN_CHUNK)
    nchunk = min(T // c, GDN_CHUNKS_PER_STEP)
    tb = c * nchunk
    C = 3 * GDN_WIDTH
    pad = lambda v: jnp.zeros((1, LANES), f32).at[0, A_COL:A_COL + GDN_HEADS].set(v.astype(f32))
    return pl.pallas_call(
        functools.partial(_gdn_body, c=c, nchunk=nchunk),
        grid=(B, T // tb),
        in_specs=[pl.BlockSpec((1, tb, C), lambda b, j: (b, j, 0)),
                  pl.BlockSpec((1, tb, LANES), lambda b, j: (b, j, 0)),
                  pl.BlockSpec((1, tb, GDN_WIDTH), lambda b, j: (b, j, 0)),
                  pl.BlockSpec((1, CONV_WIDTH - 1, C), lambda b, j: (b, 0, 0)),
                  pl.BlockSpec((CONV_WIDTH, C), lambda b, j: (0, 0)),
                  pl.BlockSpec((1, LANES), lambda b, j: (0, 0)),
                  pl.BlockSpec((1, LANES), lambda b, j: (0, 0)),
                  pl.BlockSpec((1, HEAD_DIM), lambda b, j: (0, 0)),
                  pl.BlockSpec((1, GDN_HEADS, HEAD_DIM, HEAD_DIM), lambda b, j: (b, 0, 0, 0))],
        out_specs=[pl.BlockSpec((1, tb, GDN_WIDTH), lambda b, j: (b, j, 0)),
                   pl.BlockSpec((1, GDN_HEADS, HEAD_DIM, HEAD_DIM), lambda b, j: (b, 0, 0, 0))],
        out_shape=[jax.ShapeDtypeStruct((B, T, GDN_WIDTH), f32),
                   jax.ShapeDtypeStruct((B, GDN_HEADS, HEAD_DIM, HEAD_DIM), f32)],
        scratch_shapes=[pltpu.VMEM((8 + tb + 8, C), f32)],
        compiler_params=_cparams(("parallel", "arbitrary")),
        name="gdn",
    )(gq3, sm3, gg3, conv_buf, conv_w, pad(a_log), pad(dt_bias), gnorm, s0)


def _cmp_weights(w1, b1, w2):
    z = jnp.zeros((CMP_STRIDE, HEAD_DIM, HEAD_DIM), f32)
    nq = 2 * NSA_KV_HEADS
    def half(i):
        rows = []
        for qi in range(nq):
            blk = w1[qi // NSA_KV_HEADS, i * CMP_STRIDE:(i + 1) * CMP_STRIDE]
            rows.append(jnp.concatenate([blk if qj == qi else z for qj in range(nq)], axis=2))
        return jnp.concatenate(rows, axis=1)
    wc = jnp.concatenate([half(0), half(1)], axis=2).astype(bf16)
    b1t = jnp.concatenate([b1[qi // NSA_KV_HEADS] for qi in range(nq)])[None, :]
    z2 = jnp.zeros((HEAD_DIM, HEAD_DIM), f32)
    w2bd = jnp.concatenate([jnp.concatenate([w2[qi // NSA_KV_HEADS] if qj == qi else z2 for qj in range(nq)], axis=1)
                            for qi in range(nq)], axis=0).astype(bf16)
    pairs = []
    for pr in range(2):
        per_s = []
        for s in range(CMP_STRIDE):
            blk = wc[s, pr * LANES:(pr + 1) * LANES, :].reshape(LANES, 2, 2, LANES)[:, :, pr]
            per_s.append(blk.reshape(LANES, 2 * LANES))
        pairs.append(jnp.stack([jnp.concatenate([per_s[2 * s2], per_s[2 * s2 + 1]], axis=0)
                                for s2 in range(CMP_STRIDE // 2)]))
    wp = jnp.stack(pairs)
    return wc, wp, b1t, w2bd


def _cmp_prompt_body(x_ref, wc_ref, b1_ref, w2_ref, o_ref, *, n_seg):
    acc = jnp.zeros((n_seg, 8 * HEAD_DIM), f32)
    for s in range(CMP_STRIDE):
        xs = x_ref[0, :, s * 8 * HEAD_DIM: s * 8 * HEAD_DIM + 4 * HEAD_DIM].astype(bf16)
        acc = acc + jnp.dot(xs, wc_ref[s], preferred_element_type=f32)
    first = acc[:, :4 * HEAD_DIM]
    second = pltpu.roll(acc[:, 4 * HEAD_DIM:], n_seg - 1, 0)
    hid = _silu(b1_ref[...] + first + second)
    out = jnp.dot(hid.astype(bf16), w2_ref[...], preferred_element_type=f32)
    for qi in range(2 * NSA_KV_HEADS):
        o_ref[0, qi] = out[:, qi * HEAD_DIM:(qi + 1) * HEAD_DIM]


def _compress_prompt(kv3, wc, b1t, w2bd):
    B, T, _ = kv3.shape
    n_seg = T // CMP_STRIDE
    xseg = kv3.reshape(B, n_seg, CMP_STRIDE * 8 * HEAD_DIM)
    return pl.pallas_call(
        functools.partial(_cmp_prompt_body, n_seg=n_seg),
        grid=(B,),
        in_specs=[pl.BlockSpec((1, n_seg, CMP_STRIDE * 8 * HEAD_DIM), lambda b: (b, 0, 0)),
                  pl.BlockSpec((CMP_STRIDE, 4 * HEAD_DIM, 8 * HEAD_DIM), lambda b: (0, 0, 0)),
                  pl.BlockSpec((1, 4 * HEAD_DIM), lambda b: (0, 0)),
                  pl.BlockSpec((4 * HEAD_DIM, 4 * HEAD_DIM), lambda b: (0, 0))],
        out_specs=pl.BlockSpec((1, 4, n_seg, HEAD_DIM), lambda b: (b, 0, 0, 0)),
        out_shape=jax.ShapeDtypeStruct((B, 4, n_seg, HEAD_DIM), f32),
        compiler_params=_cparams(("parallel",)),
        name="compress_prompt",
    )(xseg, wc, b1t, w2bd)


def _topk_rows(score, k):
    n = score.shape[0]
    idx = lax.broadcasted_iota(jnp.int32, score.shape, 0)
    taken = jnp.zeros(score.shape, jnp.bool_)
    for _ in range(k):
        work = jnp.where(taken, -jnp.inf, score)
        m = jnp.max(work, axis=0, keepdims=True)
        cand = jnp.where((work == m) & jnp.logical_not(taken), idx, n)
        first = jnp.min(cand, axis=0, keepdims=True)
        taken = taken | (idx == first)
    return jnp.where(taken, 1.0, 0.0)


def _topk_rows_by_rank(score, k):
    n = score.shape[0]
    idx = lax.broadcasted_iota(jnp.int32, score.shape, 0)
    rank = jnp.zeros(score.shape, jnp.int32)
    for i in range(n):
        row = score[i:i + 1, :]
        ahead = (row > score) | ((row == score) & (idx > i))
        rank = rank + ahead.astype(jnp.int32)
    return jnp.where(rank < k, 1.0, 0.0)


def _softmax_rows(s, mask):
    s = jnp.where(mask, s, -jnp.inf)
    m = jnp.max(s, axis=-1, keepdims=True)
    m = jnp.where(m == -jnp.inf, 0.0, m)
    p = jnp.exp(s - m)
    return p / jnp.maximum(jnp.sum(p, axis=-1, keepdims=True), 1e-30)


def _attend_tile(qs, kbs, vbs, biases, carries, tq):
    n = range(len(qs))
    R = qs[0].shape[0]
    tk = kbs[0].shape[0]
    ss = [lax.dot_general(qs[i], kbs[i], _NT, preferred_element_type=f32) for i in n]
    ss = [(ss[i].reshape(NSA_GROUP, tq, tk) + biases[i][None]).reshape(R, tk) for i in n]
    m_news = [jnp.maximum(carries[i][0], jnp.max(ss[i], axis=-1, keepdims=True)) for i in n]
    m_safes = [jnp.where(m == -jnp.inf, 0.0, m) for m in m_news]
    alphas = [jnp.exp(carries[i][0] - m_safes[i]) for i in n]
    ps = [jnp.exp(ss[i] - m_safes[i]) for i in n]
    ls = [alphas[i] * carries[i][1] + jnp.sum(ps[i], axis=-1, keepdims=True) for i in n]
    accs = [alphas[i] * carries[i][2] + jnp.dot(ps[i].astype(bf16), vbs[i], preferred_element_type=f32) for i in n]
    return [(m_news[i], ls[i], accs[i]) for i in n]


def _flash_init(R):
    return (jnp.full((R, 1), -jnp.inf, f32), jnp.zeros((R, 1), f32), jnp.zeros((R, HEAD_DIM), f32))


def _nsa_prompt_body(q_ref, qr_ref, kc_ref, ks_ref, vs_ref, kw_ref, vw_ref, sm_ref, ov_ref, ex_ref,
                     o_ref, wbias_ref, *, tq, tk, tkw, n_wt, T, n_cmp, n_blk):
    i = pl.program_id(1)
    t0 = i * tq
    scale = HEAD_DIM ** -0.5
    G = range(NSA_KV_HEADS)
    R = NSA_GROUP * tq
    gates = jax.nn.sigmoid(sm_ref[0])
    n_pad = kc_ref.shape[2]
    tpos_r = t0 + lax.broadcasted_iota(jnp.int32, (tq, 1), 0)
    tpos_c = t0 + lax.broadcasted_iota(jnp.int32, (1, tq), 1)
    eye = (lax.broadcasted_iota(jnp.int32, (tq, tq), 0) == lax.broadcasted_iota(jnp.int32, (tq, tq), 1)).astype(bf16)
    heads = lambda x, g: [(x[:, (g * NSA_GROUP + r) * HEAD_DIM:(g * NSA_GROUP + r + 1) * HEAD_DIM] * scale).astype(bf16)
                          for r in range(NSA_GROUP)]

    o_cmps, imps = [], []
    nn = lax.broadcasted_iota(jnp.int32, (1, n_pad), 1)
    cmask = (nn * CMP_STRIDE + (CMP_BLOCK - 1) <= tpos_r) & (nn < n_cmp)
    for g in G:
        qh = heads(q_ref[0], g)
        kc = kc_ref[0, g].astype(bf16)
        vc = kc_ref[0, NSA_KV_HEADS + g].astype(bf16)
        ps = [_softmax_rows(lax.dot_general(qh[r], kc, _NT, preferred_element_type=f32), cmask) for r in range(NSA_GROUP)]
        o_cmps.append([jnp.dot(p.astype(bf16), vc, preferred_element_type=f32) for p in ps])
        imps.append(ps[0] + ps[1] + ps[2] + ps[3])

    score = jnp.concatenate([lax.dot_general(ov_ref[...], imps[g], _NT, preferred_element_type=f32, precision=_HI)[:n_blk]
                             for g in G], axis=1)
    jj = lax.broadcasted_iota(jnp.int32, score.shape, 0)
    jt = jnp.concatenate([tpos_c] * NSA_KV_HEADS, axis=1) // SEL_BLOCK
    valid = jj <= jt
    forced = valid & ((jj == 0) | (jj == jt) | (jj == jt - 1))
    score = jnp.where(forced, jnp.inf, jnp.where(valid, score, -jnp.inf))
    sel_t = _topk_rows_by_rank(score, min(N_SELECT, n_blk)).astype(bf16)
    if n_blk < LANES:
        sel_t = jnp.concatenate([sel_t, jnp.zeros((LANES - n_blk, NSA_KV_HEADS * tq), bf16)], axis=0)
    sels = [lax.dot_general(eye, sel_t[:, g * tq:(g + 1) * tq], _NT, preferred_element_type=f32).astype(bf16)
            for g in G]

    w0 = pl.multiple_of(jnp.clip(t0 - WINDOW, 0, T - n_wt * tkw), tq)
    for j in range(n_wt):
        dist = tpos_r - (w0 + j * tkw + lax.broadcasted_iota(jnp.int32, (1, tkw), 1))
        wbias_ref[j] = jnp.where((dist >= 0) & (dist < WINDOW), 0.0, -jnp.inf)

    stack = lambda x, g: (jnp.concatenate([x[:, (g * NSA_GROUP + r) * HEAD_DIM:(g * NSA_GROUP + r + 1) * HEAD_DIM]
                                           for r in range(NSA_GROUP)], axis=0) * scale).astype(bf16)
    qrs = [stack(qr_ref[0], g) for g in G]
    kv_tile = lambda ref, k0, n, g: ref[0, pl.ds(k0, n), g * HEAD_DIM:(g + 1) * HEAD_DIM].astype(bf16)

    def sel_step(kt, carries):
        k0 = pl.multiple_of(kt * tk, tk)
        causal = k0 + lax.broadcasted_iota(jnp.int32, (1, tk), 1) <= tpos_r
        biases = [jnp.where((jnp.dot(sels[g], ex_ref[kt], preferred_element_type=f32) > 0.5) & causal, 0.0, -jnp.inf)
                  for g in G]
        return tuple(_attend_tile(qrs, [kv_tile(ks_ref, k0, tk, g) for g in G], [kv_tile(vs_ref, k0, tk, g) for g in G],
                                  biases, carries, tq))

    sel = lax.fori_loop(0, (t0 + tq + tk - 1) // tk, sel_step, tuple(_flash_init(R) for _ in G))

    win = [_flash_init(R) for _ in G]
    for j in range(n_wt):
        k0 = pl.multiple_of(w0 + j * tkw, tq)
        win = _attend_tile(qrs, [kv_tile(kw_ref, k0, tkw, g) for g in G], [kv_tile(vw_ref, k0, tkw, g) for g in G],
                           [wbias_ref[j]] * NSA_KV_HEADS, win, tq)

    for g in G:
        o_sel = sel[g][2] / jnp.maximum(sel[g][1], 1e-30)
        o_win = win[g][2] / jnp.maximum(win[g][1], 1e-30)
        outs = []
        for r in range(NSA_GROUP):
            hcol = GATE_COL + g * NSA_GROUP + r
            rows = slice(r * tq, (r + 1) * tq)
            outs.append(gates[:, hcol:hcol + 1] * o_cmps[g][r]
                        + gates[:, NSA_HEADS + hcol:NSA_HEADS + hcol + 1] * o_sel[rows]
                        + gates[:, 2 * NSA_HEADS + hcol:2 * NSA_HEADS + hcol + 1] * o_win[rows])
        o_ref[0, :, g * NSA_GROUP * HEAD_DIM:(g + 1) * NSA_GROUP * HEAD_DIM] = jnp.concatenate(outs, axis=1)


def _overlap_t(n_blk_pad, n_cmp, n_pad):
    ci = np.arange(n_pad)[None, :] * CMP_STRIDE
    sj = np.arange(n_blk_pad)[:, None] * SEL_BLOCK
    ov = (ci < sj + SEL_BLOCK) & (ci + CMP_BLOCK > sj) & (np.arange(n_pad)[None, :] < n_cmp)
    return jnp.asarray(ov.astype(np.float32))


def _nsa_prompt(q3, qr3, kc, kv3, win3, sm3):
    B, T, _ = q3.shape
    tq = 128
    tk = 512 if T >= 1024 else 256
    tkw = WINDOW + tq if T >= 1024 else 128
    n_wt = -(-min(T, WINDOW + tq) // tkw)
    assert T % tk == 0 and n_wt * tkw <= T
    n_seg = T // CMP_STRIDE
    n_cmp = n_seg - CMP_BLOCK // CMP_STRIDE + 1
    n_blk = -(-T // SEL_BLOCK)
    ov = _overlap_t(LANES, n_cmp, n_seg)
    key_blk = (np.arange(T // tk)[:, None, None] * tk + np.arange(tk)[None, None, :]) // SEL_BLOCK
    ex = jnp.asarray((np.arange(LANES)[None, :, None] == key_blk).astype(np.float32)).astype(bf16)
    qspec = pl.BlockSpec((1, tq, NSA_WIDTH), lambda b, i: (b, i, 0))
    col = lambda j: pl.BlockSpec((1, T, LANES), lambda b, i: (b, 0, j))
    return pl.pallas_call(
        functools.partial(_nsa_prompt_body, tq=tq, tk=tk, tkw=tkw, n_wt=n_wt, T=T, n_cmp=n_cmp, n_blk=n_blk),
        grid=(B, T // tq),
        in_specs=[qspec, qspec,
                  pl.BlockSpec((1, 4, n_seg, HEAD_DIM), lambda b, i: (b, 0, 0, 0)),
                  col(2), col(3), col(0), col(1),
                  pl.BlockSpec((1, tq, LANES), lambda b, i: (b, i, 0)),
                  pl.BlockSpec((LANES, n_seg), lambda b, i: (0, 0)),
                  pl.BlockSpec((T // tk, LANES, tk), lambda b, i: (0, 0, 0))],
        out_specs=qspec,
        out_shape=jax.ShapeDtypeStruct((B, T, NSA_WIDTH), f32),
        scratch_shapes=[pltpu.VMEM((n_wt, tq, tkw), f32)],
        compiler_params=_cparams(("parallel", "arbitrary")),
        name="nsa_prompt",
    )(q3, qr3, kc, kv3, kv3, win3, win3, sm3, ov, ex)


def _out_body(x_ref, ro_ref, no_ref, ng_ref, go_ref, w_ref, g_ref, y_ref):
    no = no_ref[...] * _silu(ng_ref[...])
    z = (jnp.dot(ro_ref[...].astype(bf16), w_ref[0:RET_WIDTH, :], preferred_element_type=f32)
         + jnp.dot(no.astype(bf16), w_ref[RET_WIDTH:RET_WIDTH + NSA_WIDTH, :], preferred_element_type=f32)
         + jnp.dot(go_ref[...].astype(bf16), w_ref[RET_WIDTH + NSA_WIDTH:, :], preferred_element_type=f32))
    y_ref[...] = x_ref[...] + z * lax.rsqrt(jnp.mean(z * z, axis=-1, keepdims=True) + EPS) * g_ref[...]


def _out(x2d, ro, no, ng, go, w_out_b, gain, tm):
    M = x2d.shape[0]
    row = lambda w: pl.BlockSpec((tm, w), lambda i: (i, 0))
    return pl.pallas_call(
        _out_body,
        grid=(M // tm,),
        in_specs=[row(D_MODEL), row(RET_WIDTH), row(NSA_WIDTH), row(NSA_WIDTH), row(GDN_WIDTH),
                  pl.BlockSpec((D_MODEL, D_MODEL), lambda i: (0, 0)),
                  pl.BlockSpec((1, D_MODEL), lambda i: (0, 0))],
        out_specs=row(D_MODEL),
        out_shape=jax.ShapeDtypeStruct((M, D_MODEL), f32),
        compiler_params=_cparams(("parallel",)),
        name="out",
    )(x2d, ro, no, ng, go, w_out_b, gain)


PAGES_PER_STEP = 16


def _page_specs(l, pg, half):
    return [pl.BlockSpec((None, None, 4 * HEAD_DIM, PAGE_SIZE),
                         lambda b, j, pt, i=i: (l, pt[b, j * pg + i], half, 0)) for i in range(pg)]


def _cmp_sample_body(pt_ref, *refs, pg):
    page_refs, wp_ref, o_ref, xt_ref = refs[:pg], refs[pg], refs[pg + 1], refs[pg + 2]
    for i in range(pg):
        zt = page_refs[i][...].T
        xt_ref[0, i * PAGE_SIZE:(i + 1) * PAGE_SIZE, :] = zt[:, 0:LANES]
        xt_ref[1, i * PAGE_SIZE:(i + 1) * PAGE_SIZE, :] = zt[:, LANES:2 * LANES]
    n_row = pg * PAGE_SIZE // CMP_STRIDE
    accs = [jnp.zeros((n_row, 4 * HEAD_DIM), f32) for _ in range(2)]
    for s2 in range(CMP_STRIDE // 2):
        for pr in range(2):
            xs = jnp.concatenate([xt_ref[pr, pl.ds(2 * s2, n_row, stride=CMP_STRIDE), :],
                                  xt_ref[pr, pl.ds(2 * s2 + 1, n_row, stride=CMP_STRIDE), :]], axis=1).astype(bf16)
            accs[pr] = accs[pr] + jnp.dot(xs, wp_ref[pr, s2], preferred_element_type=f32)
    o_ref[0] = jnp.concatenate([accs[0][:, :LANES], accs[1][:, :LANES], accs[0][:, LANES:], accs[1][:, LANES:]], axis=1)


def _cmp_sample(cache_t, l, page_table, wp):
    B, n_pages = page_table.shape
    pg = PAGES_PER_STEP
    n_row = pg * PAGE_SIZE // CMP_STRIDE
    return pl.pallas_call(
        functools.partial(_cmp_sample_body, pg=pg),
        grid_spec=pltpu.PrefetchScalarGridSpec(
            num_scalar_prefetch=1,
            grid=(B, n_pages // pg),
            in_specs=_page_specs(l, pg, 0) + [pl.BlockSpec((2, CMP_STRIDE // 2, 4 * HEAD_DIM, 4 * HEAD_DIM),
                                                           lambda b, j, pt: (0, 0, 0, 0))],
            out_specs=pl.BlockSpec((1, n_row, 8 * HEAD_DIM), lambda b, j, pt: (b, j, 0)),
            scratch_shapes=[pltpu.VMEM((2, pg * PAGE_SIZE, LANES), f32)]),
        out_shape=jax.ShapeDtypeStruct((B, n_pages * PAGE_SIZE // CMP_STRIDE, 8 * HEAD_DIM), f32),
        compiler_params=_cparams(("parallel", "arbitrary")),
        name="compress_sample",
    )(page_table, *([cache_t] * pg), wp)


def _stack_heads(x, g):
    return jnp.concatenate([x[:, (g * NSA_GROUP + r) * HEAD_DIM:(g * NSA_GROUP + r + 1) * HEAD_DIM]
                            for r in range(NSA_GROUP)], axis=0)


def _nsa_sample_mid_body(p_ref, b1_ref, w2_ref, q_ref, qr_ref, win_ref, st_ref, ov_ref,
                         ocmp_ref, owin_ref, selt_ref, *, Ts, n_cmp, past_len, wb):
    P = p_ref[0]
    n_seg = P.shape[0]
    hid = _silu(b1_ref[...] + P[:, :4 * HEAD_DIM] + pltpu.roll(P[:, 4 * HEAD_DIM:], n_seg - 1, 0))
    kcv = jnp.dot(hid.astype(bf16), w2_ref[...], preferred_element_type=f32)
    scale = HEAD_DIM ** -0.5
    R = NSA_GROUP * Ts
    trow = lax.broadcasted_iota(jnp.int32, (R, 1), 0) % Ts
    imps = []
    for g in range(NSA_KV_HEADS):
        qs = (_stack_heads(q_ref[0], g) * scale).astype(bf16)
        kc = kcv[:, g * HEAD_DIM:(g + 1) * HEAD_DIM].astype(bf16)
        vc = kcv[:, (NSA_KV_HEADS + g) * HEAD_DIM:(NSA_KV_HEADS + g + 1) * HEAD_DIM].astype(bf16)
        s = lax.dot_general(qs, kc, _NT, preferred_element_type=f32)
        nn = lax.broadcasted_iota(jnp.int32, (1, n_seg), 1)
        cmask = (nn * CMP_STRIDE + (CMP_BLOCK - 1) <= past_len + trow) & (nn < n_cmp)
        p = _softmax_rows(s, cmask)
        ocmp_ref[0, g] = jnp.dot(p.astype(bf16), vc, preferred_element_type=f32)
        imps.append(jnp.sum(p.reshape(NSA_GROUP, Ts, n_seg), axis=0))
        qrs = (_stack_heads(qr_ref[0], g) * scale).astype(bf16)
        kwt = st_ref[g * HEAD_DIM:(g + 1) * HEAD_DIM, :].astype(bf16)
        vwt = st_ref[(NSA_KV_HEADS + g) * HEAD_DIM:(NSA_KV_HEADS + g + 1) * HEAD_DIM, :].astype(bf16)
        wn = win_ref[0]
        knew = wn[:, g * HEAD_DIM:(g + 1) * HEAD_DIM].astype(bf16)
        vnew = wn[:, (NSA_KV_HEADS + g) * HEAD_DIM:(NSA_KV_HEADS + g + 1) * HEAD_DIM].astype(bf16)
        s_b = jnp.dot(qrs, kwt, preferred_element_type=f32)
        s_n = lax.dot_general(qrs, knew, _NT, preferred_element_type=f32)
        jb = lax.broadcasted_iota(jnp.int32, (1, wb), 1)
        dist_b = trow + wb - jb
        s_b = jnp.where((dist_b >= 0) & (dist_b < WINDOW) & (past_len - wb + jb >= 0), s_b, -jnp.inf)
        dist_n = trow - lax.broadcasted_iota(jnp.int32, (1, Ts), 1)
        s_n = jnp.where((dist_n >= 0) & (dist_n < WINDOW), s_n, -jnp.inf)
        m = jnp.maximum(jnp.max(s_b, axis=-1, keepdims=True), jnp.max(s_n, axis=-1, keepdims=True))
        m = jnp.where(m == -jnp.inf, 0.0, m)
        pb = jnp.exp(s_b - m)
        pn = jnp.exp(s_n - m)
        den = jnp.sum(pb, axis=-1, keepdims=True) + jnp.sum(pn, axis=-1, keepdims=True)
        o = (lax.dot_general(pb.astype(bf16), vwt, _NT, preferred_element_type=f32)
             + jnp.dot(pn.astype(bf16), vnew, preferred_element_type=f32))
        owin_ref[0, g] = o / jnp.maximum(den, 1e-30)

    imp2 = jnp.concatenate(imps, axis=0)
    score = lax.dot_general(ov_ref[...], imp2, _NT, preferred_element_type=f32, precision=_HI)
    jj = lax.broadcasted_iota(jnp.int32, score.shape, 0)
    jt = (past_len + lax.broadcasted_iota(jnp.int32, score.shape, 1) % Ts) // SEL_BLOCK
    valid = jj <= jt
    forced = valid & ((jj == 0) | (jj == jt) | (jj == jt - 1))
    score = jnp.where(forced, jnp.inf, jnp.where(valid, score, -jnp.inf))
    selt_ref[0] = _topk_rows(score, N_SELECT)


def _nsa_sample_mid(P, b1t, w2bd, q3, qr3, win3, st_t, l, past_len):
    B, Ts, _ = q3.shape
    n_seg = P.shape[1]
    n_cmp = n_seg - CMP_BLOCK // CMP_STRIDE + 1
    n_blk = -(-(past_len + Ts) // SEL_BLOCK)
    nb_pad = -(-n_blk // 8) * 8
    wb = st_t.shape[-1]
    ov = _overlap_t(nb_pad, n_cmp, n_seg)
    R = NSA_GROUP * Ts
    tok = lambda w: pl.BlockSpec((1, Ts, w), lambda b: (b, 0, 0))
    return pl.pallas_call(
        functools.partial(_nsa_sample_mid_body, Ts=Ts, n_cmp=n_cmp, past_len=past_len, wb=wb),
        grid=(B,),
        in_specs=[pl.BlockSpec((1, n_seg, 8 * HEAD_DIM), lambda b: (b, 0, 0)),
                  pl.BlockSpec((1, 4 * HEAD_DIM), lambda b: (0, 0)),
                  pl.BlockSpec((4 * HEAD_DIM, 4 * HEAD_DIM), lambda b: (0, 0)),
                  tok(NSA_WIDTH), tok(NSA_WIDTH), tok(4 * HEAD_DIM),
                  pl.BlockSpec((None, None, 4 * HEAD_DIM, wb), lambda b: (l, b, 0, 0)),
                  pl.BlockSpec((nb_pad, n_seg), lambda b: (0, 0))],
        out_specs=[pl.BlockSpec((1, NSA_KV_HEADS, R, HEAD_DIM), lambda b: (b, 0, 0, 0)),
                   pl.BlockSpec((1, NSA_KV_HEADS, R, HEAD_DIM), lambda b: (b, 0, 0, 0)),
                   pl.BlockSpec((1, nb_pad, NSA_KV_HEADS * Ts), lambda b: (b, 0, 0))],
        out_shape=[jax.ShapeDtypeStruct((B, NSA_KV_HEADS, R, HEAD_DIM), f32),
                   jax.ShapeDtypeStruct((B, NSA_KV_HEADS, R, HEAD_DIM), f32),
                   jax.ShapeDtypeStruct((B, nb_pad, NSA_KV_HEADS * Ts), f32)],
        compiler_params=_cparams(("parallel",)),
        name="nsa_sample_mid",
    )(P, b1t, w2bd, q3, qr3, win3, st_t, ov)


def _nsa_sample_sel_body(pt_ref, *refs, pg, Ts, n_steps):
    page_refs = refs[:pg]
    qr_ref, kvn_ref, selt_ref, ex_ref, ocmp_ref, owin_ref, sm_ref, o_ref, m_ref, l_ref, acc_ref = refs[pg:]
    j = pl.program_id(1)
    scale = HEAD_DIM ** -0.5
    R = NSA_GROUP * Ts
    GT = NSA_KV_HEADS * Ts
    nb = 2 * pg
    trow = lax.broadcasted_iota(jnp.int32, (R, 1), 0) % Ts
    eye = (lax.broadcasted_iota(jnp.int32, (GT, GT), 0) == lax.broadcasted_iota(jnp.int32, (GT, GT), 1)).astype(bf16)
    sel_j = selt_ref[0, pl.ds(pl.multiple_of(j * nb, nb), nb), :].astype(bf16)
    sel_j = jnp.concatenate([sel_j, jnp.zeros((LANES - nb, GT), bf16)], axis=0)
    sel_rows = lax.dot_general(eye, sel_j, _NT, preferred_element_type=f32)
    full = jnp.dot(sel_rows.astype(bf16), ex_ref[...], preferred_element_type=f32)

    G = range(NSA_KV_HEADS)
    qrs = [(_stack_heads(qr_ref[0], g) * scale).astype(bf16) for g in G]

    @pl.when(j == 0)
    def _():
        kvn = kvn_ref[0]
        for g in G:
            knew = kvn[:, (2 * NSA_KV_HEADS + g) * HEAD_DIM:(2 * NSA_KV_HEADS + g + 1) * HEAD_DIM].astype(bf16)
            vnew = kvn[:, (3 * NSA_KV_HEADS + g) * HEAD_DIM:(3 * NSA_KV_HEADS + g + 1) * HEAD_DIM].astype(bf16)
            s_n = lax.dot_general(qrs[g], knew, _NT, preferred_element_type=f32)
            s_n = jnp.where(lax.broadcasted_iota(jnp.int32, (1, Ts), 1) <= trow, s_n, -jnp.inf)
            m0 = jnp.max(s_n, axis=-1, keepdims=True)
            p0 = jnp.exp(s_n - m0)
            m_ref[g] = m0
            l_ref[g] = jnp.sum(p0, axis=-1, keepdims=True)
            acc_ref[g] = jnp.dot(p0.astype(bf16), vnew, preferred_element_type=f32)

    kts = [jnp.concatenate([page_refs[i][g * HEAD_DIM:(g + 1) * HEAD_DIM, :] for i in range(pg)], axis=1).astype(bf16)
           for g in G]
    vts = [jnp.concatenate([page_refs[i][(NSA_KV_HEADS + g) * HEAD_DIM:(NSA_KV_HEADS + g + 1) * HEAD_DIM, :]
                            for i in range(pg)], axis=1).astype(bf16) for g in G]
    ss = [jnp.dot(qrs[g], kts[g], preferred_element_type=f32) for g in G]
    ss = [jnp.where(jnp.concatenate([full[g * Ts:(g + 1) * Ts]] * NSA_GROUP, axis=0) > 0.5, ss[g], -jnp.inf) for g in G]
    m_olds = [m_ref[g] for g in G]
    m_news = [jnp.maximum(m_olds[g], jnp.max(ss[g], axis=-1, keepdims=True)) for g in G]
    alphas = [jnp.exp(m_olds[g] - m_news[g]) for g in G]
    ps = [jnp.exp(ss[g] - m_news[g]) for g in G]
    pvs = [lax.dot_general(ps[g].astype(bf16), vts[g], _NT, preferred_element_type=f32) for g in G]
    for g in G:
        l_ref[g] = alphas[g] * l_ref[g] + jnp.sum(ps[g], axis=-1, keepdims=True)
        acc_ref[g] = alphas[g] * acc_ref[g] + pvs[g]
        m_ref[g] = m_news[g]


    @pl.when(j == n_steps - 1)
    def _():
        gates = jax.nn.sigmoid(sm_ref[0])
        outs = []
        for g in range(NSA_KV_HEADS):
            o_sel = acc_ref[g] / jnp.maximum(l_ref[g], 1e-30)
            o_cmp = ocmp_ref[0, g]
            o_win = owin_ref[0, g]
            for r in range(NSA_GROUP):
                hcol = GATE_COL + g * NSA_GROUP + r
                rows = slice(r * Ts, (r + 1) * Ts)
                outs.append(gates[:, hcol:hcol + 1] * o_cmp[rows]
                            + gates[:, NSA_HEADS + hcol:NSA_HEADS + hcol + 1] * o_sel[rows]
                            + gates[:, 2 * NSA_HEADS + hcol:2 * NSA_HEADS + hcol + 1] * o_win[rows])
        o_ref[0] = jnp.concatenate(outs, axis=1)


def _nsa_sample_sel(cache_t, l, page_table, qr3, kv3, selt, ocmp, owin, sm3):
    B, n_pages = page_table.shape
    Ts = qr3.shape[1]
    pg = PAGES_PER_STEP
    n_steps = n_pages // pg
    R = NSA_GROUP * Ts
    ex = jnp.asarray((np.arange(LANES)[:, None] == (np.arange(pg * PAGE_SIZE)[None, :] // SEL_BLOCK)).astype(np.float32)).astype(bf16)
    tok = lambda w: pl.BlockSpec((1, Ts, w), lambda b, j, pt: (b, 0, 0))
    stk = pl.BlockSpec((1, NSA_KV_HEADS, R, HEAD_DIM), lambda b, j, pt: (b, 0, 0, 0))
    return pl.pallas_call(
        functools.partial(_nsa_sample_sel_body, pg=pg, Ts=Ts, n_steps=n_steps),
        grid_spec=pltpu.PrefetchScalarGridSpec(
            num_scalar_prefetch=1,
            grid=(B, n_steps),
            in_specs=_page_specs(l, pg, 1) + [
                tok(NSA_WIDTH), tok(8 * HEAD_DIM),
                pl.BlockSpec((1, selt.shape[1], selt.shape[2]), lambda b, j, pt: (b, 0, 0)),
                pl.BlockSpec((LANES, pg * PAGE_SIZE), lambda b, j, pt: (0, 0)),
                stk, stk, tok(LANES)],
            out_specs=tok(NSA_WIDTH),
            scratch_shapes=[pltpu.VMEM((NSA_KV_HEADS, R, 1), f32), pltpu.VMEM((NSA_KV_HEADS, R, 1), f32),
                            pltpu.VMEM((NSA_KV_HEADS, R, HEAD_DIM), f32)]),
        out_shape=jax.ShapeDtypeStruct((B, Ts, NSA_WIDTH), f32),
        compiler_params=_cparams(("parallel", "arbitrary")),
        name="nsa_sample_sel",
    )(page_table, *([cache_t] * pg), qr3, kv3, selt, ex, ocmp, owin, sm3)


def _nsa_sample(q3, qr3, kv3, win3, sm3, cache_t, st_t, l, page_table, wp, b1t, w2bd):
    Ts = q3.shape[1]
    n_pages = page_table.shape[1]
    past_len = n_pages * PAGE_SIZE
    assert (past_len + Ts) // CMP_STRIDE * CMP_STRIDE <= past_len
    assert past_len % SEL_BLOCK == 0 and Ts <= SEL_BLOCK and n_pages % PAGES_PER_STEP == 0
    P = _cmp_sample(cache_t, l, page_table, wp)
    ocmp, owin, selt = _nsa_sample_mid(P, b1t, w2bd, q3, qr3, win3, st_t, l, past_len)
    return _nsa_sample_sel(cache_t, l, page_table, qr3, kv3, selt, ocmp, owin, sm3)


def _rope_tables(pos):
    half = HEAD_DIM // 2
    inv = ROPE_THETA ** (-jnp.arange(half, dtype=f32) / half)
    ang = pos.astype(f32)[:, None] * inv[None, :]
    cos, sin = jnp.cos(ang), jnp.sin(ang)
    return jnp.concatenate([cos, cos, cos, cos], axis=1), jnp.concatenate([-sin, sin, -sin, sin], axis=1)


def _prep_w_in(w):
    cols = [w[:, _OFF[n][0]:_OFF[n][1]] for n in _PAD_ORDER]
    used = sum(c.shape[1] for c in cols)
    cols.append(jnp.zeros((w.shape[0], PROJ_PAD - used), w.dtype))
    return jnp.concatenate(cols, axis=1).astype(bf16)


def _layer(x, pos, prm, l, nsa_fn, ret_state, gdn_state, conv_buf, tm):
    B, T, _ = x.shape
    M = B * T
    cosf, sinf = _rope_tables(pos)
    if tm > T:
        cosf, sinf = jnp.tile(cosf, (tm // T, 1)), jnp.tile(sinf, (tm // T, 1))
    x2d = x.reshape(M, D_MODEL)
    ret, q, qr, kv, win, ng, gq, gg, sm = _proj(x2d, prm['norm_pre'][l][None], prm['w_in_p'][l], cosf, sinf, tm)
    r3 = lambda a: a.reshape(B, T, a.shape[-1])
    ro, ret_new = _retention(r3(ret), ret_state, prm['ret_gn'][l][None], T)
    go, gdn_new = _gdn(r3(gq), r3(sm), r3(gg), conv_buf, prm['gdn_conv'][l], prm['gdn_a_log'][l],
                       prm['gdn_dt_bias'][l], prm['gdn_norm'][l][None], gdn_state, T)
    assert T >= CONV_WIDTH - 1
    conv_new = r3(gq)[:, T - (CONV_WIDTH - 1):]
    no, win_new = nsa_fn(r3(q), r3(qr), r3(kv), r3(win), r3(sm))
    y = _out(x2d, ro.reshape(M, -1), no.reshape(M, -1), ng, go.reshape(M, -1), prm['w_out_b'][l],
             prm['norm_post'][l][None], tm)
    rows = kv.reshape(B, T, 4, NSA_KV_HEADS, HEAD_DIM)
    return y.reshape(B, T, D_MODEL), (rows, win_new, ret_new, gdn_new, conv_new)


def kernel(x_prompt, x_sample, cache_nsa_kv, page_table, state_nsa_win, state_ret, state_gdn, state_gdn_conv,
           w_in, w_out, norm_pre, norm_post, ret_gn, gdn_norm, gdn_conv, gdn_a_log, gdn_dt_bias,
           cmp_w1, cmp_b1, cmp_w2):
    Bp, T, _ = x_prompt.shape
    Bs, Ts, _ = x_sample.shape
    depth = w_in.shape[0]
    past_len = page_table.shape[1] * PAGE_SIZE
    pos_p = jnp.arange(T, dtype=jnp.int32)
    pos_s = past_len + jnp.arange(Ts, dtype=jnp.int32)
    prm = {'w_in_p': [_prep_w_in(w_in[l]) for l in range(depth)],
           'w_out_b': [w_out[l].astype(bf16) for l in range(depth)],
           'norm_pre': norm_pre, 'norm_post': norm_post, 'ret_gn': ret_gn, 'gdn_norm': gdn_norm,
           'gdn_conv': gdn_conv, 'gdn_a_log': gdn_a_log, 'gdn_dt_bias': gdn_dt_bias}
    yp, ys = x_prompt, x_sample
    cache_t = jnp.transpose(cache_nsa_kv, (0, 1, 3, 4, 5, 2)).reshape(
        depth, cache_nsa_kv.shape[1], 8 * HEAD_DIM, PAGE_SIZE)
    st_t = jnp.transpose(state_nsa_win, (0, 1, 3, 4, 5, 2)).reshape(
        depth, Bs, 4 * HEAD_DIM, state_nsa_win.shape[2])
    st_p, st_s = [], []
    for l in range(depth):
        wc, wp, b1t, w2bd = _cmp_weights(cmp_w1[l], cmp_b1[l], cmp_w2[l])

        def nsa_prompt(q3, qr3, kv3, win3, sm3):
            kc = _compress_prompt(kv3, wc, b1t, w2bd)
            no = _nsa_prompt(q3, qr3, kc, kv3, win3, sm3)
            win_rows = win3.reshape(Bp, T, 2, NSA_KV_HEADS, HEAD_DIM)
            return no, win_rows[:, T - min(WINDOW, T):]

        def nsa_sample(q3, qr3, kv3, win3, sm3):
            no = _nsa_sample(q3, qr3, kv3, win3, sm3, cache_t, st_t, l, page_table, wp, b1t, w2bd)
            win_rows = win3.reshape(Bs, Ts, 2, NSA_KV_HEADS, HEAD_DIM)
            keys = jnp.concatenate([state_nsa_win[l], win_rows], axis=1)
            return no, keys[:, -state_nsa_win.shape[2]:]

        yp, sp = _layer(yp, pos_p, prm, l, nsa_prompt,
                        jnp.zeros((Bp, RET_HEADS, HEAD_DIM, HEAD_DIM), f32),
                        jnp.zeros((Bp, GDN_HEADS, HEAD_DIM, HEAD_DIM), f32),
                        jnp.zeros((Bp, CONV_WIDTH - 1, 3 * GDN_WIDTH), f32), 256)
        ys, ss = _layer(ys, pos_s, prm, l, nsa_sample, state_ret[l], state_gdn[l], state_gdn_conv[l], Bs * Ts)
        st_p.append(sp)
        st_s.append(ss)
    stk = lambda sts, i: jnp.stack([s[i] for s in sts])
    return (yp, ys, stk(st_p, 0), stk(st_s, 0), stk(st_p, 1), stk(st_s, 1), stk(st_p, 2), stk(st_s, 2),
            stk(st_p, 3), stk(st_s, 3), stk(st_p, 4), stk(st_s, 4))
```

```python
import functools
import math

import numpy as np
import jax
import jax.numpy as jnp
from jax import lax
from jax.experimental import pallas as pl
from jax.experimental.pallas import tpu as pltpu

f32 = jnp.float32
bf16 = jnp.bfloat16

D_MODEL = 1024
HEAD_DIM = 64
RET_WIDTH = 256
NSA_WIDTH = 512
GDN_WIDTH = 256
RET_HEADS = 4
NSA_HEADS = 8
NSA_KV_HEADS = 2
NSA_GROUP = 4
GDN_HEADS = 4
CMP_BLOCK = 32
CMP_STRIDE = 16
SEL_BLOCK = 64
N_SELECT = 16
WINDOW = 512
RET_CHUNK = 64
GDN_CHUNK = 64
GDN_CHUNKS_PER_STEP = 4
CONV_WIDTH = 4
PAGE_SIZE = 128
ROPE_THETA = 10000.0
EPS = 1e-6

LANES = 128
VMEM_LIMIT = 56 * 1024 * 1024
PROMPT_ROW_TILE = 512

_SPLITS = (('ret', 4 * RET_WIDTH), ('nsa_q', NSA_WIDTH), ('nsa_kv', 6 * NSA_KV_HEADS * HEAD_DIM),
           ('nsa_gate', 3 * NSA_HEADS), ('nsa_g', NSA_WIDTH), ('gdn_qkv', 3 * GDN_WIDTH),
           ('gdn_ba', 2 * GDN_HEADS), ('gdn_g', GDN_WIDTH))
_OFF = {}
_o = 0
for _n, _w in _SPLITS:
    _OFF[_n] = (_o, _o + _w)
    _o += _w
PROJ_WIDTH = _o
_PAD_ORDER = ('ret', 'nsa_q', 'nsa_kv', 'nsa_g', 'gdn_qkv', 'gdn_g', 'nsa_gate', 'gdn_ba')
_POFF = {}
_o = 0
for _n in _PAD_ORDER:
    _w = _OFF[_n][1] - _OFF[_n][0]
    _POFF[_n] = _o
    _o += _w
SMALL_OFF = _POFF['nsa_gate']
PROJ_PAD = SMALL_OFF + LANES
GATE_COL = 0
BETA_COL = 3 * NSA_HEADS
A_COL = BETA_COL + GDN_HEADS

_NT = (((1,), (1,)), ((), ()))
_TN = (((0,), (0,)), ((), ()))
_HI = lax.Precision.HIGHEST


def _silu(x):
    return x * jax.nn.sigmoid(x)


def _cparams(sem):
    return pltpu.CompilerParams(dimension_semantics=sem, vmem_limit_bytes=VMEM_LIMIT)


def _proj_body(x_ref, g_ref, w_ref, cos_ref, sin_ref,
               ret_ref, q_ref, qr_ref, kv_ref, win_ref, ng_ref, gq_ref, gg_ref, sm_ref):
    x = x_ref[...]
    h = x * lax.rsqrt(jnp.mean(x * x, axis=-1, keepdims=True) + EPS) * g_ref[...]
    hb = h.astype(bf16)
    cos = cos_ref[...]
    sin = sin_ref[...]
    lane = lax.broadcasted_iota(jnp.int32, cos.shape, 1)
    low = (lane % HEAD_DIM) < HEAD_DIM // 2

    def mm(c0, width):
        return jnp.dot(hb, w_ref[:, c0:c0 + width], preferred_element_type=f32)

    def rope(v):
        sw = jnp.where(low, pltpu.roll(v, LANES - HEAD_DIM // 2, 1), pltpu.roll(v, HEAD_DIM // 2, 1))
        return v * cos + sw * sin

    r = mm(_POFF['ret'], 4 * RET_WIDTH)
    for j in range(2):
        ret_ref[:, j * LANES:(j + 1) * LANES] = rope(r[:, j * LANES:(j + 1) * LANES])
    for j in range(2, 4):
        ret_ref[:, j * LANES:(j + 1) * LANES] = rope(r[:, j * LANES:(j + 1) * LANES]) * (HEAD_DIM ** -0.5)
    ret_ref[:, 2 * RET_WIDTH:] = r[:, 2 * RET_WIDTH:]

    q = mm(_POFF['nsa_q'], NSA_WIDTH)
    q_ref[...] = q
    for j in range(NSA_WIDTH // LANES):
        qr_ref[:, j * LANES:(j + 1) * LANES] = rope(q[:, j * LANES:(j + 1) * LANES])

    kv = mm(_POFF['nsa_kv'], 6 * LANES)
    kv_ref[:, 0:2 * LANES] = kv[:, 0:2 * LANES]
    kv_ref[:, 2 * LANES:3 * LANES] = rope(kv[:, 2 * LANES:3 * LANES])
    kv_ref[:, 3 * LANES:4 * LANES] = kv[:, 3 * LANES:4 * LANES]
    win_ref[:, 0:LANES] = rope(kv[:, 4 * LANES:5 * LANES])
    win_ref[:, LANES:2 * LANES] = kv[:, 5 * LANES:6 * LANES]

    ng_ref[...] = mm(_POFF['nsa_g'], NSA_WIDTH)
    gq_ref[...] = mm(_POFF['gdn_qkv'], 3 * GDN_WIDTH)
    gg_ref[...] = mm(_POFF['gdn_g'], GDN_WIDTH)
    sm_ref[...] = mm(SMALL_OFF, LANES)


def _proj(x2d, gain, wpad, cosf, sinf, tm):
    M = x2d.shape[0]
    n_pos = cosf.shape[0] // tm
    widths = (4 * RET_WIDTH, NSA_WIDTH, NSA_WIDTH, 4 * LANES, 2 * LANES, NSA_WIDTH, 3 * GDN_WIDTH, GDN_WIDTH, LANES)
    row = lambda w: pl.BlockSpec((tm, w), lambda i: (i, 0))
    return pl.pallas_call(
        _proj_body,
        grid=(M // tm,),
        in_specs=[row(D_MODEL),
                  pl.BlockSpec((1, D_MODEL), lambda i: (0, 0)),
                  pl.BlockSpec((D_MODEL, PROJ_PAD), lambda i: (0, 0)),
                  pl.BlockSpec((tm, LANES), lambda i: (i % n_pos, 0)),
                  pl.BlockSpec((tm, LANES), lambda i: (i % n_pos, 0))],
        out_specs=[row(w) for w in widths],
        out_shape=[jax.ShapeDtypeStruct((M, w), f32) for w in widths],
        compiler_params=_cparams(("parallel",)),
        name="proj",
    )(x2d, gain, wpad, cosf, sinf)


def _ret_body(ret_ref, dm_ref, qd_ref, kd_ref, gn_ref, s0_ref, o_ref, s_ref, *, c, nchunk, sdec):
    @pl.when(pl.program_id(1) == 0)
    def _():
        s_ref[...] = s0_ref[...]

    H = range(RET_HEADS)
    hsl = lambda x, h: x[:, h * HEAD_DIM:(h + 1) * HEAD_DIM]
    Ss = [s_ref[0, h] for h in H]
    for ci in range(nchunk):
        blk = ret_ref[0, ci * c:(ci + 1) * c, :]
        q_all = blk[:, 0:RET_WIDTH].astype(bf16)
        k_all = blk[:, RET_WIDTH:2 * RET_WIDTH]
        kb_all = k_all.astype(bf16)
        kd_all = (k_all * kd_ref[...]).astype(bf16)
        v_all = blk[:, 2 * RET_WIDTH:3 * RET_WIDTH].astype(bf16)
        gate = _silu(blk[:, 3 * RET_WIDTH:])
        atts = [(lax.dot_general(hsl(q_all, h), hsl(kb_all, h), _NT, preferred_element_type=f32) * dm_ref[h]).astype(bf16)
                for h in H]
        qss = [jnp.dot(hsl(q_all, h), Ss[h].astype(bf16), preferred_element_type=f32) for h in H]
        attvs = [jnp.dot(atts[h], hsl(v_all, h), preferred_element_type=f32) for h in H]
        kvs = [lax.dot_general(hsl(kd_all, h), hsl(v_all, h), _TN, preferred_element_type=f32) for h in H]
        Ss = [Ss[h] * sdec[h] + kvs[h] for h in H]
        qd = qd_ref[...]
        outs = []
        for h in H:
            o = attvs[h] + qss[h] * hsl(qd, h)
            mu = jnp.mean(o, axis=-1, keepdims=True)
            d = o - mu
            var = jnp.mean(d * d, axis=-1, keepdims=True)
            outs.append(d * lax.rsqrt(var + EPS))
        o_ref[0, ci * c:(ci + 1) * c, :] = jnp.concatenate(outs, axis=1) * gn_ref[...] * gate
    for h in H:
        s_ref[0, h] = Ss[h]


def _retention(ret3, s0, gn, T):
    B = ret3.shape[0]
    c = math.gcd(T, RET_CHUNK)
    nchunk = min(T // c, 4)
    tb = c * nchunk
    hh = jnp.arange(RET_HEADS, dtype=f32)
    lg = jnp.log1p(-jnp.exp2(-5.0 - hh))
    ii = jnp.arange(c, dtype=f32)
    rel = ii[:, None] - ii[None, :]
    dm = jnp.exp(jnp.where(rel[None] >= 0, rel[None] * lg[:, None, None], -jnp.inf))
    qd = jnp.repeat(jnp.exp((ii + 1.0)[:, None] * lg[None, :]), HEAD_DIM, axis=1)
    kd = jnp.repeat(jnp.exp((c - 1.0 - ii)[:, None] * lg[None, :]), HEAD_DIM, axis=1)
    sdec = tuple(float((1.0 - 2.0 ** (-5.0 - h)) ** c) for h in range(RET_HEADS))
    return pl.pallas_call(
        functools.partial(_ret_body, c=c, nchunk=nchunk, sdec=sdec),
        grid=(B, T // tb),
        in_specs=[pl.BlockSpec((1, tb, 4 * RET_WIDTH), lambda b, j: (b, j, 0)),
                  pl.BlockSpec((RET_HEADS, c, c), lambda b, j: (0, 0, 0)),
                  pl.BlockSpec((c, RET_WIDTH), lambda b, j: (0, 0)),
                  pl.BlockSpec((c, RET_WIDTH), lambda b, j: (0, 0)),
                  pl.BlockSpec((1, RET_WIDTH), lambda b, j: (0, 0)),
                  pl.BlockSpec((1, RET_HEADS, HEAD_DIM, HEAD_DIM), lambda b, j: (b, 0, 0, 0))],
        out_specs=[pl.BlockSpec((1, tb, RET_WIDTH), lambda b, j: (b, j, 0)),
                   pl.BlockSpec((1, RET_HEADS, HEAD_DIM, HEAD_DIM), lambda b, j: (b, 0, 0, 0))],
        out_shape=[jax.ShapeDtypeStruct((B, T, RET_WIDTH), f32),
                   jax.ShapeDtypeStruct((B, RET_HEADS, HEAD_DIM, HEAD_DIM), f32)],
        compiler_params=_cparams(("parallel", "arbitrary")),
        name="retention",
    )(ret3, dm, qd, kd, gn, s0)


def _softplus(x):
    return jnp.maximum(x, 0.0) + jnp.log1p(jnp.exp(-jnp.abs(x)))


def _split_bf16(x):
    hi = x.astype(bf16)
    return hi, (x - hi.astype(f32)).astype(bf16)


def _dot_split(a, b):
    d = lambda x, y: jnp.dot(x, y, preferred_element_type=f32)
    return d(a[0], b[0]) + d(a[0], b[1]) + d(a[1], b[0])


def _unit_lower_inverse(n_list, c):
    rr = lax.broadcasted_iota(jnp.int32, (c, c), 0)
    cc = lax.broadcasted_iota(jnp.int32, (c, c), 1)
    eye = (rr == cc).astype(f32)
    ress = [eye - n for n in n_list]
    pws = [_split_bf16(-n) for n in n_list]
    span = 2
    while span < c:
        pws = [_split_bf16(_dot_split(pw, pw)) for pw in pws]
        ress = [res + _dot_split(_split_bf16(res), pw) for res, pw in zip(ress, pws)]
        span *= 2
    return ress


def _gdn_body(x_ref, sm_ref, gg_ref, cb_ref, cw_ref, al_ref, dtb_ref, gnorm_ref, s0_ref,
              o_ref, s_ref, xc_ref, *, c, nchunk):
    tb = c * nchunk

    @pl.when(pl.program_id(1) == 0)
    def _():
        s_ref[...] = s0_ref[...]
        xc_ref[5:8, :] = cb_ref[0]

    xc_ref[8:8 + tb, :] = x_ref[0]
    y = cw_ref[0:1, :] * xc_ref[pl.ds(5, tb), :]
    for j in range(1, CONV_WIDTH):
        y = y + cw_ref[j:j + 1, :] * xc_ref[pl.ds(5 + j, tb), :]
    tail = xc_ref[pl.ds(5 + tb, 3), :]
    xc_ref[5:8, :] = tail
    y = _silu(y)

    sm = sm_ref[0]
    beta_all = jax.nn.sigmoid(sm)
    g_all = -jnp.exp(al_ref[...]) * _softplus(sm + dtb_ref[...])
    rr = lax.broadcasted_iota(jnp.int32, (c, c), 0)
    cc = lax.broadcasted_iota(jnp.int32, (c, c), 1)
    tri = rr >= cc
    strict = rr > cc
    trif = tri.astype(f32)
    triu = (rr <= cc).astype(f32)
    gg = gg_ref[0]
    H = range(GDN_HEADS)
    P = [(ci, h) for ci in range(nchunk) for h in H]

    qs, ks, vs, betas, Gcs, Ls, kbs, kbfs, n_list = {}, {}, {}, {}, {}, {}, {}, {}, []
    for ci in range(nchunk):
        rows = slice(ci * c, (ci + 1) * c)
        gch = g_all[rows]
        G_cols = jnp.dot(trif, gch, preferred_element_type=f32, precision=_HI)
        G_rows = jnp.dot(gch.T, triu, preferred_element_type=f32, precision=_HI)
        for h in H:
            p = (ci, h)
            hs = lambda base: y[rows, base + h * HEAD_DIM: base + (h + 1) * HEAD_DIM]
            q, k = hs(0), hs(GDN_WIDTH)
            qs[p] = q * lax.rsqrt(jnp.sum(q * q, axis=-1, keepdims=True) + EPS) * (HEAD_DIM ** -0.5)
            k = k * lax.rsqrt(jnp.sum(k * k, axis=-1, keepdims=True) + EPS)
            ks[p] = k
            vs[p] = hs(2 * GDN_WIDTH)
            betas[p] = beta_all[rows, BETA_COL + h: BETA_COL + h + 1]
            Gcs[p] = G_cols[:, A_COL + h: A_COL + h + 1]
            Gr = G_rows[A_COL + h: A_COL + h + 1, :]
            Ls[p] = jnp.exp(jnp.where(tri, Gcs[p] - Gr, -jnp.inf))
            kbs[p] = k * betas[p]
            kbfs[p] = k.astype(bf16)
            n_list.append(jnp.where(strict, lax.dot_general(kbs[p].astype(bf16), kbfs[p], _NT,
                                                            preferred_element_type=f32) * Ls[p], 0.0))
    tinvs = dict(zip(P, [t.astype(bf16) for t in _unit_lower_inverse(n_list, c)]))
    egs = {p: jnp.exp(Gcs[p]) for p in P}
    us = {p: jnp.dot(tinvs[p], (vs[p] * betas[p]).astype(bf16), preferred_element_type=f32) for p in P}
    ws = {p: jnp.dot(tinvs[p], (kbs[p] * egs[p]).astype(bf16), preferred_element_type=f32).astype(bf16) for p in P}
    qks = {p: (lax.dot_general(qs[p].astype(bf16), kbfs[p], _NT, preferred_element_type=f32) * Ls[p]).astype(bf16) for p in P}
    qes = {p: (qs[p] * egs[p]).astype(bf16) for p in P}

    Ss = [s_ref[0, h] for h in H]
    for ci in range(nchunk):
        Sbs = [S.astype(bf16) for S in Ss]
        qss = [jnp.dot(qes[ci, h], Sbs[h], preferred_element_type=f32) for h in H]
        vnbs = [(us[ci, h] - jnp.dot(ws[ci, h], Sbs[h], preferred_element_type=f32)).astype(bf16) for h in H]
        os_ = [qss[h] + jnp.dot(qks[ci, h], vnbs[h], preferred_element_type=f32) for h in H]
        new_s = []
        for h in H:
            g_last = Gcs[ci, h][c - 1:c, :]
            kdec = (ks[ci, h] * jnp.exp(g_last - Gcs[ci, h])).astype(bf16)
            new_s.append(Ss[h] * jnp.exp(g_last) + lax.dot_general(kdec, vnbs[h], _TN, preferred_element_type=f32))
        Ss = new_s
        outs = []
        for h in H:
            o = os_[h]
            yo = o * lax.rsqrt(jnp.mean(o * o, axis=-1, keepdims=True) + EPS) * gnorm_ref[...]
            outs.append(yo * _silu(gg[ci * c:(ci + 1) * c, h * HEAD_DIM:(h + 1) * HEAD_DIM]))
        o_ref[0, ci * c:(ci + 1) * c, :] = jnp.concatenate(outs, axis=1)
    for h in H:
        s_ref[0, h] = Ss[h]


def _gdn(gq3, sm3, gg3, conv_buf, conv_w, a_log, dt_bias, gnorm, s0, T):
    B = gq3.shape[0]
    c = math.gcd(T, GDN_CHUNK)
    nchunk = min(T // c, GDN_CHUNKS_PER_STEP)
    tb = c * nchunk
    C = 3 * GDN_WIDTH
    pad = lambda v: jnp.zeros((1, LANES), f32).at[0, A_COL:A_COL + GDN_HEADS].set(v.astype(f32))
    return pl.pallas_call(
        functools.partial(_gdn_body, c=c, nchunk=nchunk),
        grid=(B, T // tb),
        in_specs=[pl.BlockSpec((1, tb, C), lambda b, j: (b, j, 0)),
                  pl.BlockSpec((1, tb, LANES), lambda b, j: (b, j, 0)),
                  pl.BlockSpec((1, tb, GDN_WIDTH), lambda b, j: (b, j, 0)),
                  pl.BlockSpec((1, CONV_WIDTH - 1, C), lambda b, j: (b, 0, 0)),
                  pl.BlockSpec((CONV_WIDTH, C), lambda b, j: (0, 0)),
                  pl.BlockSpec((1, LANES), lambda b, j: (0, 0)),
                  pl.BlockSpec((1, LANES), lambda b, j: (0, 0)),
                  pl.BlockSpec((1, HEAD_DIM), lambda b, j: (0, 0)),
                  pl.BlockSpec((1, GDN_HEADS, HEAD_DIM, HEAD_DIM), lambda b, j: (b, 0, 0, 0))],
        out_specs=[pl.BlockSpec((1, tb, GDN_WIDTH), lambda b, j: (b, j, 0)),
                   pl.BlockSpec((1, GDN_HEADS, HEAD_DIM, HEAD_DIM), lambda b, j: (b, 0, 0, 0))],
        out_shape=[jax.ShapeDtypeStruct((B, T, GDN_WIDTH), f32),
                   jax.ShapeDtypeStruct((B, GDN_HEADS, HEAD_DIM, HEAD_DIM), f32)],
        scratch_shapes=[pltpu.VMEM((8 + tb + 8, C), f32)],
        compiler_params=_cparams(("parallel", "arbitrary")),
        name="gdn",
    )(gq3, sm3, gg3, conv_buf, conv_w, pad(a_log), pad(dt_bias), gnorm, s0)


def _cmp_weights(w1, b1, w2):
    nq = 2 * NSA_KV_HEADS
    kind = np.arange(nq) // NSA_KV_HEADS
    eye = jnp.eye(nq, dtype=f32)
    r = CMP_BLOCK // CMP_STRIDE
    w = w1[kind].reshape(nq, r, CMP_STRIDE, HEAD_DIM, HEAD_DIM)
    w = jnp.transpose(w, (2, 0, 3, 1, 4))
    wc = (w[:, :, :, :, None, :] * eye[None, :, None, None, :, None])
    wc = wc.reshape(CMP_STRIDE, nq * HEAD_DIM, r * nq * HEAD_DIM).astype(bf16)
    b1t = b1[kind].reshape(1, nq * HEAD_DIM)
    w2bd = (w2[kind][:, :, None, :] * eye[:, None, :, None]).reshape(nq * HEAD_DIM, nq * HEAD_DIM).astype(bf16)
    w7 = wc.reshape(CMP_STRIDE // 2, 2, 2, LANES, r, 2, LANES)
    wp = jnp.stack([w7[:, :, pr, :, :, pr, :].reshape(CMP_STRIDE // 2, 2 * LANES, r * LANES) for pr in range(2)])
    return wc, wp, b1t, w2bd


def _cmp_prompt_body(x_ref, wc_ref, b1_ref, w2_ref, o_ref, *, n_seg):
    acc = jnp.zeros((n_seg, 8 * HEAD_DIM), f32)
    for s in range(CMP_STRIDE):
        xs = x_ref[0, :, s * 8 * HEAD_DIM: s * 8 * HEAD_DIM + 4 * HEAD_DIM].astype(bf16)
        acc = acc + jnp.dot(xs, wc_ref[s], preferred_element_type=f32)
    first = acc[:, :4 * HEAD_DIM]
    second = pltpu.roll(acc[:, 4 * HEAD_DIM:], n_seg - 1, 0)
    hid = _silu(b1_ref[...] + first + second)
    out = jnp.dot(hid.astype(bf16), w2_ref[...], preferred_element_type=f32)
    for qi in range(2 * NSA_KV_HEADS):
        o_ref[0, qi] = out[:, qi * HEAD_DIM:(qi + 1) * HEAD_DIM]


def _compress_prompt(kv3, wc, b1t, w2bd):
    B, T, _ = kv3.shape
    n_seg = T // CMP_STRIDE
    xseg = kv3.reshape(B, n_seg, CMP_STRIDE * 8 * HEAD_DIM)
    return pl.pallas_call(
        functools.partial(_cmp_prompt_body, n_seg=n_seg),
        grid=(B,),
        in_specs=[pl.BlockSpec((1, n_seg, CMP_STRIDE * 8 * HEAD_DIM), lambda b: (b, 0, 0)),
                  pl.BlockSpec((CMP_STRIDE, 4 * HEAD_DIM, 8 * HEAD_DIM), lambda b: (0, 0, 0)),
                  pl.BlockSpec((1, 4 * HEAD_DIM), lambda b: (0, 0)),
                  pl.BlockSpec((4 * HEAD_DIM, 4 * HEAD_DIM), lambda b: (0, 0))],
        out_specs=pl.BlockSpec((1, 4, n_seg, HEAD_DIM), lambda b: (b, 0, 0, 0)),
        out_shape=jax.ShapeDtypeStruct((B, 4, n_seg, HEAD_DIM), f32),
        compiler_params=_cparams(("parallel",)),
        name="compress_prompt",
    )(xseg, wc, b1t, w2bd)


def _topk_rows(score, k):
    n = score.shape[0]
    idx = lax.broadcasted_iota(jnp.int32, score.shape, 0)
    taken = jnp.zeros(score.shape, jnp.bool_)
    for _ in range(k):
        work = jnp.where(taken, -jnp.inf, score)
        m = jnp.max(work, axis=0, keepdims=True)
        cand = jnp.where((work == m) & jnp.logical_not(taken), idx, n)
        first = jnp.min(cand, axis=0, keepdims=True)
        taken = taken | (idx == first)
    return jnp.where(taken, 1.0, 0.0)


def _topk_rows_by_rank(score, k):
    n = score.shape[0]
    idx = lax.broadcasted_iota(jnp.int32, score.shape, 0)
    rank = jnp.zeros(score.shape, jnp.int32)
    for i in range(n):
        row = score[i:i + 1, :]
        ahead = (row > score) | ((row == score) & (idx > i))
        rank = rank + ahead.astype(jnp.int32)
    return jnp.where(rank < k, 1.0, 0.0)


def _softmax_rows(s, mask):
    s = jnp.where(mask, s, -jnp.inf)
    m = jnp.max(s, axis=-1, keepdims=True)
    m = jnp.where(m == -jnp.inf, 0.0, m)
    p = jnp.exp(s - m)
    return p / jnp.maximum(jnp.sum(p, axis=-1, keepdims=True), 1e-30)


def _attend_tile(qs, kbs, vbs, biases, carries, tq):
    n = range(len(qs))
    R = qs[0].shape[0]
    tk = kbs[0].shape[0]
    ss = [lax.dot_general(qs[i], kbs[i], _NT, preferred_element_type=f32) for i in n]
    ss = [(ss[i].reshape(NSA_GROUP, tq, tk) + biases[i][None]).reshape(R, tk) for i in n]
    m_news = [jnp.maximum(carries[i][0], jnp.max(ss[i], axis=-1, keepdims=True)) for i in n]
    m_safes = [jnp.where(m == -jnp.inf, 0.0, m) for m in m_news]
    alphas = [jnp.exp(carries[i][0] - m_safes[i]) for i in n]
    ps = [jnp.exp(ss[i] - m_safes[i]) for i in n]
    ls = [alphas[i] * carries[i][1] + jnp.sum(ps[i], axis=-1, keepdims=True) for i in n]
    accs = [alphas[i] * carries[i][2] + jnp.dot(ps[i].astype(bf16), vbs[i], preferred_element_type=f32) for i in n]
    return [(m_news[i], ls[i], accs[i]) for i in n]


def _flash_init(R):
    return (jnp.full((R, 1), -jnp.inf, f32), jnp.zeros((R, 1), f32), jnp.zeros((R, HEAD_DIM), f32))


def _nsa_prompt_body(q_ref, qr_ref, kc_ref, ks_ref, vs_ref, kw_ref, vw_ref, sm_ref, ov_ref, ex_ref,
                     o_ref, wbias_ref, *, tq, tk, tkw, n_wt, T, n_cmp, n_blk):
    i = pl.program_id(1)
    t0 = i * tq
    scale = HEAD_DIM ** -0.5
    G = range(NSA_KV_HEADS)
    R = NSA_GROUP * tq
    gates = jax.nn.sigmoid(sm_ref[0])
    n_pad = kc_ref.shape[2]
    tpos_r = t0 + lax.broadcasted_iota(jnp.int32, (tq, 1), 0)
    tpos_c = t0 + lax.broadcasted_iota(jnp.int32, (1, tq), 1)
    eye = (lax.broadcasted_iota(jnp.int32, (tq, tq), 0) == lax.broadcasted_iota(jnp.int32, (tq, tq), 1)).astype(bf16)
    heads = lambda x, g: [(x[:, (g * NSA_GROUP + r) * HEAD_DIM:(g * NSA_GROUP + r + 1) * HEAD_DIM] * scale).astype(bf16)
                          for r in range(NSA_GROUP)]

    o_cmps, imps = [], []
    nn = lax.broadcasted_iota(jnp.int32, (1, n_pad), 1)
    cmask = (nn * CMP_STRIDE + (CMP_BLOCK - 1) <= tpos_r) & (nn < n_cmp)
    HH = [(g, r) for g in G for r in range(NSA_GROUP)]
    qhs = {g: heads(q_ref[0], g) for g in G}
    kcs = {g: kc_ref[0, g].astype(bf16) for g in G}
    vcs = {g: kc_ref[0, NSA_KV_HEADS + g].astype(bf16) for g in G}
    ss = {h: jnp.where(cmask, lax.dot_general(qhs[h[0]][h[1]], kcs[h[0]], _NT, preferred_element_type=f32), -jnp.inf)
          for h in HH}
    ms = {h: jnp.max(ss[h], axis=-1, keepdims=True) for h in HH}
    es = {h: jnp.exp(ss[h] - jnp.where(ms[h] == -jnp.inf, 0.0, ms[h])) for h in HH}
    ps = {h: es[h] / jnp.maximum(jnp.sum(es[h], axis=-1, keepdims=True), 1e-30) for h in HH}
    ocs = {h: jnp.dot(ps[h].astype(bf16), vcs[h[0]], preferred_element_type=f32) for h in HH}
    for g in G:
        o_cmps.append([ocs[g, r] for r in range(NSA_GROUP)])
        imps.append(ps[g, 0] + ps[g, 1] + ps[g, 2] + ps[g, 3])

    score = jnp.concatenate([lax.dot_general(ov_ref[...], imps[g], _NT, preferred_element_type=f32, precision=_HI)[:n_blk]
                             for g in G], axis=1)
    jj = lax.broadcasted_iota(jnp.int32, score.shape, 0)
    jt = jnp.concatenate([tpos_c] * NSA_KV_HEADS, axis=1) // SEL_BLOCK
    valid = jj <= jt
    forced = valid & ((jj == 0) | (jj == jt) | (jj == jt - 1))
    score = jnp.where(forced, jnp.inf, jnp.where(valid, score, -jnp.inf))
    sel_t = _topk_rows_by_rank(score, min(N_SELECT, n_blk)).astype(bf16)
    if n_blk < LANES:
        sel_t = jnp.concatenate([sel_t, jnp.zeros((LANES - n_blk, NSA_KV_HEADS * tq), bf16)], axis=0)
    sels = [lax.dot_general(eye, sel_t[:, g * tq:(g + 1) * tq], _NT, preferred_element_type=f32).astype(bf16)
            for g in G]

    w0 = pl.multiple_of(jnp.clip(t0 - WINDOW, 0, T - n_wt * tkw), tq)
    for j in range(n_wt):
        dist = tpos_r - (w0 + j * tkw + lax.broadcasted_iota(jnp.int32, (1, tkw), 1))
        wbias_ref[j] = jnp.where((dist >= 0) & (dist < WINDOW), 0.0, -jnp.inf)

    stack = lambda x, g: (jnp.concatenate([x[:, (g * NSA_GROUP + r) * HEAD_DIM:(g * NSA_GROUP + r + 1) * HEAD_DIM]
                                           for r in range(NSA_GROUP)], axis=0) * scale).astype(bf16)
    qrs = [stack(qr_ref[0], g) for g in G]
    kv_tile = lambda ref, k0, n, g: ref[0, pl.ds(k0, n), g * HEAD_DIM:(g + 1) * HEAD_DIM].astype(bf16)

    def sel_step(kt, carries):
        k0 = pl.multiple_of(kt * tk, tk)
        causal = k0 + lax.broadcasted_iota(jnp.int32, (1, tk), 1) <= tpos_r
        biases = [jnp.where((jnp.dot(sels[g], ex_ref[kt], preferred_element_type=f32) > 0.5) & causal, 0.0, -jnp.inf)
                  for g in G]
        return tuple(_attend_tile(qrs, [kv_tile(ks_ref, k0, tk, g) for g in G], [kv_tile(vs_ref, k0, tk, g) for g in G],
                                  biases, carries, tq))

    sel = lax.fori_loop(0, (t0 + tq + tk - 1) // tk, sel_step, tuple(_flash_init(R) for _ in G))

    win = [_flash_init(R) for _ in G]
    for j in range(n_wt):
        k0 = pl.multiple_of(w0 + j * tkw, tq)
        win = _attend_tile(qrs, [kv_tile(kw_ref, k0, tkw, g) for g in G], [kv_tile(vw_ref, k0, tkw, g) for g in G],
                           [wbias_ref[j]] * NSA_KV_HEADS, win, tq)

    for g in G:
        o_sel = sel[g][2] / jnp.maximum(sel[g][1], 1e-30)
        o_win = win[g][2] / jnp.maximum(win[g][1], 1e-30)
        outs = []
        for r in range(NSA_GROUP):
            hcol = GATE_COL + g * NSA_GROUP + r
            rows = slice(r * tq, (r + 1) * tq)
            outs.append(gates[:, hcol:hcol + 1] * o_cmps[g][r]
                        + gates[:, NSA_HEADS + hcol:NSA_HEADS + hcol + 1] * o_sel[rows]
                        + gates[:, 2 * NSA_HEADS + hcol:2 * NSA_HEADS + hcol + 1] * o_win[rows])
        o_ref[0, :, g * NSA_GROUP * HEAD_DIM:(g + 1) * NSA_GROUP * HEAD_DIM] = jnp.concatenate(outs, axis=1)


def _overlap_t(n_blk_pad, n_cmp, n_pad):
    ci = np.arange(n_pad)[None, :] * CMP_STRIDE
    sj = np.arange(n_blk_pad)[:, None] * SEL_BLOCK
    ov = (ci < sj + SEL_BLOCK) & (ci + CMP_BLOCK > sj) & (np.arange(n_pad)[None, :] < n_cmp)
    return jnp.asarray(ov.astype(np.float32))


def _nsa_prompt(q3, qr3, kc, kv3, win3, sm3):
    B, T, _ = q3.shape
    tq = 128
    tk = 512 if T >= 1024 else 256
    tkw = WINDOW + tq if T >= 1024 else 128
    n_wt = -(-min(T, WINDOW + tq) // tkw)
    assert T % tk == 0 and n_wt * tkw <= T
    n_seg = T // CMP_STRIDE
    n_cmp = n_seg - CMP_BLOCK // CMP_STRIDE + 1
    n_blk = -(-T // SEL_BLOCK)
    ov = _overlap_t(LANES, n_cmp, n_seg)
    key_blk = (np.arange(T // tk)[:, None, None] * tk + np.arange(tk)[None, None, :]) // SEL_BLOCK
    ex = jnp.asarray((np.arange(LANES)[None, :, None] == key_blk).astype(np.float32)).astype(bf16)
    qspec = pl.BlockSpec((1, tq, NSA_WIDTH), lambda b, i: (b, i, 0))
    col = lambda j: pl.BlockSpec((1, T, LANES), lambda b, i: (b, 0, j))
    return pl.pallas_call(
        functools.partial(_nsa_prompt_body, tq=tq, tk=tk, tkw=tkw, n_wt=n_wt, T=T, n_cmp=n_cmp, n_blk=n_blk),
        grid=(B, T // tq),
        in_specs=[qspec, qspec,
                  pl.BlockSpec((1, 4, n_seg, HEAD_DIM), lambda b, i: (b, 0, 0, 0)),
                  col(2), col(3), col(0), col(1),
                  pl.BlockSpec((1, tq, LANES), lambda b, i: (b, i, 0)),
                  pl.BlockSpec((LANES, n_seg), lambda b, i: (0, 0)),
                  pl.BlockSpec((T // tk, LANES, tk), lambda b, i: (0, 0, 0))],
        out_specs=qspec,
        out_shape=jax.ShapeDtypeStruct((B, T, NSA_WIDTH), f32),
        scratch_shapes=[pltpu.VMEM((n_wt, tq, tkw), f32)],
        compiler_params=_cparams(("parallel", "arbitrary")),
        name="nsa_prompt",
    )(q3, qr3, kc, kv3, kv3, win3, win3, sm3, ov, ex)


def _out_body(x_ref, ro_ref, no_ref, ng_ref, go_ref, w_ref, g_ref, y_ref):
    no = no_ref[...] * _silu(ng_ref[...])
    z = (jnp.dot(ro_ref[...].astype(bf16), w_ref[0:RET_WIDTH, :], preferred_element_type=f32)
         + jnp.dot(no.astype(bf16), w_ref[RET_WIDTH:RET_WIDTH + NSA_WIDTH, :], preferred_element_type=f32)
         + jnp.dot(go_ref[...].astype(bf16), w_ref[RET_WIDTH + NSA_WIDTH:, :], preferred_element_type=f32))
    y_ref[...] = x_ref[...] + z * lax.rsqrt(jnp.mean(z * z, axis=-1, keepdims=True) + EPS) * g_ref[...]


def _out(x2d, ro, no, ng, go, w_out_b, gain, tm):
    M = x2d.shape[0]
    row = lambda w: pl.BlockSpec((tm, w), lambda i: (i, 0))
    return pl.pallas_call(
        _out_body,
        grid=(M // tm,),
        in_specs=[row(D_MODEL), row(RET_WIDTH), row(NSA_WIDTH), row(NSA_WIDTH), row(GDN_WIDTH),
                  pl.BlockSpec((D_MODEL, D_MODEL), lambda i: (0, 0)),
                  pl.BlockSpec((1, D_MODEL), lambda i: (0, 0))],
        out_specs=row(D_MODEL),
        out_shape=jax.ShapeDtypeStruct((M, D_MODEL), f32),
        compiler_params=_cparams(("parallel",)),
        name="out",
    )(x2d, ro, no, ng, go, w_out_b, gain)


PAGES_PER_STEP = 16


def _page_specs(l, pg, half):
    return [pl.BlockSpec((None, None, 4 * HEAD_DIM, PAGE_SIZE),
                         lambda b, j, pt, i=i: (l, pt[b, j * pg + i], half, 0)) for i in range(pg)]


def _cmp_sample_body(pt_ref, *refs, pg):
    page_refs, perm_ref, wp_ref, o_ref, xt_ref = refs[:pg], refs[pg], refs[pg + 1], refs[pg + 2], refs[pg + 3]
    seg = PAGE_SIZE // CMP_STRIDE
    for i in range(pg):
        xp = lax.dot_general(perm_ref[...], page_refs[i][...].astype(bf16), _NT, preferred_element_type=f32)
        for s in range(CMP_STRIDE):
            xt_ref[s, i * seg:(i + 1) * seg, :] = xp[s * seg:(s + 1) * seg, :].astype(bf16)
    n_row = pg * seg
    accs = [jnp.zeros((n_row, 4 * HEAD_DIM), f32) for _ in range(2)]
    for s2 in range(CMP_STRIDE // 2):
        x0 = xt_ref[2 * s2]
        x1 = xt_ref[2 * s2 + 1]
        for pr in range(2):
            xs = jnp.concatenate([x0[:, pr * LANES:(pr + 1) * LANES], x1[:, pr * LANES:(pr + 1) * LANES]], axis=1)
            accs[pr] = accs[pr] + jnp.dot(xs, wp_ref[pr, s2], preferred_element_type=f32)
    o_ref[0] = jnp.concatenate([accs[0][:, :LANES], accs[1][:, :LANES], accs[0][:, LANES:], accs[1][:, LANES:]], axis=1)


def _cmp_sample(cache_t, l, page_table, wp):
    B, n_pages = page_table.shape
    pg = PAGES_PER_STEP
    n_row = pg * PAGE_SIZE // CMP_STRIDE
    seg = PAGE_SIZE // CMP_STRIDE
    off = np.arange(PAGE_SIZE)
    perm = jnp.asarray(((off[None, :] % CMP_STRIDE) * seg + off[None, :] // CMP_STRIDE == off[:, None]).astype(np.float32)).astype(bf16)
    return pl.pallas_call(
        functools.partial(_cmp_sample_body, pg=pg),
        grid_spec=pltpu.PrefetchScalarGridSpec(
            num_scalar_prefetch=1,
            grid=(B, n_pages // pg),
            in_specs=_page_specs(l, pg, 0) + [pl.BlockSpec((PAGE_SIZE, PAGE_SIZE), lambda b, j, pt: (0, 0)),
                                              pl.BlockSpec((2, CMP_STRIDE // 2, 4 * HEAD_DIM, 4 * HEAD_DIM),
                                                           lambda b, j, pt: (0, 0, 0, 0))],
            out_specs=pl.BlockSpec((1, n_row, 8 * HEAD_DIM), lambda b, j, pt: (b, j, 0)),
            scratch_shapes=[pltpu.VMEM((CMP_STRIDE, n_row, 4 * HEAD_DIM), bf16)]),
        out_shape=jax.ShapeDtypeStruct((B, n_pages * PAGE_SIZE // CMP_STRIDE, 8 * HEAD_DIM), f32),
        compiler_params=_cparams(("parallel", "arbitrary")),
        name="compress_sample",
    )(page_table, *([cache_t] * pg), perm, wp)


def _stack_heads(x, g):
    return jnp.concatenate([x[:, (g * NSA_GROUP + r) * HEAD_DIM:(g * NSA_GROUP + r + 1) * HEAD_DIM]
                            for r in range(NSA_GROUP)], axis=0)


def _nsa_sample_mid_body(p_ref, b1_ref, w2_ref, q_ref, qr_ref, win_ref, st_ref, ov_ref,
                         ocmp_ref, owin_ref, selt_ref, *, Ts, n_cmp, past_len, wb):
    P = p_ref[0]
    n_seg = P.shape[0]
    hid = _silu(b1_ref[...] + P[:, :4 * HEAD_DIM] + pltpu.roll(P[:, 4 * HEAD_DIM:], n_seg - 1, 0))
    kcv = jnp.dot(hid.astype(bf16), w2_ref[...], preferred_element_type=f32)
    scale = HEAD_DIM ** -0.5
    R = NSA_GROUP * Ts
    trow = lax.broadcasted_iota(jnp.int32, (R, 1), 0) % Ts
    imps = []
    for g in range(NSA_KV_HEADS):
        qs = (_stack_heads(q_ref[0], g) * scale).astype(bf16)
        kc = kcv[:, g * HEAD_DIM:(g + 1) * HEAD_DIM].astype(bf16)
        vc = kcv[:, (NSA_KV_HEADS + g) * HEAD_DIM:(NSA_KV_HEADS + g + 1) * HEAD_DIM].astype(bf16)
        s = lax.dot_general(qs, kc, _NT, preferred_element_type=f32)
        nn = lax.broadcasted_iota(jnp.int32, (1, n_seg), 1)
        cmask = (nn * CMP_STRIDE + (CMP_BLOCK - 1) <= past_len + trow) & (nn < n_cmp)
        p = _softmax_rows(s, cmask)
        ocmp_ref[0, g] = jnp.dot(p.astype(bf16), vc, preferred_element_type=f32)
        imps.append(jnp.sum(p.reshape(NSA_GROUP, Ts, n_seg), axis=0))
        qrs = (_stack_heads(qr_ref[0], g) * scale).astype(bf16)
        kwt = st_ref[g * HEAD_DIM:(g + 1) * HEAD_DIM, :].astype(bf16)
        vwt = st_ref[(NSA_KV_HEADS + g) * HEAD_DIM:(NSA_KV_HEADS + g + 1) * HEAD_DIM, :].astype(bf16)
        wn = win_ref[0]
        knew = wn[:, g * HEAD_DIM:(g + 1) * HEAD_DIM].astype(bf16)
        vnew = wn[:, (NSA_KV_HEADS + g) * HEAD_DIM:(NSA_KV_HEADS + g + 1) * HEAD_DIM].astype(bf16)
        s_b = jnp.dot(qrs, kwt, preferred_element_type=f32)
        s_n = lax.dot_general(qrs, knew, _NT, preferred_element_type=f32)
        jb = lax.broadcasted_iota(jnp.int32, (1, wb), 1)
        dist_b = trow + wb - jb
        s_b = jnp.where((dist_b >= 0) & (dist_b < WINDOW) & (past_len - wb + jb >= 0), s_b, -jnp.inf)
        dist_n = trow - lax.broadcasted_iota(jnp.int32, (1, Ts), 1)
        s_n = jnp.where((dist_n >= 0) & (dist_n < WINDOW), s_n, -jnp.inf)
        m = jnp.maximum(jnp.max(s_b, axis=-1, keepdims=True), jnp.max(s_n, axis=-1, keepdims=True))
        m = jnp.where(m == -jnp.inf, 0.0, m)
        pb = jnp.exp(s_b - m)
        pn = jnp.exp(s_n - m)
        den = jnp.sum(pb, axis=-1, keepdims=True) + jnp.sum(pn, axis=-1, keepdims=True)
        o = (lax.dot_general(pb.astype(bf16), vwt, _NT, preferred_element_type=f32)
             + jnp.dot(pn.astype(bf16), vnew, preferred_element_type=f32))
        owin_ref[0, g] = o / jnp.maximum(den, 1e-30)

    imp2 = jnp.concatenate(imps, axis=0)
    score = lax.dot_general(ov_ref[...], imp2, _NT, preferred_element_type=f32, precision=_HI)
    jj = lax.broadcasted_iota(jnp.int32, score.shape, 0)
    jt = (past_len + lax.broadcasted_iota(jnp.int32, score.shape, 1) % Ts) // SEL_BLOCK
    valid = jj <= jt
    forced = valid & ((jj == 0) | (jj == jt) | (jj == jt - 1))
    score = jnp.where(forced, jnp.inf, jnp.where(valid, score, -jnp.inf))
    selt_ref[0] = _topk_rows(score, N_SELECT)


def _nsa_sample_mid(P, b1t, w2bd, q3, qr3, win3, st_t, l, past_len):
    B, Ts, _ = q3.shape
    n_seg = P.shape[1]
    n_cmp = n_seg - CMP_BLOCK // CMP_STRIDE + 1
    n_blk = -(-(past_len + Ts) // SEL_BLOCK)
    nb_pad = -(-n_blk // 8) * 8
    wb = st_t.shape[-1]
    ov = _overlap_t(nb_pad, n_cmp, n_seg)
    R = NSA_GROUP * Ts
    tok = lambda w: pl.BlockSpec((1, Ts, w), lambda b: (b, 0, 0))
    return pl.pallas_call(
        functools.partial(_nsa_sample_mid_body, Ts=Ts, n_cmp=n_cmp, past_len=past_len, wb=wb),
        grid=(B,),
        in_specs=[pl.BlockSpec((1, n_seg, 8 * HEAD_DIM), lambda b: (b, 0, 0)),
                  pl.BlockSpec((1, 4 * HEAD_DIM), lambda b: (0, 0)),
                  pl.BlockSpec((4 * HEAD_DIM, 4 * HEAD_DIM), lambda b: (0, 0)),
                  tok(NSA_WIDTH), tok(NSA_WIDTH), tok(4 * HEAD_DIM),
                  pl.BlockSpec((None, None, 4 * HEAD_DIM, wb), lambda b: (l, b, 0, 0)),
                  pl.BlockSpec((nb_pad, n_seg), lambda b: (0, 0))],
        out_specs=[pl.BlockSpec((1, NSA_KV_HEADS, R, HEAD_DIM), lambda b: (b, 0, 0, 0)),
                   pl.BlockSpec((1, NSA_KV_HEADS, R, HEAD_DIM), lambda b: (b, 0, 0, 0)),
                   pl.BlockSpec((1, nb_pad, NSA_KV_HEADS * Ts), lambda b: (b, 0, 0))],
        out_shape=[jax.ShapeDtypeStruct((B, NSA_KV_HEADS, R, HEAD_DIM), f32),
                   jax.ShapeDtypeStruct((B, NSA_KV_HEADS, R, HEAD_DIM), f32),
                   jax.ShapeDtypeStruct((B, nb_pad, NSA_KV_HEADS * Ts), f32)],
        compiler_params=_cparams(("parallel",)),
        name="nsa_sample_mid",
    )(P, b1t, w2bd, q3, qr3, win3, st_t, ov)


def _nsa_sample_sel_body(pt_ref, *refs, pg, Ts, n_steps):
    page_refs = refs[:pg]
    qr_ref, kvn_ref, selt_ref, ex_ref, ocmp_ref, owin_ref, sm_ref, o_ref, m_ref, l_ref, acc_ref = refs[pg:]
    j = pl.program_id(1)
    scale = HEAD_DIM ** -0.5
    R = NSA_GROUP * Ts
    GT = NSA_KV_HEADS * Ts
    nb = 2 * pg
    trow = lax.broadcasted_iota(jnp.int32, (R, 1), 0) % Ts
    eye = (lax.broadcasted_iota(jnp.int32, (GT, GT), 0) == lax.broadcasted_iota(jnp.int32, (GT, GT), 1)).astype(bf16)
    sel_j = selt_ref[0, pl.ds(pl.multiple_of(j * nb, nb), nb), :].astype(bf16)
    sel_j = jnp.concatenate([sel_j, jnp.zeros((LANES - nb, GT), bf16)], axis=0)
    sel_rows = lax.dot_general(eye, sel_j, _NT, preferred_element_type=f32)
    full = jnp.dot(sel_rows.astype(bf16), ex_ref[...], preferred_element_type=f32)

    G = range(NSA_KV_HEADS)
    qrs = [(_stack_heads(qr_ref[0], g) * scale).astype(bf16) for g in G]

    @pl.when(j == 0)
    def _():
        kvn = kvn_ref[0]
        for g in G:
            knew = kvn[:, (2 * NSA_KV_HEADS + g) * HEAD_DIM:(2 * NSA_KV_HEADS + g + 1) * HEAD_DIM].astype(bf16)
            vnew = kvn[:, (3 * NSA_KV_HEADS + g) * HEAD_DIM:(3 * NSA_KV_HEADS + g + 1) * HEAD_DIM].astype(bf16)
            s_n = lax.dot_general(qrs[g], knew, _NT, preferred_element_type=f32)
            s_n = jnp.where(lax.broadcasted_iota(jnp.int32, (1, Ts), 1) <= trow, s_n, -jnp.inf)
            m0 = jnp.max(s_n, axis=-1, keepdims=True)
            p0 = jnp.exp(s_n - m0)
            m_ref[g] = m0
            l_ref[g] = jnp.sum(p0, axis=-1, keepdims=True)
            acc_ref[g] = jnp.dot(p0.astype(bf16), vnew, preferred_element_type=f32)

    kts = [jnp.concatenate([page_refs[i][g * HEAD_DIM:(g + 1) * HEAD_DIM, :] for i in range(pg)], axis=1).astype(bf16)
           for g in G]
    vts = [jnp.concatenate([page_refs[i][(NSA_KV_HEADS + g) * HEAD_DIM:(NSA_KV_HEADS + g + 1) * HEAD_DIM, :]
                            for i in range(pg)], axis=1).astype(bf16) for g in G]
    ss = [jnp.dot(qrs[g], kts[g], preferred_element_type=f32) for g in G]
    ss = [jnp.where(jnp.concatenate([full[g * Ts:(g + 1) * Ts]] * NSA_GROUP, axis=0) > 0.5, ss[g], -jnp.inf) for g in G]
    m_olds = [m_ref[g] for g in G]
    m_news = [jnp.maximum(m_olds[g], jnp.max(ss[g], axis=-1, keepdims=True)) for g in G]
    alphas = [jnp.exp(m_olds[g] - m_news[g]) for g in G]
    ps = [jnp.exp(ss[g] - m_news[g]) for g in G]
    pvs = [lax.dot_general(ps[g].astype(bf16), vts[g], _NT, preferred_element_type=f32) for g in G]
    for g in G:
        l_ref[g] = alphas[g] * l_ref[g] + jnp.sum(ps[g], axis=-1, keepdims=True)
        acc_ref[g] = alphas[g] * acc_ref[g] + pvs[g]
        m_ref[g] = m_news[g]


    @pl.when(j == n_steps - 1)
    def _():
        gates = jax.nn.sigmoid(sm_ref[0])
        outs = []
        for g in range(NSA_KV_HEADS):
            o_sel = acc_ref[g] / jnp.maximum(l_ref[g], 1e-30)
            o_cmp = ocmp_ref[0, g]
            o_win = owin_ref[0, g]
            for r in range(NSA_GROUP):
                hcol = GATE_COL + g * NSA_GROUP + r
                rows = slice(r * Ts, (r + 1) * Ts)
                outs.append(gates[:, hcol:hcol + 1] * o_cmp[rows]
                            + gates[:, NSA_HEADS + hcol:NSA_HEADS + hcol + 1] * o_sel[rows]
                            + gates[:, 2 * NSA_HEADS + hcol:2 * NSA_HEADS + hcol + 1] * o_win[rows])
        o_ref[0] = jnp.concatenate(outs, axis=1)


def _nsa_sample_sel(cache_t, l, page_table, qr3, kv3, selt, ocmp, owin, sm3):
    B, n_pages = page_table.shape
    Ts = qr3.shape[1]
    pg = PAGES_PER_STEP
    n_steps = n_pages // pg
    R = NSA_GROUP * Ts
    ex = jnp.asarray((np.arange(LANES)[:, None] == (np.arange(pg * PAGE_SIZE)[None, :] // SEL_BLOCK)).astype(np.float32)).astype(bf16)
    tok = lambda w: pl.BlockSpec((1, Ts, w), lambda b, j, pt: (b, 0, 0))
    stk = pl.BlockSpec((1, NSA_KV_HEADS, R, HEAD_DIM), lambda b, j, pt: (b, 0, 0, 0))
    return pl.pallas_call(
        functools.partial(_nsa_sample_sel_body, pg=pg, Ts=Ts, n_steps=n_steps),
        grid_spec=pltpu.PrefetchScalarGridSpec(
            num_scalar_prefetch=1,
            grid=(B, n_steps),
            in_specs=_page_specs(l, pg, 1) + [
                tok(NSA_WIDTH), tok(8 * HEAD_DIM),
                pl.BlockSpec((1, selt.shape[1], selt.shape[2]), lambda b, j, pt: (b, 0, 0)),
                pl.BlockSpec((LANES, pg * PAGE_SIZE), lambda b, j, pt: (0, 0)),
                stk, stk, tok(LANES)],
            out_specs=tok(NSA_WIDTH),
            scratch_shapes=[pltpu.VMEM((NSA_KV_HEADS, R, 1), f32), pltpu.VMEM((NSA_KV_HEADS, R, 1), f32),
                            pltpu.VMEM((NSA_KV_HEADS, R, HEAD_DIM), f32)]),
        out_shape=jax.ShapeDtypeStruct((B, Ts, NSA_WIDTH), f32),
        compiler_params=_cparams(("parallel", "arbitrary")),
        name="nsa_sample_sel",
    )(page_table, *([cache_t] * pg), qr3, kv3, selt, ex, ocmp, owin, sm3)


def _nsa_sample(q3, qr3, kv3, win3, sm3, cache_t, st_t, l, page_table, wp, b1t, w2bd):
    Ts = q3.shape[1]
    n_pages = page_table.shape[1]
    past_len = n_pages * PAGE_SIZE
    assert (past_len + Ts) // CMP_STRIDE * CMP_STRIDE <= past_len
    assert past_len % SEL_BLOCK == 0 and Ts <= SEL_BLOCK and n_pages % PAGES_PER_STEP == 0
    P = _cmp_sample(cache_t, l, page_table, wp)
    ocmp, owin, selt = _nsa_sample_mid(P, b1t, w2bd, q3, qr3, win3, st_t, l, past_len)
    return _nsa_sample_sel(cache_t, l, page_table, qr3, kv3, selt, ocmp, owin, sm3)


def _rope_tables(pos):
    half = HEAD_DIM // 2
    inv = ROPE_THETA ** (-jnp.arange(half, dtype=f32) / half)
    ang = pos.astype(f32)[:, None] * inv[None, :]
    cos, sin = jnp.cos(ang), jnp.sin(ang)
    return jnp.concatenate([cos, cos, cos, cos], axis=1), jnp.concatenate([-sin, sin, -sin, sin], axis=1)


def _prep_w_in(w):
    cols = [w[:, _OFF[n][0]:_OFF[n][1]] for n in _PAD_ORDER]
    used = sum(c.shape[1] for c in cols)
    cols.append(jnp.zeros((w.shape[0], PROJ_PAD - used), w.dtype))
    return jnp.concatenate(cols, axis=1).astype(bf16)


def _layer(x, pos, prm, l, nsa_fn, ret_state, gdn_state, conv_buf, tm):
    B, T, _ = x.shape
    M = B * T
    tm = min(tm, M)
    cosf, sinf = _rope_tables(pos)
    if tm > T:
        cosf, sinf = jnp.tile(cosf, (tm // T, 1)), jnp.tile(sinf, (tm // T, 1))
    x2d = x.reshape(M, D_MODEL)
    ret, q, qr, kv, win, ng, gq, gg, sm = _proj(x2d, prm['norm_pre'][l][None], prm['w_in_p'][l], cosf, sinf, tm)
    r3 = lambda a: a.reshape(B, T, a.shape[-1])
    ro, ret_new = _retention(r3(ret), ret_state, prm['ret_gn'][l][None], T)
    go, gdn_new = _gdn(r3(gq), r3(sm), r3(gg), conv_buf, prm['gdn_conv'][l], prm['gdn_a_log'][l],
                       prm['gdn_dt_bias'][l], prm['gdn_norm'][l][None], gdn_state, T)
    assert T >= CONV_WIDTH - 1
    conv_new = r3(gq)[:, T - (CONV_WIDTH - 1):]
    no, win_new = nsa_fn(r3(q), r3(qr), r3(kv), r3(win), r3(sm))
    y = _out(x2d, ro.reshape(M, -1), no.reshape(M, -1), ng, go.reshape(M, -1), prm['w_out_b'][l],
             prm['norm_post'][l][None], tm)
    rows = kv.reshape(B, T, 4, NSA_KV_HEADS, HEAD_DIM)
    return y.reshape(B, T, D_MODEL), (rows, win_new, ret_new, gdn_new, conv_new)


def kernel(x_prompt, x_sample, cache_nsa_kv, page_table, state_nsa_win, state_ret, state_gdn, state_gdn_conv,
           w_in, w_out, norm_pre, norm_post, ret_gn, gdn_norm, gdn_conv, gdn_a_log, gdn_dt_bias,
           cmp_w1, cmp_b1, cmp_w2):
    Bp, T, _ = x_prompt.shape
    Bs, Ts, _ = x_sample.shape
    depth = w_in.shape[0]
    past_len = page_table.shape[1] * PAGE_SIZE
    pos_p = jnp.arange(T, dtype=jnp.int32)
    pos_s = past_len + jnp.arange(Ts, dtype=jnp.int32)
    prm = {'w_in_p': [_prep_w_in(w_in[l]) for l in range(depth)],
           'w_out_b': [w_out[l].astype(bf16) for l in range(depth)],
           'norm_pre': norm_pre, 'norm_post': norm_post, 'ret_gn': ret_gn, 'gdn_norm': gdn_norm,
           'gdn_conv': gdn_conv, 'gdn_a_log': gdn_a_log, 'gdn_dt_bias': gdn_dt_bias}
    yp, ys = x_prompt, x_sample
    cache_t = jnp.transpose(cache_nsa_kv, (0, 1, 3, 4, 5, 2)).reshape(
        depth, cache_nsa_kv.shape[1], 8 * HEAD_DIM, PAGE_SIZE)
    st_t = jnp.transpose(state_nsa_win, (0, 1, 3, 4, 5, 2)).reshape(
        depth, Bs, 4 * HEAD_DIM, state_nsa_win.shape[2])
    st_p, st_s = [], []
    for l in range(depth):
        wc, wp, b1t, w2bd = _cmp_weights(cmp_w1[l], cmp_b1[l], cmp_w2[l])

        def nsa_prompt(q3, qr3, kv3, win3, sm3):
            kc = _compress_prompt(kv3, wc, b1t, w2bd)
            no = _nsa_prompt(q3, qr3, kc, kv3, win3, sm3)
            win_rows = win3.reshape(Bp, T, 2, NSA_KV_HEADS, HEAD_DIM)
            return no, win_rows[:, T - min(WINDOW, T):]

        def nsa_sample(q3, qr3, kv3, win3, sm3):
            no = _nsa_sample(q3, qr3, kv3, win3, sm3, cache_t, st_t, l, page_table, wp, b1t, w2bd)
            win_rows = win3.reshape(Bs, Ts, 2, NSA_KV_HEADS, HEAD_DIM)
            keys = jnp.concatenate([state_nsa_win[l], win_rows], axis=1)
            return no, keys[:, -state_nsa_win.shape[2]:]

        yp, sp = _layer(yp, pos_p, prm, l, nsa_prompt,
                        jnp.zeros((Bp, RET_HEADS, HEAD_DIM, HEAD_DIM), f32),
                        jnp.zeros((Bp, GDN_HEADS, HEAD_DIM, HEAD_DIM), f32),
                        jnp.zeros((Bp, CONV_WIDTH - 1, 3 * GDN_WIDTH), f32), PROMPT_ROW_TILE)
        ys, ss = _layer(ys, pos_s, prm, l, nsa_sample, state_ret[l], state_gdn[l], state_gdn_conv[l], Bs * Ts)
        st_p.append(sp)
        st_s.append(ss)
    stk = lambda sts, i: jnp.stack([s[i] for s in sts])
    return (yp, ys, stk(st_p, 0), stk(st_s, 0), stk(st_p, 1), stk(st_s, 1), stk(st_p, 2), stk(st_s, 2),
            stk(st_p, 3), stk(st_s, 3), stk(st_p, 4), stk(st_s, 4))
```

```python
import functools
import math

import numpy as np
import jax
import jax.numpy as jnp
from jax import lax
from jax.experimental import pallas as pl
from jax.experimental.pallas import tpu as pltpu

f32 = jnp.float32
bf16 = jnp.bfloat16

D_MODEL = 1024
HEAD_DIM = 64
RET_WIDTH = 256
NSA_WIDTH = 512
GDN_WIDTH = 256
RET_HEADS = 4
NSA_HEADS = 8
NSA_KV_HEADS = 2
NSA_GROUP = 4
GDN_HEADS = 4
CMP_BLOCK = 32
CMP_STRIDE = 16
SEL_BLOCK = 64
N_SELECT = 16
WINDOW = 512
RET_CHUNK = 64
GDN_CHUNK = 64
GDN_CHUNKS_PER_STEP = 4
CONV_WIDTH = 4
PAGE_SIZE = 128
ROPE_THETA = 10000.0
EPS = 1e-6

LANES = 128
VMEM_LIMIT = 56 * 1024 * 1024
PROMPT_ROW_TILE = 512

_SPLITS = (('ret', 4 * RET_WIDTH), ('nsa_q', NSA_WIDTH), ('nsa_kv', 6 * NSA_KV_HEADS * HEAD_DIM),
           ('nsa_gate', 3 * NSA_HEADS), ('nsa_g', NSA_WIDTH), ('gdn_qkv', 3 * GDN_WIDTH),
           ('gdn_ba', 2 * GDN_HEADS), ('gdn_g', GDN_WIDTH))
_OFF = {}
_o = 0
for _n, _w in _SPLITS:
    _OFF[_n] = (_o, _o + _w)
    _o += _w
PROJ_WIDTH = _o
_PAD_ORDER = ('ret', 'nsa_q', 'nsa_kv', 'nsa_g', 'gdn_qkv', 'gdn_g', 'nsa_gate', 'gdn_ba')
_POFF = {}
_o = 0
for _n in _PAD_ORDER:
    _w = _OFF[_n][1] - _OFF[_n][0]
    _POFF[_n] = _o
    _o += _w
SMALL_OFF = _POFF['nsa_gate']
PROJ_PAD = SMALL_OFF + LANES
GATE_COL = 0
BETA_COL = 3 * NSA_HEADS
A_COL = BETA_COL + GDN_HEADS

_NT = (((1,), (1,)), ((), ()))
_TN = (((0,), (0,)), ((), ()))
_HI = lax.Precision.HIGHEST


def _silu(x):
    return x * jax.nn.sigmoid(x)


def _cparams(sem):
    return pltpu.CompilerParams(dimension_semantics=sem, vmem_limit_bytes=VMEM_LIMIT)


def _proj_body(x_ref, g_ref, w_ref, cos_ref, sin_ref,
               ret_ref, q_ref, qr_ref, kv_ref, win_ref, ng_ref, gq_ref, gg_ref, sm_ref):
    x = x_ref[...]
    h = x * lax.rsqrt(jnp.mean(x * x, axis=-1, keepdims=True) + EPS) * g_ref[...]
    hb = h.astype(bf16)
    cos = cos_ref[...]
    sin = sin_ref[...]
    lane = lax.broadcasted_iota(jnp.int32, cos.shape, 1)
    low = (lane % HEAD_DIM) < HEAD_DIM // 2

    def mm(c0, width):
        return jnp.dot(hb, w_ref[:, c0:c0 + width], preferred_element_type=f32)

    def rope(v):
        sw = jnp.where(low, pltpu.roll(v, LANES - HEAD_DIM // 2, 1), pltpu.roll(v, HEAD_DIM // 2, 1))
        return v * cos + sw * sin

    r = mm(_POFF['ret'], 4 * RET_WIDTH)
    for j in range(2):
        ret_ref[:, j * LANES:(j + 1) * LANES] = rope(r[:, j * LANES:(j + 1) * LANES])
    for j in range(2, 4):
        ret_ref[:, j * LANES:(j + 1) * LANES] = rope(r[:, j * LANES:(j + 1) * LANES]) * (HEAD_DIM ** -0.5)
    ret_ref[:, 2 * RET_WIDTH:] = r[:, 2 * RET_WIDTH:]

    q = mm(_POFF['nsa_q'], NSA_WIDTH)
    q_ref[...] = q
    for j in range(NSA_WIDTH // LANES):
        qr_ref[:, j * LANES:(j + 1) * LANES] = rope(q[:, j * LANES:(j + 1) * LANES])

    kv = mm(_POFF['nsa_kv'], 6 * LANES)
    kv_ref[:, 0:2 * LANES] = kv[:, 0:2 * LANES]
    kv_ref[:, 2 * LANES:3 * LANES] = rope(kv[:, 2 * LANES:3 * LANES])
    kv_ref[:, 3 * LANES:4 * LANES] = kv[:, 3 * LANES:4 * LANES]
    win_ref[:, 0:LANES] = rope(kv[:, 4 * LANES:5 * LANES])
    win_ref[:, LANES:2 * LANES] = kv[:, 5 * LANES:6 * LANES]

    ng_ref[...] = mm(_POFF['nsa_g'], NSA_WIDTH)
    gq_ref[...] = mm(_POFF['gdn_qkv'], 3 * GDN_WIDTH)
    gg_ref[...] = mm(_POFF['gdn_g'], GDN_WIDTH)
    sm_ref[...] = mm(SMALL_OFF, LANES)


def _proj(x2d, gain, wpad, cosf, sinf, tm):
    M = x2d.shape[0]
    n_pos = cosf.shape[0] // tm
    widths = (4 * RET_WIDTH, NSA_WIDTH, NSA_WIDTH, 4 * LANES, 2 * LANES, NSA_WIDTH, 3 * GDN_WIDTH, GDN_WIDTH, LANES)
    row = lambda w: pl.BlockSpec((tm, w), lambda i: (i, 0))
    return pl.pallas_call(
        _proj_body,
        grid=(M // tm,),
        in_specs=[row(D_MODEL),
                  pl.BlockSpec((1, D_MODEL), lambda i: (0, 0)),
                  pl.BlockSpec((D_MODEL, PROJ_PAD), lambda i: (0, 0)),
                  pl.BlockSpec((tm, LANES), lambda i: (i % n_pos, 0)),
                  pl.BlockSpec((tm, LANES), lambda i: (i % n_pos, 0))],
        out_specs=[row(w) for w in widths],
        out_shape=[jax.ShapeDtypeStruct((M, w), f32) for w in widths],
        compiler_params=_cparams(("parallel",)),
        name="proj",
    )(x2d, gain, wpad, cosf, sinf)


def _ret_body(ret_ref, dm_ref, qd_ref, kd_ref, gn_ref, s0_ref, o_ref, s_ref, *, c, nchunk, sdec):
    @pl.when(pl.program_id(1) == 0)
    def _():
        s_ref[...] = s0_ref[...]

    H = range(RET_HEADS)
    hsl = lambda x, h: x[:, h * HEAD_DIM:(h + 1) * HEAD_DIM]
    Ss = [s_ref[0, h] for h in H]
    for ci in range(nchunk):
        blk = ret_ref[0, ci * c:(ci + 1) * c, :]
        q_all = blk[:, 0:RET_WIDTH].astype(bf16)
        k_all = blk[:, RET_WIDTH:2 * RET_WIDTH]
        kb_all = k_all.astype(bf16)
        kd_all = (k_all * kd_ref[...]).astype(bf16)
        v_all = blk[:, 2 * RET_WIDTH:3 * RET_WIDTH].astype(bf16)
        gate = _silu(blk[:, 3 * RET_WIDTH:])
        atts = [(lax.dot_general(hsl(q_all, h), hsl(kb_all, h), _NT, preferred_element_type=f32) * dm_ref[h]).astype(bf16)
                for h in H]
        qss = [jnp.dot(hsl(q_all, h), Ss[h].astype(bf16), preferred_element_type=f32) for h in H]
        attvs = [jnp.dot(atts[h], hsl(v_all, h), preferred_element_type=f32) for h in H]
        kvs = [lax.dot_general(hsl(kd_all, h), hsl(v_all, h), _TN, preferred_element_type=f32) for h in H]
        Ss = [Ss[h] * sdec[h] + kvs[h] for h in H]
        qd = qd_ref[...]
        outs = []
        for h in H:
            o = attvs[h] + qss[h] * hsl(qd, h)
            mu = jnp.mean(o, axis=-1, keepdims=True)
            d = o - mu
            var = jnp.mean(d * d, axis=-1, keepdims=True)
            outs.append(d * lax.rsqrt(var + EPS))
        o_ref[0, ci * c:(ci + 1) * c, :] = jnp.concatenate(outs, axis=1) * gn_ref[...] * gate
    for h in H:
        s_ref[0, h] = Ss[h]


def _retention(ret3, s0, gn, T):
    B = ret3.shape[0]
    c = math.gcd(T, RET_CHUNK)
    nchunk = min(T // c, 4)
    tb = c * nchunk
    hh = jnp.arange(RET_HEADS, dtype=f32)
    lg = jnp.log1p(-jnp.exp2(-5.0 - hh))
    ii = jnp.arange(c, dtype=f32)
    rel = ii[:, None] - ii[None, :]
    dm = jnp.exp(jnp.where(rel[None] >= 0, rel[None] * lg[:, None, None], -jnp.inf))
    qd = jnp.repeat(jnp.exp((ii + 1.0)[:, None] * lg[None, :]), HEAD_DIM, axis=1)
    kd = jnp.repeat(jnp.exp((c - 1.0 - ii)[:, None] * lg[None, :]), HEAD_DIM, axis=1)
    sdec = tuple(float((1.0 - 2.0 ** (-5.0 - h)) ** c) for h in range(RET_HEADS))
    return pl.pallas_call(
        functools.partial(_ret_body, c=c, nchunk=nchunk, sdec=sdec),
        grid=(B, T // tb),
        in_specs=[pl.BlockSpec((1, tb, 4 * RET_WIDTH), lambda b, j: (b, j, 0)),
                  pl.BlockSpec((RET_HEADS, c, c), lambda b, j: (0, 0, 0)),
                  pl.BlockSpec((c, RET_WIDTH), lambda b, j: (0, 0)),
                  pl.BlockSpec((c, RET_WIDTH), lambda b, j: (0, 0)),
                  pl.BlockSpec((1, RET_WIDTH), lambda b, j: (0, 0)),
                  pl.BlockSpec((1, RET_HEADS, HEAD_DIM, HEAD_DIM), lambda b, j: (b, 0, 0, 0))],
        out_specs=[pl.BlockSpec((1, tb, RET_WIDTH), lambda b, j: (b, j, 0)),
                   pl.BlockSpec((1, RET_HEADS, HEAD_DIM, HEAD_DIM), lambda b, j: (b, 0, 0, 0))],
        out_shape=[jax.ShapeDtypeStruct((B, T, RET_WIDTH), f32),
                   jax.ShapeDtypeStruct((B, RET_HEADS, HEAD_DIM, HEAD_DIM), f32)],
        compiler_params=_cparams(("parallel", "arbitrary")),
        name="retention",
    )(ret3, dm, qd, kd, gn, s0)


def _softplus(x):
    return jnp.maximum(x, 0.0) + jnp.log1p(jnp.exp(-jnp.abs(x)))


def _split_bf16(x):
    hi = x.astype(bf16)
    return hi, (x - hi.astype(f32)).astype(bf16)


def _dot_split(a, b):
    d = lambda x, y: jnp.dot(x, y, preferred_element_type=f32)
    return d(a[0], b[0]) + d(a[0], b[1]) + d(a[1], b[0])


def _unit_lower_inverse(n_list, c):
    rr = lax.broadcasted_iota(jnp.int32, (c, c), 0)
    cc = lax.broadcasted_iota(jnp.int32, (c, c), 1)
    eye = (rr == cc).astype(f32)
    ress = [eye - n for n in n_list]
    pws = [_split_bf16(-n) for n in n_list]
    span = 2
    while span < c:
        pws = [_split_bf16(_dot_split(pw, pw)) for pw in pws]
        ress = [res + _dot_split(_split_bf16(res), pw) for res, pw in zip(ress, pws)]
        span *= 2
    return ress


def _gdn_body(x_ref, sm_ref, gg_ref, cb_ref, cw_ref, al_ref, dtb_ref, gnorm_ref, s0_ref,
              o_ref, s_ref, xc_ref, *, c, nchunk):
    tb = c * nchunk

    @pl.when(pl.program_id(1) == 0)
    def _():
        s_ref[...] = s0_ref[...]
        xc_ref[5:8, :] = cb_ref[0]

    xc_ref[8:8 + tb, :] = x_ref[0]
    y = cw_ref[0:1, :] * xc_ref[pl.ds(5, tb), :]
    for j in range(1, CONV_WIDTH):
        y = y + cw_ref[j:j + 1, :] * xc_ref[pl.ds(5 + j, tb), :]
    tail = xc_ref[pl.ds(5 + tb, 3), :]
    xc_ref[5:8, :] = tail
    y = _silu(y)

    sm = sm_ref[0]
    beta_all = jax.nn.sigmoid(sm)
    g_all = -jnp.exp(al_ref[...]) * _softplus(sm + dtb_ref[...])
    rr = lax.broadcasted_iota(jnp.int32, (c, c), 0)
    cc = lax.broadcasted_iota(jnp.int32, (c, c), 1)
    tri = rr >= cc
    strict = rr > cc
    trif = tri.astype(f32)
    triu = (rr <= cc).astype(f32)
    gg = gg_ref[0]
    H = range(GDN_HEADS)
    P = [(ci, h) for ci in range(nchunk) for h in H]

    qs, ks, vs, betas, Gcs, Ls, kbs, kbfs, n_list = {}, {}, {}, {}, {}, {}, {}, {}, []
    for ci in range(nchunk):
        rows = slice(ci * c, (ci + 1) * c)
        gch = g_all[rows]
        G_cols = jnp.dot(trif, gch, preferred_element_type=f32, precision=_HI)
        G_rows = jnp.dot(gch.T, triu, preferred_element_type=f32, precision=_HI)
        for h in H:
            p = (ci, h)
            hs = lambda base: y[rows, base + h * HEAD_DIM: base + (h + 1) * HEAD_DIM]
            q, k = hs(0), hs(GDN_WIDTH)
            qs[p] = q * lax.rsqrt(jnp.sum(q * q, axis=-1, keepdims=True) + EPS) * (HEAD_DIM ** -0.5)
            k = k * lax.rsqrt(jnp.sum(k * k, axis=-1, keepdims=True) + EPS)
            ks[p] = k
            vs[p] = hs(2 * GDN_WIDTH)
            betas[p] = beta_all[rows, BETA_COL + h: BETA_COL + h + 1]
            Gcs[p] = G_cols[:, A_COL + h: A_COL + h + 1]
            Gr = G_rows[A_COL + h: A_COL + h + 1, :]
            Ls[p] = jnp.exp(jnp.where(tri, Gcs[p] - Gr, -jnp.inf))
            kbs[p] = k * betas[p]
            kbfs[p] = k.astype(bf16)
            n_list.append(jnp.where(strict, lax.dot_general(kbs[p].astype(bf16), kbfs[p], _NT,
                                                            preferred_element_type=f32) * Ls[p], 0.0))
    tinvs = dict(zip(P, [t.astype(bf16) for t in _unit_lower_inverse(n_list, c)]))
    egs = {p: jnp.exp(Gcs[p]) for p in P}
    us = {p: jnp.dot(tinvs[p], (vs[p] * betas[p]).astype(bf16), preferred_element_type=f32) for p in P}
    ws = {p: jnp.dot(tinvs[p], (kbs[p] * egs[p]).astype(bf16), preferred_element_type=f32).astype(bf16) for p in P}
    qks = {p: (lax.dot_general(qs[p].astype(bf16), kbfs[p], _NT, preferred_element_type=f32) * Ls[p]).astype(bf16) for p in P}
    qes = {p: (qs[p] * egs[p]).astype(bf16) for p in P}

    Ss = [s_ref[0, h] for h in H]
    for ci in range(nchunk):
        Sbs = [S.astype(bf16) for S in Ss]
        qss = [jnp.dot(qes[ci, h], Sbs[h], preferred_element_type=f32) for h in H]
        vnbs = [(us[ci, h] - jnp.dot(ws[ci, h], Sbs[h], preferred_element_type=f32)).astype(bf16) for h in H]
        os_ = [qss[h] + jnp.dot(qks[ci, h], vnbs[h], preferred_element_type=f32) for h in H]
        new_s = []
        for h in H:
            g_last = Gcs[ci, h][c - 1:c, :]
            kdec = (ks[ci, h] * jnp.exp(g_last - Gcs[ci, h])).astype(bf16)
            new_s.append(Ss[h] * jnp.exp(g_last) + lax.dot_general(kdec, vnbs[h], _TN, preferred_element_type=f32))
        Ss = new_s
        outs = []
        for h in H:
            o = os_[h]
            yo = o * lax.rsqrt(jnp.mean(o * o, axis=-1, keepdims=True) + EPS) * gnorm_ref[...]
            outs.append(yo * _silu(gg[ci * c:(ci + 1) * c, h * HEAD_DIM:(h + 1) * HEAD_DIM]))
        o_ref[0, ci * c:(ci + 1) * c, :] = jnp.concatenate(outs, axis=1)
    for h in H:
        s_ref[0, h] = Ss[h]


def _gdn(gq3, sm3, gg3, conv_buf, conv_w, a_log, dt_bias, gnorm, s0, T):
    B = gq3.shape[0]
    c = math.gcd(T, GDN_CHUNK)
    nchunk = min(T // c, GDN_CHUNKS_PER_STEP)
    tb = c * nchunk
    C = 3 * GDN_WIDTH
    pad = lambda v: jnp.zeros((1, LANES), f32).at[0, A_COL:A_COL + GDN_HEADS].set(v.astype(f32))
    return pl.pallas_call(
        functools.partial(_gdn_body, c=c, nchunk=nchunk),
        grid=(B, T // tb),
        in_specs=[pl.BlockSpec((1, tb, C), lambda b, j: (b, j, 0)),
                  pl.BlockSpec((1, tb, LANES), lambda b, j: (b, j, 0)),
                  pl.BlockSpec((1, tb, GDN_WIDTH), lambda b, j: (b, j, 0)),
                  pl.BlockSpec((1, CONV_WIDTH - 1, C), lambda b, j: (b, 0, 0)),
                  pl.BlockSpec((CONV_WIDTH, C), lambda b, j: (0, 0)),
                  pl.BlockSpec((1, LANES), lambda b, j: (0, 0)),
                  pl.BlockSpec((1, LANES), lambda b, j: (0, 0)),
                  pl.BlockSpec((1, HEAD_DIM), lambda b, j: (0, 0)),
                  pl.BlockSpec((1, GDN_HEADS, HEAD_DIM, HEAD_DIM), lambda b, j: (b, 0, 0, 0))],
        out_specs=[pl.BlockSpec((1, tb, GDN_WIDTH), lambda b, j: (b, j, 0)),
                   pl.BlockSpec((1, GDN_HEADS, HEAD_DIM, HEAD_DIM), lambda b, j: (b, 0, 0, 0))],
        out_shape=[jax.ShapeDtypeStruct((B, T, GDN_WIDTH), f32),
                   jax.ShapeDtypeStruct((B, GDN_HEADS, HEAD_DIM, HEAD_DIM), f32)],
        scratch_shapes=[pltpu.VMEM((8 + tb + 8, C), f32)],
        compiler_params=_cparams(("parallel", "arbitrary")),
        name="gdn",
    )(gq3, sm3, gg3, conv_buf, conv_w, pad(a_log), pad(dt_bias), gnorm, s0)


def _cmp_weights(w1, b1, w2):
    nq = 2 * NSA_KV_HEADS
    kind = np.arange(nq) // NSA_KV_HEADS
    eye = jnp.eye(nq, dtype=f32)
    r = CMP_BLOCK // CMP_STRIDE
    w = w1[kind].reshape(nq, r, CMP_STRIDE, HEAD_DIM, HEAD_DIM)
    w = jnp.transpose(w, (2, 0, 3, 1, 4))
    wc = (w[:, :, :, :, None, :] * eye[None, :, None, None, :, None])
    wc = wc.reshape(CMP_STRIDE, nq * HEAD_DIM, r * nq * HEAD_DIM).astype(bf16)
    b1t = b1[kind].reshape(1, nq * HEAD_DIM)
    w2bd = (w2[kind][:, :, None, :] * eye[:, None, :, None]).reshape(nq * HEAD_DIM, nq * HEAD_DIM).astype(bf16)
    w7 = wc.reshape(CMP_STRIDE // 2, 2, 2, LANES, r, 2, LANES)
    wp = jnp.stack([w7[:, :, pr, :, :, pr, :].reshape(CMP_STRIDE // 2, 2 * LANES, r * LANES) for pr in range(2)])
    return wp, b1t, w2bd


def _segment_perm():
    seg = PAGE_SIZE // CMP_STRIDE
    off = np.arange(PAGE_SIZE)
    return jnp.asarray(((off[None, :] % CMP_STRIDE) * seg + off[None, :] // CMP_STRIDE == off[:, None]).astype(np.float32)).astype(bf16)


def _first_layer(xt_ref, wp_ref, n_row):
    accs = [jnp.zeros((n_row, 4 * HEAD_DIM), f32) for _ in range(2)]
    for s2 in range(CMP_STRIDE // 2):
        x0 = xt_ref[2 * s2]
        x1 = xt_ref[2 * s2 + 1]
        for pr in range(2):
            xs = jnp.concatenate([x0[:, pr * LANES:(pr + 1) * LANES], x1[:, pr * LANES:(pr + 1) * LANES]], axis=1)
            accs[pr] = accs[pr] + jnp.dot(xs, wp_ref[pr, s2], preferred_element_type=f32)
    return jnp.concatenate([accs[0][:, :LANES], accs[1][:, :LANES], accs[0][:, LANES:], accs[1][:, LANES:]], axis=1)


def _cmp_prompt_body(x_ref, perm_ref, wp_ref, b1_ref, w2_ref, o_ref, xt_ref, *, n_seg):
    seg = PAGE_SIZE // CMP_STRIDE
    for i in range(n_seg // seg):
        xp = jnp.dot(perm_ref[...], x_ref[0, i * PAGE_SIZE:(i + 1) * PAGE_SIZE, :].astype(bf16), preferred_element_type=f32)
        for s in range(CMP_STRIDE):
            xt_ref[s, i * seg:(i + 1) * seg, :] = xp[s * seg:(s + 1) * seg, :].astype(bf16)
    acc = _first_layer(xt_ref, wp_ref, n_seg)
    first = acc[:, :4 * HEAD_DIM]
    second = pltpu.roll(acc[:, 4 * HEAD_DIM:], n_seg - 1, 0)
    hid = _silu(b1_ref[...] + first + second)
    out = jnp.dot(hid.astype(bf16), w2_ref[...], preferred_element_type=f32)
    for qi in range(2 * NSA_KV_HEADS):
        o_ref[0, qi] = out[:, qi * HEAD_DIM:(qi + 1) * HEAD_DIM]


def _compress_prompt(kv3, wp, b1t, w2bd):
    B, T, _ = kv3.shape
    assert T % PAGE_SIZE == 0
    n_seg = T // CMP_STRIDE
    return pl.pallas_call(
        functools.partial(_cmp_prompt_body, n_seg=n_seg),
        grid=(B,),
        in_specs=[pl.BlockSpec((1, T, 4 * HEAD_DIM), lambda b: (b, 0, 0)),
                  pl.BlockSpec((PAGE_SIZE, PAGE_SIZE), lambda b: (0, 0)),
                  pl.BlockSpec((2, CMP_STRIDE // 2, 4 * HEAD_DIM, 4 * HEAD_DIM), lambda b: (0, 0, 0, 0)),
                  pl.BlockSpec((1, 4 * HEAD_DIM), lambda b: (0, 0)),
                  pl.BlockSpec((4 * HEAD_DIM, 4 * HEAD_DIM), lambda b: (0, 0))],
        out_specs=pl.BlockSpec((1, 4, n_seg, HEAD_DIM), lambda b: (b, 0, 0, 0)),
        out_shape=jax.ShapeDtypeStruct((B, 4, n_seg, HEAD_DIM), f32),
        scratch_shapes=[pltpu.VMEM((CMP_STRIDE, n_seg, 4 * HEAD_DIM), bf16)],
        compiler_params=_cparams(("parallel",)),
        name="compress_prompt",
    )(kv3, _segment_perm(), wp, b1t, w2bd)


def _topk_rows(score, k):
    n = score.shape[0]
    idx = lax.broadcasted_iota(jnp.int32, score.shape, 0)
    taken = jnp.zeros(score.shape, jnp.bool_)
    for _ in range(k):
        work = jnp.where(taken, -jnp.inf, score)
        m = jnp.max(work, axis=0, keepdims=True)
        cand = jnp.where((work == m) & jnp.logical_not(taken), idx, n)
        first = jnp.min(cand, axis=0, keepdims=True)
        taken = taken | (idx == first)
    return jnp.where(taken, 1.0, 0.0)


def _topk_rows_by_rank(score, k):
    n = score.shape[0]
    idx = lax.broadcasted_iota(jnp.int32, score.shape, 0)
    rank = jnp.zeros(score.shape, jnp.int32)
    for i in range(n):
        row = score[i:i + 1, :]
        ahead = (row > score) | ((row == score) & (idx > i))
        rank = rank + ahead.astype(jnp.int32)
    return jnp.where(rank < k, 1.0, 0.0)


def _softmax_rows(s, mask):
    s = jnp.where(mask, s, -jnp.inf)
    m = jnp.max(s, axis=-1, keepdims=True)
    m = jnp.where(m == -jnp.inf, 0.0, m)
    p = jnp.exp(s - m)
    return p / jnp.maximum(jnp.sum(p, axis=-1, keepdims=True), 1e-30)


def _attend_tile(qs, kbs, vbs, biases, carries, tq):
    n = range(len(qs))
    R = qs[0].shape[0]
    tk = kbs[0].shape[0]
    ss = [lax.dot_general(qs[i], kbs[i], _NT, preferred_element_type=f32) for i in n]
    ss = [(ss[i].reshape(NSA_GROUP, tq, tk) + biases[i][None]).reshape(R, tk) for i in n]
    m_news = [jnp.maximum(carries[i][0], jnp.max(ss[i], axis=-1, keepdims=True)) for i in n]
    m_safes = [jnp.where(m == -jnp.inf, 0.0, m) for m in m_news]
    alphas = [jnp.exp(carries[i][0] - m_safes[i]) for i in n]
    ps = [jnp.exp(ss[i] - m_safes[i]) for i in n]
    ls = [alphas[i] * carries[i][1] + jnp.sum(ps[i], axis=-1, keepdims=True) for i in n]
    accs = [alphas[i] * carries[i][2] + jnp.dot(ps[i].astype(bf16), vbs[i], preferred_element_type=f32) for i in n]
    return [(m_news[i], ls[i], accs[i]) for i in n]


def _flash_init(R):
    return (jnp.full((R, 1), -jnp.inf, f32), jnp.zeros((R, 1), f32), jnp.zeros((R, HEAD_DIM), f32))


def _nsa_prompt_body(q_ref, qr_ref, kc_ref, ks_ref, vs_ref, kw_ref, vw_ref, sm_ref, ov_ref, ex_ref,
                     o_ref, wbias_ref, *, tq, tk, tkw, n_wt, T, n_cmp, n_blk):
    i = pl.program_id(1)
    t0 = i * tq
    scale = HEAD_DIM ** -0.5
    G = range(NSA_KV_HEADS)
    R = NSA_GROUP * tq
    gates = jax.nn.sigmoid(sm_ref[0])
    n_pad = kc_ref.shape[2]
    tpos_r = t0 + lax.broadcasted_iota(jnp.int32, (tq, 1), 0)
    tpos_c = t0 + lax.broadcasted_iota(jnp.int32, (1, tq), 1)
    eye = (lax.broadcasted_iota(jnp.int32, (tq, tq), 0) == lax.broadcasted_iota(jnp.int32, (tq, tq), 1)).astype(bf16)
    heads = lambda x, g: [(x[:, (g * NSA_GROUP + r) * HEAD_DIM:(g * NSA_GROUP + r + 1) * HEAD_DIM] * scale).astype(bf16)
                          for r in range(NSA_GROUP)]

    o_cmps, imps = [], []
    nn = lax.broadcasted_iota(jnp.int32, (1, n_pad), 1)
    cmask = (nn * CMP_STRIDE + (CMP_BLOCK - 1) <= tpos_r) & (nn < n_cmp)
    HH = [(g, r) for g in G for r in range(NSA_GROUP)]
    qhs = {g: heads(q_ref[0], g) for g in G}
    kcs = {g: kc_ref[0, g].astype(bf16) for g in G}
    vcs = {g: kc_ref[0, NSA_KV_HEADS + g].astype(bf16) for g in G}
    ss = {h: jnp.where(cmask, lax.dot_general(qhs[h[0]][h[1]], kcs[h[0]], _NT, preferred_element_type=f32), -jnp.inf)
          for h in HH}
    ms = {h: jnp.max(ss[h], axis=-1, keepdims=True) for h in HH}
    es = {h: jnp.exp(ss[h] - jnp.where(ms[h] == -jnp.inf, 0.0, ms[h])) for h in HH}
    ps = {h: es[h] / jnp.maximum(jnp.sum(es[h], axis=-1, keepdims=True), 1e-30) for h in HH}
    ocs = {h: jnp.dot(ps[h].astype(bf16), vcs[h[0]], preferred_element_type=f32) for h in HH}
    for g in G:
        o_cmps.append([ocs[g, r] for r in range(NSA_GROUP)])
        imps.append(ps[g, 0] + ps[g, 1] + ps[g, 2] + ps[g, 3])

    score = jnp.concatenate([lax.dot_general(ov_ref[...], imps[g], _NT, preferred_element_type=f32, precision=_HI)[:n_blk]
                             for g in G], axis=1)
    jj = lax.broadcasted_iota(jnp.int32, score.shape, 0)
    jt = jnp.concatenate([tpos_c] * NSA_KV_HEADS, axis=1) // SEL_BLOCK
    valid = jj <= jt
    forced = valid & ((jj == 0) | (jj == jt) | (jj == jt - 1))
    score = jnp.where(forced, jnp.inf, jnp.where(valid, score, -jnp.inf))
    sel_t = _topk_rows_by_rank(score, min(N_SELECT, n_blk)).astype(bf16)
    if n_blk < LANES:
        sel_t = jnp.concatenate([sel_t, jnp.zeros((LANES - n_blk, NSA_KV_HEADS * tq), bf16)], axis=0)
    sels = [lax.dot_general(eye, sel_t[:, g * tq:(g + 1) * tq], _NT, preferred_element_type=f32).astype(bf16)
            for g in G]

    w0 = pl.multiple_of(jnp.clip(t0 - WINDOW, 0, T - n_wt * tkw), tq)
    for j in range(n_wt):
        dist = tpos_r - (w0 + j * tkw + lax.broadcasted_iota(jnp.int32, (1, tkw), 1))
        wbias_ref[j] = jnp.where((dist >= 0) & (dist < WINDOW), 0.0, -jnp.inf)

    stack = lambda x, g: (jnp.concatenate([x[:, (g * NSA_GROUP + r) * HEAD_DIM:(g * NSA_GROUP + r + 1) * HEAD_DIM]
                                           for r in range(NSA_GROUP)], axis=0) * scale).astype(bf16)
    qrs = [stack(qr_ref[0], g) for g in G]
    kv_tile = lambda ref, k0, n, g: ref[0, pl.ds(k0, n), g * HEAD_DIM:(g + 1) * HEAD_DIM].astype(bf16)

    def sel_step(kt, carries):
        k0 = pl.multiple_of(kt * tk, tk)
        causal = k0 + lax.broadcasted_iota(jnp.int32, (1, tk), 1) <= tpos_r
        biases = [jnp.where((jnp.dot(sels[g], ex_ref[kt], preferred_element_type=f32) > 0.5) & causal, 0.0, -jnp.inf)
                  for g in G]
        return tuple(_attend_tile(qrs, [kv_tile(ks_ref, k0, tk, g) for g in G], [kv_tile(vs_ref, k0, tk, g) for g in G],
                                  biases, carries, tq))

    sel = lax.fori_loop(0, (t0 + tq + tk - 1) // tk, sel_step, tuple(_flash_init(R) for _ in G))

    win = [_flash_init(R) for _ in G]
    for j in range(n_wt):
        k0 = pl.multiple_of(w0 + j * tkw, tq)
        win = _attend_tile(qrs, [kv_tile(kw_ref, k0, tkw, g) for g in G], [kv_tile(vw_ref, k0, tkw, g) for g in G],
                           [wbias_ref[j]] * NSA_KV_HEADS, win, tq)

    for g in G:
        o_sel = sel[g][2] / jnp.maximum(sel[g][1], 1e-30)
        o_win = win[g][2] / jnp.maximum(win[g][1], 1e-30)
        outs = []
        for r in range(NSA_GROUP):
            hcol = GATE_COL + g * NSA_GROUP + r
            rows = slice(r * tq, (r + 1) * tq)
            outs.append(gates[:, hcol:hcol + 1] * o_cmps[g][r]
                        + gates[:, NSA_HEADS + hcol:NSA_HEADS + hcol + 1] * o_sel[rows]
                        + gates[:, 2 * NSA_HEADS + hcol:2 * NSA_HEADS + hcol + 1] * o_win[rows])
        o_ref[0, :, g * NSA_GROUP * HEAD_DIM:(g + 1) * NSA_GROUP * HEAD_DIM] = jnp.concatenate(outs, axis=1)


def _overlap_t(n_blk_pad, n_cmp, n_pad):
    ci = np.arange(n_pad)[None, :] * CMP_STRIDE
    sj = np.arange(n_blk_pad)[:, None] * SEL_BLOCK
    ov = (ci < sj + SEL_BLOCK) & (ci + CMP_BLOCK > sj) & (np.arange(n_pad)[None, :] < n_cmp)
    return jnp.asarray(ov.astype(np.float32))


def _nsa_prompt(q3, qr3, kc, kv3, win3, sm3):
    B, T, _ = q3.shape
    tq = 128
    tk = 512 if T >= 1024 else 256
    tkw = WINDOW + tq if T >= 1024 else 128
    n_wt = -(-min(T, WINDOW + tq) // tkw)
    assert T % tk == 0 and n_wt * tkw <= T
    n_seg = T // CMP_STRIDE
    n_cmp = n_seg - CMP_BLOCK // CMP_STRIDE + 1
    n_blk = -(-T // SEL_BLOCK)
    ov = _overlap_t(LANES, n_cmp, n_seg)
    key_blk = (np.arange(T // tk)[:, None, None] * tk + np.arange(tk)[None, None, :]) // SEL_BLOCK
    ex = jnp.asarray((np.arange(LANES)[None, :, None] == key_blk).astype(np.float32)).astype(bf16)
    qspec = pl.BlockSpec((1, tq, NSA_WIDTH), lambda b, i: (b, i, 0))
    col = lambda j: pl.BlockSpec((1, T, LANES), lambda b, i: (b, 0, j))
    return pl.pallas_call(
        functools.partial(_nsa_prompt_body, tq=tq, tk=tk, tkw=tkw, n_wt=n_wt, T=T, n_cmp=n_cmp, n_blk=n_blk),
        grid=(B, T // tq),
        in_specs=[qspec, qspec,
                  pl.BlockSpec((1, 4, n_seg, HEAD_DIM), lambda b, i: (b, 0, 0, 0)),
                  col(2), col(3), col(0), col(1),
                  pl.BlockSpec((1, tq, LANES), lambda b, i: (b, i, 0)),
                  pl.BlockSpec((LANES, n_seg), lambda b, i: (0, 0)),
                  pl.BlockSpec((T // tk, LANES, tk), lambda b, i: (0, 0, 0))],
        out_specs=qspec,
        out_shape=jax.ShapeDtypeStruct((B, T, NSA_WIDTH), f32),
        scratch_shapes=[pltpu.VMEM((n_wt, tq, tkw), f32)],
        compiler_params=_cparams(("parallel", "arbitrary")),
        name="nsa_prompt",
    )(q3, qr3, kc, kv3, kv3, win3, win3, sm3, ov, ex)


def _out_body(x_ref, ro_ref, no_ref, ng_ref, go_ref, w_ref, g_ref, y_ref):
    no = no_ref[...] * _silu(ng_ref[...])
    z = (jnp.dot(ro_ref[...].astype(bf16), w_ref[0:RET_WIDTH, :], preferred_element_type=f32)
         + jnp.dot(no.astype(bf16), w_ref[RET_WIDTH:RET_WIDTH + NSA_WIDTH, :], preferred_element_type=f32)
         + jnp.dot(go_ref[...].astype(bf16), w_ref[RET_WIDTH + NSA_WIDTH:, :], preferred_element_type=f32))
    y_ref[...] = x_ref[...] + z * lax.rsqrt(jnp.mean(z * z, axis=-1, keepdims=True) + EPS) * g_ref[...]


def _out(x2d, ro, no, ng, go, w_out_b, gain, tm):
    M = x2d.shape[0]
    row = lambda w: pl.BlockSpec((tm, w), lambda i: (i, 0))
    return pl.pallas_call(
        _out_body,
        grid=(M // tm,),
        in_specs=[row(D_MODEL), row(RET_WIDTH), row(NSA_WIDTH), row(NSA_WIDTH), row(GDN_WIDTH),
                  pl.BlockSpec((D_MODEL, D_MODEL), lambda i: (0, 0)),
                  pl.BlockSpec((1, D_MODEL), lambda i: (0, 0))],
        out_specs=row(D_MODEL),
        out_shape=jax.ShapeDtypeStruct((M, D_MODEL), f32),
        compiler_params=_cparams(("parallel",)),
        name="out",
    )(x2d, ro, no, ng, go, w_out_b, gain)


PAGES_PER_STEP = 64


def _page_specs(l, pg, half):
    return [pl.BlockSpec((None, None, 4 * HEAD_DIM, PAGE_SIZE),
                         lambda b, j, pt, i=i: (l, pt[b, j * pg + i], half, 0)) for i in range(pg)]


def _cmp_sample_body(pt_ref, *refs, pg):
    page_refs, perm_ref, wp_ref, o_ref, xt_ref = refs[:pg], refs[pg], refs[pg + 1], refs[pg + 2], refs[pg + 3]
    seg = PAGE_SIZE // CMP_STRIDE
    for i in range(pg):
        xp = lax.dot_general(perm_ref[...], page_refs[i][...].astype(bf16), _NT, preferred_element_type=f32)
        for s in range(CMP_STRIDE):
            xt_ref[s, i * seg:(i + 1) * seg, :] = xp[s * seg:(s + 1) * seg, :].astype(bf16)
    o_ref[0] = _first_layer(xt_ref, wp_ref, pg * seg)


def _cmp_sample(cache_t, l, page_table, wp):
    B, n_pages = page_table.shape
    pg = math.gcd(n_pages, PAGES_PER_STEP)
    n_row = pg * PAGE_SIZE // CMP_STRIDE
    return pl.pallas_call(
        functools.partial(_cmp_sample_body, pg=pg),
        grid_spec=pltpu.PrefetchScalarGridSpec(
            num_scalar_prefetch=1,
            grid=(B, n_pages // pg),
            in_specs=_page_specs(l, pg, 0) + [pl.BlockSpec((PAGE_SIZE, PAGE_SIZE), lambda b, j, pt: (0, 0)),
                                              pl.BlockSpec((2, CMP_STRIDE // 2, 4 * HEAD_DIM, 4 * HEAD_DIM),
                                                           lambda b, j, pt: (0, 0, 0, 0))],
            out_specs=pl.BlockSpec((1, n_row, 8 * HEAD_DIM), lambda b, j, pt: (b, j, 0)),
            scratch_shapes=[pltpu.VMEM((CMP_STRIDE, n_row, 4 * HEAD_DIM), bf16)]),
        out_shape=jax.ShapeDtypeStruct((B, n_pages * PAGE_SIZE // CMP_STRIDE, 8 * HEAD_DIM), f32),
        compiler_params=_cparams(("parallel", "arbitrary")),
        name="compress_sample",
    )(page_table, *([cache_t] * pg), _segment_perm(), wp)


def _stack_heads(x, g):
    return jnp.concatenate([x[:, (g * NSA_GROUP + r) * HEAD_DIM:(g * NSA_GROUP + r + 1) * HEAD_DIM]
                            for r in range(NSA_GROUP)], axis=0)


def _nsa_sample_mid_body(p_ref, b1_ref, w2_ref, q_ref, qr_ref, win_ref, st_ref, ov_ref,
                         ocmp_ref, owin_ref, selt_ref, *, Ts, n_cmp, past_len, wb):
    P = p_ref[0]
    n_seg = P.shape[0]
    hid = _silu(b1_ref[...] + P[:, :4 * HEAD_DIM] + pltpu.roll(P[:, 4 * HEAD_DIM:], n_seg - 1, 0))
    kcv = jnp.dot(hid.astype(bf16), w2_ref[...], preferred_element_type=f32)
    scale = HEAD_DIM ** -0.5
    R = NSA_GROUP * Ts
    trow = lax.broadcasted_iota(jnp.int32, (R, 1), 0) % Ts
    imps = []
    for g in range(NSA_KV_HEADS):
        qs = (_stack_heads(q_ref[0], g) * scale).astype(bf16)
        kc = kcv[:, g * HEAD_DIM:(g + 1) * HEAD_DIM].astype(bf16)
        vc = kcv[:, (NSA_KV_HEADS + g) * HEAD_DIM:(NSA_KV_HEADS + g + 1) * HEAD_DIM].astype(bf16)
        s = lax.dot_general(qs, kc, _NT, preferred_element_type=f32)
        nn = lax.broadcasted_iota(jnp.int32, (1, n_seg), 1)
        cmask = (nn * CMP_STRIDE + (CMP_BLOCK - 1) <= past_len + trow) & (nn < n_cmp)
        p = _softmax_rows(s, cmask)
        ocmp_ref[0, g] = jnp.dot(p.astype(bf16), vc, preferred_element_type=f32)
        imps.append(jnp.sum(p.reshape(NSA_GROUP, Ts, n_seg), axis=0))
        qrs = (_stack_heads(qr_ref[0], g) * scale).astype(bf16)
        kwt = st_ref[g * HEAD_DIM:(g + 1) * HEAD_DIM, :].astype(bf16)
        vwt = st_ref[(NSA_KV_HEADS + g) * HEAD_DIM:(NSA_KV_HEADS + g + 1) * HEAD_DIM, :].astype(bf16)
        wn = win_ref[0]
        knew = wn[:, g * HEAD_DIM:(g + 1) * HEAD_DIM].astype(bf16)
        vnew = wn[:, (NSA_KV_HEADS + g) * HEAD_DIM:(NSA_KV_HEADS + g + 1) * HEAD_DIM].astype(bf16)
        s_b = jnp.dot(qrs, kwt, preferred_element_type=f32)
        s_n = lax.dot_general(qrs, knew, _NT, preferred_element_type=f32)
        jb = lax.broadcasted_iota(jnp.int32, (1, wb), 1)
        dist_b = trow + wb - jb
        s_b = jnp.where((dist_b >= 0) & (dist_b < WINDOW) & (past_len - wb + jb >= 0), s_b, -jnp.inf)
        dist_n = trow - lax.broadcasted_iota(jnp.int32, (1, Ts), 1)
        s_n = jnp.where((dist_n >= 0) & (dist_n < WINDOW), s_n, -jnp.inf)
        m = jnp.maximum(jnp.max(s_b, axis=-1, keepdims=True), jnp.max(s_n, axis=-1, keepdims=True))
        m = jnp.where(m == -jnp.inf, 0.0, m)
        pb = jnp.exp(s_b - m)
        pn = jnp.exp(s_n - m)
        den = jnp.sum(pb, axis=-1, keepdims=True) + jnp.sum(pn, axis=-1, keepdims=True)
        o = (lax.dot_general(pb.astype(bf16), vwt, _NT, preferred_element_type=f32)
             + jnp.dot(pn.astype(bf16), vnew, preferred_element_type=f32))
        owin_ref[0, g] = o / jnp.maximum(den, 1e-30)

    imp2 = jnp.concatenate(imps, axis=0)
    score = lax.dot_general(ov_ref[...], imp2, _NT, preferred_element_type=f32, precision=_HI)
    jj = lax.broadcasted_iota(jnp.int32, score.shape, 0)
    jt = (past_len + lax.broadcasted_iota(jnp.int32, score.shape, 1) % Ts) // SEL_BLOCK
    valid = jj <= jt
    forced = valid & ((jj == 0) | (jj == jt) | (jj == jt - 1))
    score = jnp.where(forced, jnp.inf, jnp.where(valid, score, -jnp.inf))
    selt_ref[0] = _topk_rows(score, N_SELECT)


def _nsa_sample_mid(P, b1t, w2bd, q3, qr3, win3, st_t, l, past_len):
    B, Ts, _ = q3.shape
    n_seg = P.shape[1]
    n_cmp = n_seg - CMP_BLOCK // CMP_STRIDE + 1
    n_blk = -(-(past_len + Ts) // SEL_BLOCK)
    nb_pad = -(-n_blk // 8) * 8
    wb = st_t.shape[-1]
    ov = _overlap_t(nb_pad, n_cmp, n_seg)
    R = NSA_GROUP * Ts
    tok = lambda w: pl.BlockSpec((1, Ts, w), lambda b: (b, 0, 0))
    return pl.pallas_call(
        functools.partial(_nsa_sample_mid_body, Ts=Ts, n_cmp=n_cmp, past_len=past_len, wb=wb),
        grid=(B,),
        in_specs=[pl.BlockSpec((1, n_seg, 8 * HEAD_DIM), lambda b: (b, 0, 0)),
                  pl.BlockSpec((1, 4 * HEAD_DIM), lambda b: (0, 0)),
                  pl.BlockSpec((4 * HEAD_DIM, 4 * HEAD_DIM), lambda b: (0, 0)),
                  tok(NSA_WIDTH), tok(NSA_WIDTH), tok(4 * HEAD_DIM),
                  pl.BlockSpec((None, None, 4 * HEAD_DIM, wb), lambda b: (l, b, 0, 0)),
                  pl.BlockSpec((nb_pad, n_seg), lambda b: (0, 0))],
        out_specs=[pl.BlockSpec((1, NSA_KV_HEADS, R, HEAD_DIM), lambda b: (b, 0, 0, 0)),
                   pl.BlockSpec((1, NSA_KV_HEADS, R, HEAD_DIM), lambda b: (b, 0, 0, 0)),
                   pl.BlockSpec((1, nb_pad, NSA_KV_HEADS * Ts), lambda b: (b, 0, 0))],
        out_shape=[jax.ShapeDtypeStruct((B, NSA_KV_HEADS, R, HEAD_DIM), f32),
                   jax.ShapeDtypeStruct((B, NSA_KV_HEADS, R, HEAD_DIM), f32),
                   jax.ShapeDtypeStruct((B, nb_pad, NSA_KV_HEADS * Ts), f32)],
        compiler_params=_cparams(("parallel",)),
        name="nsa_sample_mid",
    )(P, b1t, w2bd, q3, qr3, win3, st_t, ov)


def _nsa_sample_sel_body(pt_ref, *refs, pg, Ts, n_steps):
    page_refs = refs[:pg]
    qr_ref, kvn_ref, selt_ref, ex_ref, ocmp_ref, owin_ref, sm_ref, o_ref, m_ref, l_ref, acc_ref = refs[pg:]
    j = pl.program_id(1)
    scale = HEAD_DIM ** -0.5
    R = NSA_GROUP * Ts
    GT = NSA_KV_HEADS * Ts
    nb = 2 * pg
    trow = lax.broadcasted_iota(jnp.int32, (R, 1), 0) % Ts
    eye = (lax.broadcasted_iota(jnp.int32, (GT, GT), 0) == lax.broadcasted_iota(jnp.int32, (GT, GT), 1)).astype(bf16)
    sel_j = selt_ref[0, pl.ds(pl.multiple_of(j * nb, nb), nb), :].astype(bf16)
    if nb < LANES:
        sel_j = jnp.concatenate([sel_j, jnp.zeros((LANES - nb, GT), bf16)], axis=0)
    sel_rows = lax.dot_general(eye, sel_j, _NT, preferred_element_type=f32)
    full = jnp.dot(sel_rows.astype(bf16), ex_ref[...], preferred_element_type=f32)

    G = range(NSA_KV_HEADS)
    qrs = [(_stack_heads(qr_ref[0], g) * scale).astype(bf16) for g in G]

    @pl.when(j == 0)
    def _():
        kvn = kvn_ref[0]
        for g in G:
            knew = kvn[:, (2 * NSA_KV_HEADS + g) * HEAD_DIM:(2 * NSA_KV_HEADS + g + 1) * HEAD_DIM].astype(bf16)
            vnew = kvn[:, (3 * NSA_KV_HEADS + g) * HEAD_DIM:(3 * NSA_KV_HEADS + g + 1) * HEAD_DIM].astype(bf16)
            s_n = lax.dot_general(qrs[g], knew, _NT, preferred_element_type=f32)
            s_n = jnp.where(lax.broadcasted_iota(jnp.int32, (1, Ts), 1) <= trow, s_n, -jnp.inf)
            m0 = jnp.max(s_n, axis=-1, keepdims=True)
            p0 = jnp.exp(s_n - m0)
            m_ref[g] = m0
            l_ref[g] = jnp.sum(p0, axis=-1, keepdims=True)
            acc_ref[g] = jnp.dot(p0.astype(bf16), vnew, preferred_element_type=f32)

    kts = [jnp.concatenate([page_refs[i][g * HEAD_DIM:(g + 1) * HEAD_DIM, :] for i in range(pg)], axis=1).astype(bf16)
           for g in G]
    vts = [jnp.concatenate([page_refs[i][(NSA_KV_HEADS + g) * HEAD_DIM:(NSA_KV_HEADS + g + 1) * HEAD_DIM, :]
                            for i in range(pg)], axis=1).astype(bf16) for g in G]
    ss = [jnp.dot(qrs[g], kts[g], preferred_element_type=f32) for g in G]
    ss = [jnp.where(jnp.concatenate([full[g * Ts:(g + 1) * Ts]] * NSA_GROUP, axis=0) > 0.5, ss[g], -jnp.inf) for g in G]
    m_olds = [m_ref[g] for g in G]
    m_news = [jnp.maximum(m_olds[g], jnp.max(ss[g], axis=-1, keepdims=True)) for g in G]
    alphas = [jnp.exp(m_olds[g] - m_news[g]) for g in G]
    ps = [jnp.exp(ss[g] - m_news[g]) for g in G]
    pvs = [lax.dot_general(ps[g].astype(bf16), vts[g], _NT, preferred_element_type=f32) for g in G]
    for g in G:
        l_ref[g] = alphas[g] * l_ref[g] + jnp.sum(ps[g], axis=-1, keepdims=True)
        acc_ref[g] = alphas[g] * acc_ref[g] + pvs[g]
        m_ref[g] = m_news[g]


    @pl.when(j == n_steps - 1)
    def _():
        gates = jax.nn.sigmoid(sm_ref[0])
        outs = []
        for g in range(NSA_KV_HEADS):
            o_sel = acc_ref[g] / jnp.maximum(l_ref[g], 1e-30)
            o_cmp = ocmp_ref[0, g]
            o_win = owin_ref[0, g]
            for r in range(NSA_GROUP):
                hcol = GATE_COL + g * NSA_GROUP + r
                rows = slice(r * Ts, (r + 1) * Ts)
                outs.append(gates[:, hcol:hcol + 1] * o_cmp[rows]
                            + gates[:, NSA_HEADS + hcol:NSA_HEADS + hcol + 1] * o_sel[rows]
                            + gates[:, 2 * NSA_HEADS + hcol:2 * NSA_HEADS + hcol + 1] * o_win[rows])
        o_ref[0] = jnp.concatenate(outs, axis=1)


def _nsa_sample_sel(cache_t, l, page_table, qr3, kv3, selt, ocmp, owin, sm3):
    B, n_pages = page_table.shape
    Ts = qr3.shape[1]
    pg = math.gcd(n_pages, PAGES_PER_STEP)
    n_steps = n_pages // pg
    R = NSA_GROUP * Ts
    ex = jnp.asarray((np.arange(LANES)[:, None] == (np.arange(pg * PAGE_SIZE)[None, :] // SEL_BLOCK)).astype(np.float32)).astype(bf16)
    tok = lambda w: pl.BlockSpec((1, Ts, w), lambda b, j, pt: (b, 0, 0))
    stk = pl.BlockSpec((1, NSA_KV_HEADS, R, HEAD_DIM), lambda b, j, pt: (b, 0, 0, 0))
    return pl.pallas_call(
        functools.partial(_nsa_sample_sel_body, pg=pg, Ts=Ts, n_steps=n_steps),
        grid_spec=pltpu.PrefetchScalarGridSpec(
            num_scalar_prefetch=1,
            grid=(B, n_steps),
            in_specs=_page_specs(l, pg, 1) + [
                tok(NSA_WIDTH), tok(8 * HEAD_DIM),
                pl.BlockSpec((1, selt.shape[1], selt.shape[2]), lambda b, j, pt: (b, 0, 0)),
                pl.BlockSpec((LANES, pg * PAGE_SIZE), lambda b, j, pt: (0, 0)),
                stk, stk, tok(LANES)],
            out_specs=tok(NSA_WIDTH),
            scratch_shapes=[pltpu.VMEM((NSA_KV_HEADS, R, 1), f32), pltpu.VMEM((NSA_KV_HEADS, R, 1), f32),
                            pltpu.VMEM((NSA_KV_HEADS, R, HEAD_DIM), f32)]),
        out_shape=jax.ShapeDtypeStruct((B, Ts, NSA_WIDTH), f32),
        compiler_params=_cparams(("parallel", "arbitrary")),
        name="nsa_sample_sel",
    )(page_table, *([cache_t] * pg), qr3, kv3, selt, ex, ocmp, owin, sm3)


def _nsa_sample(q3, qr3, kv3, win3, sm3, cache_t, st_t, l, page_table, wp, b1t, w2bd):
    Ts = q3.shape[1]
    n_pages = page_table.shape[1]
    past_len = n_pages * PAGE_SIZE
    assert (past_len + Ts) // CMP_STRIDE * CMP_STRIDE <= past_len
    assert past_len % SEL_BLOCK == 0 and Ts <= SEL_BLOCK
    P = _cmp_sample(cache_t, l, page_table, wp)
    ocmp, owin, selt = _nsa_sample_mid(P, b1t, w2bd, q3, qr3, win3, st_t, l, past_len)
    return _nsa_sample_sel(cache_t, l, page_table, qr3, kv3, selt, ocmp, owin, sm3)


def _rope_tables(pos):
    half = HEAD_DIM // 2
    inv = ROPE_THETA ** (-jnp.arange(half, dtype=f32) / half)
    ang = pos.astype(f32)[:, None] * inv[None, :]
    cos, sin = jnp.cos(ang), jnp.sin(ang)
    return jnp.concatenate([cos, cos, cos, cos], axis=1), jnp.concatenate([-sin, sin, -sin, sin], axis=1)


def _prep_w_in(w):
    cols = [w[:, _OFF[n][0]:_OFF[n][1]] for n in _PAD_ORDER]
    used = sum(c.shape[1] for c in cols)
    cols.append(jnp.zeros((w.shape[0], PROJ_PAD - used), w.dtype))
    return jnp.concatenate(cols, axis=1).astype(bf16)


def _layer(x, pos, prm, l, nsa_fn, ret_state, gdn_state, conv_buf, tm):
    B, T, _ = x.shape
    M = B * T
    tm = min(tm, M)
    cosf, sinf = _rope_tables(pos)
    if tm > T:
        cosf, sinf = jnp.tile(cosf, (tm // T, 1)), jnp.tile(sinf, (tm // T, 1))
    x2d = x.reshape(M, D_MODEL)
    ret, q, qr, kv, win, ng, gq, gg, sm = _proj(x2d, prm['norm_pre'][l][None], prm['w_in_p'][l], cosf, sinf, tm)
    r3 = lambda a: a.reshape(B, T, a.shape[-1])
    ro, ret_new = _retention(r3(ret), ret_state, prm['ret_gn'][l][None], T)
    go, gdn_new = _gdn(r3(gq), r3(sm), r3(gg), conv_buf, prm['gdn_conv'][l], prm['gdn_a_log'][l],
                       prm['gdn_dt_bias'][l], prm['gdn_norm'][l][None], gdn_state, T)
    assert T >= CONV_WIDTH - 1
    conv_new = r3(gq)[:, T - (CONV_WIDTH - 1):]
    no, win_new = nsa_fn(r3(q), r3(qr), r3(kv), r3(win), r3(sm))
    y = _out(x2d, ro.reshape(M, -1), no.reshape(M, -1), ng, go.reshape(M, -1), prm['w_out_b'][l],
             prm['norm_post'][l][None], tm)
    rows = kv.reshape(B, T, 4, NSA_KV_HEADS, HEAD_DIM)
    return y.reshape(B, T, D_MODEL), (rows, win_new, ret_new, gdn_new, conv_new)


def kernel(x_prompt, x_sample, cache_nsa_kv, page_table, state_nsa_win, state_ret, state_gdn, state_gdn_conv,
           w_in, w_out, norm_pre, norm_post, ret_gn, gdn_norm, gdn_conv, gdn_a_log, gdn_dt_bias,
           cmp_w1, cmp_b1, cmp_w2):
    Bp, T, _ = x_prompt.shape
    Bs, Ts, _ = x_sample.shape
    depth = w_in.shape[0]
    past_len = page_table.shape[1] * PAGE_SIZE
    pos_p = jnp.arange(T, dtype=jnp.int32)
    pos_s = past_len + jnp.arange(Ts, dtype=jnp.int32)
    prm = {'w_in_p': [_prep_w_in(w_in[l]) for l in range(depth)],
           'w_out_b': [w_out[l].astype(bf16) for l in range(depth)],
           'norm_pre': norm_pre, 'norm_post': norm_post, 'ret_gn': ret_gn, 'gdn_norm': gdn_norm,
           'gdn_conv': gdn_conv, 'gdn_a_log': gdn_a_log, 'gdn_dt_bias': gdn_dt_bias}
    yp, ys = x_prompt, x_sample
    cache_t = jnp.transpose(cache_nsa_kv, (0, 1, 3, 4, 5, 2)).reshape(
        depth, cache_nsa_kv.shape[1], 8 * HEAD_DIM, PAGE_SIZE)
    st_t = jnp.transpose(state_nsa_win, (0, 1, 3, 4, 5, 2)).reshape(
        depth, Bs, 4 * HEAD_DIM, state_nsa_win.shape[2])
    st_p, st_s = [], []
    for l in range(depth):
        wp, b1t, w2bd = _cmp_weights(cmp_w1[l], cmp_b1[l], cmp_w2[l])

        def nsa_prompt(q3, qr3, kv3, win3, sm3):
            kc = _compress_prompt(kv3, wp, b1t, w2bd)
            no = _nsa_prompt(q3, qr3, kc, kv3, win3, sm3)
            win_rows = win3.reshape(Bp, T, 2, NSA_KV_HEADS, HEAD_DIM)
            return no, win_rows[:, T - min(WINDOW, T):]

        def nsa_sample(q3, qr3, kv3, win3, sm3):
            no = _nsa_sample(q3, qr3, kv3, win3, sm3, cache_t, st_t, l, page_table, wp, b1t, w2bd)
            win_rows = win3.reshape(Bs, Ts, 2, NSA_KV_HEADS, HEAD_DIM)
            keys = jnp.concatenate([state_nsa_win[l], win_rows], axis=1)
            return no, keys[:, -state_nsa_win.shape[2]:]

        yp, sp = _layer(yp, pos_p, prm, l, nsa_prompt,
                        jnp.zeros((Bp, RET_HEADS, HEAD_DIM, HEAD_DIM), f32),
                        jnp.zeros((Bp, GDN_HEADS, HEAD_DIM, HEAD_DIM), f32),
                        jnp.zeros((Bp, CONV_WIDTH - 1, 3 * GDN_WIDTH), f32), PROMPT_ROW_TILE)
        ys, ss = _layer(ys, pos_s, prm, l, nsa_sample, state_ret[l], state_gdn[l], state_gdn_conv[l], Bs * Ts)
        st_p.append(sp)
        st_s.append(ss)
    stk = lambda sts, i: jnp.stack([s[i] for s in sts])
    return (yp, ys, stk(st_p, 0), stk(st_s, 0), stk(st_p, 1), stk(st_s, 1), stk(st_p, 2), stk(st_s, 2),
            stk(st_p, 3), stk(st_s, 3), stk(st_p, 4), stk(st_s, 4))
```

```python
import functools
import math

import numpy as np
import jax
import jax.numpy as jnp
from jax import lax
from jax.experimental import pallas as pl
from jax.experimental.pallas import tpu as pltpu

f32 = jnp.float32
bf16 = jnp.bfloat16

D_MODEL = 1024
HEAD_DIM = 64
RET_WIDTH = 256
NSA_WIDTH = 512
GDN_WIDTH = 256
RET_HEADS = 4
NSA_HEADS = 8
NSA_KV_HEADS = 2
NSA_GROUP = 4
GDN_HEADS = 4
CMP_BLOCK = 32
CMP_STRIDE = 16
SEL_BLOCK = 64
N_SELECT = 16
WINDOW = 512
RET_CHUNK = 64
GDN_CHUNK = 64
GDN_CHUNKS_PER_STEP = 4
CONV_WIDTH = 4
PAGE_SIZE = 128
ROPE_THETA = 10000.0
EPS = 1e-6

LANES = 128
VMEM_LIMIT = 56 * 1024 * 1024
PROMPT_ROW_TILE = 512

_SPLITS = (('ret', 4 * RET_WIDTH), ('nsa_q', NSA_WIDTH), ('nsa_kv', 6 * NSA_KV_HEADS * HEAD_DIM),
           ('nsa_gate', 3 * NSA_HEADS), ('nsa_g', NSA_WIDTH), ('gdn_qkv', 3 * GDN_WIDTH),
           ('gdn_ba', 2 * GDN_HEADS), ('gdn_g', GDN_WIDTH))
_OFF = {}
_o = 0
for _n, _w in _SPLITS:
    _OFF[_n] = (_o, _o + _w)
    _o += _w
PROJ_WIDTH = _o
_PAD_ORDER = ('ret', 'nsa_q', 'nsa_kv', 'nsa_g', 'gdn_qkv', 'gdn_g', 'nsa_gate', 'gdn_ba')
_POFF = {}
_o = 0
for _n in _PAD_ORDER:
    _w = _OFF[_n][1] - _OFF[_n][0]
    _POFF[_n] = _o
    _o += _w
SMALL_OFF = _POFF['nsa_gate']
PROJ_PAD = SMALL_OFF + LANES
GATE_COL = 0
BETA_COL = 3 * NSA_HEADS
A_COL = BETA_COL + GDN_HEADS

_NT = (((1,), (1,)), ((), ()))
_TN = (((0,), (0,)), ((), ()))
_HI = lax.Precision.HIGHEST


def _silu(x):
    return x * jax.nn.sigmoid(x)


def _cparams(sem):
    return pltpu.CompilerParams(dimension_semantics=sem, vmem_limit_bytes=VMEM_LIMIT)


def _proj_body(x_ref, g_ref, w_ref, cos_ref, sin_ref, *refs, n_t=None, n_keep=None):
    ret_ref, q_ref, qr_ref, kv_ref, win_ref, ng_ref, gq_ref, gg_ref, sm_ref = refs[-11:-2] if n_t else refs
    x = x_ref[...]
    h = x * lax.rsqrt(jnp.mean(x * x, axis=-1, keepdims=True) + EPS) * g_ref[...]
    hb = h.astype(bf16)
    cos = cos_ref[...]
    sin = sin_ref[...]
    lane = lax.broadcasted_iota(jnp.int32, cos.shape, 1)
    low = (lane % HEAD_DIM) < HEAD_DIM // 2

    def mm(c0, width):
        return jnp.dot(hb, w_ref[:, c0:c0 + width], preferred_element_type=f32)

    def rope(v):
        sw = jnp.where(low, pltpu.roll(v, LANES - HEAD_DIM // 2, 1), pltpu.roll(v, HEAD_DIM // 2, 1))
        return v * cos + sw * sin

    r = mm(_POFF['ret'], 4 * RET_WIDTH)
    for j in range(2):
        ret_ref[:, j * LANES:(j + 1) * LANES] = rope(r[:, j * LANES:(j + 1) * LANES])
    for j in range(2, 4):
        ret_ref[:, j * LANES:(j + 1) * LANES] = rope(r[:, j * LANES:(j + 1) * LANES]) * (HEAD_DIM ** -0.5)
    ret_ref[:, 2 * RET_WIDTH:] = r[:, 2 * RET_WIDTH:]

    q = mm(_POFF['nsa_q'], NSA_WIDTH)
    q_ref[...] = q
    for j in range(NSA_WIDTH // LANES):
        qr_ref[:, j * LANES:(j + 1) * LANES] = rope(q[:, j * LANES:(j + 1) * LANES])

    kv = mm(_POFF['nsa_kv'], 6 * LANES)
    kv_ref[:, 0:2 * LANES] = kv[:, 0:2 * LANES]
    kv_ref[:, 2 * LANES:3 * LANES] = rope(kv[:, 2 * LANES:3 * LANES])
    kv_ref[:, 3 * LANES:4 * LANES] = kv[:, 3 * LANES:4 * LANES]
    win_ref[:, 0:LANES] = rope(kv[:, 4 * LANES:5 * LANES])
    win_ref[:, LANES:2 * LANES] = kv[:, 5 * LANES:6 * LANES]

    ng_ref[...] = mm(_POFF['nsa_g'], NSA_WIDTH)
    gq_ref[...] = mm(_POFF['gdn_qkv'], 3 * GDN_WIDTH)
    gg_ref[...] = mm(_POFF['gdn_g'], GDN_WIDTH)
    sm_ref[...] = mm(SMALL_OFF, LANES)

    if n_t:
        kvt_ref, wint_ref = refs[-2:]
        tm = x.shape[0]
        kvt_ref[...] = kv_ref[...].T
        pos_tile = pl.program_id(0) % n_t
        for jj in range(n_keep):
            @pl.when(pos_tile == n_t - n_keep + jj)
            def _():
                wint_ref[:, jj * tm:(jj + 1) * tm] = win_ref[...].T


def _proj(x2d, gain, wpad, cosf, sinf, tm, state_t=None):
    M = x2d.shape[0]
    n_pos = cosf.shape[0] // tm
    widths = (4 * RET_WIDTH, NSA_WIDTH, NSA_WIDTH, 4 * LANES, 2 * LANES, NSA_WIDTH, 3 * GDN_WIDTH, GDN_WIDTH, LANES)
    row = lambda w: pl.BlockSpec((tm, w), lambda i: (i, 0))
    in_specs = [row(D_MODEL),
                pl.BlockSpec((1, D_MODEL), lambda i: (0, 0)),
                pl.BlockSpec((D_MODEL, PROJ_PAD), lambda i: (0, 0)),
                pl.BlockSpec((tm, LANES), lambda i: (i % n_pos, 0)),
                pl.BlockSpec((tm, LANES), lambda i: (i % n_pos, 0))]
    out_specs = [row(w) for w in widths]
    out_shape = [jax.ShapeDtypeStruct((M, w), f32) for w in widths]
    args = [x2d, gain, wpad, cosf, sinf]
    kwargs, aliases = {}, {}
    if state_t is not None:
        l, depth, B, T, stacks = state_t
        keep = min(WINDOW, T)
        assert T % tm == 0 and keep % tm == 0
        n_t = T // tm
        kwargs = dict(n_t=n_t, n_keep=keep // tm)
        out_specs += [pl.BlockSpec((None, None, 4 * LANES, tm), lambda i: (l, i // n_t, 0, i % n_t)),
                      pl.BlockSpec((None, None, 2 * LANES, keep), lambda i: (l, i // n_t, 0, 0))]
        out_shape += [jax.ShapeDtypeStruct((depth, B, 4 * LANES, T), f32),
                      jax.ShapeDtypeStruct((depth, B, 2 * LANES, keep), f32)]
        if stacks is not None:
            in_specs = [pl.BlockSpec(memory_space=pl.ANY)] * 2 + in_specs
            args = list(stacks) + args
            aliases = {0: len(widths), 1: len(widths) + 1}
    body = functools.partial(_proj_body, **kwargs)
    if aliases:
        body = functools.partial(_skip_leading, body, 2)
    return pl.pallas_call(
        body,
        grid=(M // tm,),
        in_specs=in_specs,
        out_specs=out_specs,
        out_shape=out_shape,
        input_output_aliases=aliases,
        compiler_params=_cparams(("arbitrary",)),
        name="proj",
    )(*args)


def _skip_leading(body, n, *refs, **kw):
    return body(*refs[n:], **kw)


def _ret_body(ret_ref, dm_ref, qd_ref, kd_ref, gn_ref, s0_ref, o_ref, s_ref, *, c, nchunk, sdec):
    @pl.when(pl.program_id(1) == 0)
    def _():
        s_ref[...] = s0_ref[...]

    H = range(RET_HEADS)
    hsl = lambda x, h: x[:, h * HEAD_DIM:(h + 1) * HEAD_DIM]
    Ss = [s_ref[0, h] for h in H]
    for ci in range(nchunk):
        blk = ret_ref[0, ci * c:(ci + 1) * c, :]
        q_all = blk[:, 0:RET_WIDTH].astype(bf16)
        k_all = blk[:, RET_WIDTH:2 * RET_WIDTH]
        kb_all = k_all.astype(bf16)
        kd_all = (k_all * kd_ref[...]).astype(bf16)
        v_all = blk[:, 2 * RET_WIDTH:3 * RET_WIDTH].astype(bf16)
        gate = _silu(blk[:, 3 * RET_WIDTH:])
        atts = [(lax.dot_general(hsl(q_all, h), hsl(kb_all, h), _NT, preferred_element_type=f32) * dm_ref[h]).astype(bf16)
                for h in H]
        qss = [jnp.dot(hsl(q_all, h), Ss[h].astype(bf16), preferred_element_type=f32) for h in H]
        attvs = [jnp.dot(atts[h], hsl(v_all, h), preferred_element_type=f32) for h in H]
        kvs = [lax.dot_general(hsl(kd_all, h), hsl(v_all, h), _TN, preferred_element_type=f32) for h in H]
        Ss = [Ss[h] * sdec[h] + kvs[h] for h in H]
        qd = qd_ref[...]
        outs = []
        for h in H:
            o = attvs[h] + qss[h] * hsl(qd, h)
            mu = jnp.mean(o, axis=-1, keepdims=True)
            d = o - mu
            var = jnp.mean(d * d, axis=-1, keepdims=True)
            outs.append(d * lax.rsqrt(var + EPS))
        o_ref[0, ci * c:(ci + 1) * c, :] = jnp.concatenate(outs, axis=1) * gn_ref[...] * gate
    for h in H:
        s_ref[0, h] = Ss[h]


def _retention(ret3, s0, gn, T):
    B = ret3.shape[0]
    c = math.gcd(T, RET_CHUNK)
    nchunk = min(T // c, 4)
    tb = c * nchunk
    hh = jnp.arange(RET_HEADS, dtype=f32)
    lg = jnp.log1p(-jnp.exp2(-5.0 - hh))
    ii = jnp.arange(c, dtype=f32)
    rel = ii[:, None] - ii[None, :]
    dm = jnp.exp(jnp.where(rel[None] >= 0, rel[None] * lg[:, None, None], -jnp.inf))
    qd = jnp.repeat(jnp.exp((ii + 1.0)[:, None] * lg[None, :]), HEAD_DIM, axis=1)
    kd = jnp.repeat(jnp.exp((c - 1.0 - ii)[:, None] * lg[None, :]), HEAD_DIM, axis=1)
    sdec = tuple(float((1.0 - 2.0 ** (-5.0 - h)) ** c) for h in range(RET_HEADS))
    return pl.pallas_call(
        functools.partial(_ret_body, c=c, nchunk=nchunk, sdec=sdec),
        grid=(B, T // tb),
        in_specs=[pl.BlockSpec((1, tb, 4 * RET_WIDTH), lambda b, j: (b, j, 0)),
                  pl.BlockSpec((RET_HEADS, c, c), lambda b, j: (0, 0, 0)),
                  pl.BlockSpec((c, RET_WIDTH), lambda b, j: (0, 0)),
                  pl.BlockSpec((c, RET_WIDTH), lambda b, j: (0, 0)),
                  pl.BlockSpec((1, RET_WIDTH), lambda b, j: (0, 0)),
                  pl.BlockSpec((1, RET_HEADS, HEAD_DIM, HEAD_DIM), lambda b, j: (b, 0, 0, 0))],
        out_specs=[pl.BlockSpec((1, tb, RET_WIDTH), lambda b, j: (b, j, 0)),
                   pl.BlockSpec((1, RET_HEADS, HEAD_DIM, HEAD_DIM), lambda b, j: (b, 0, 0, 0))],
        out_shape=[jax.ShapeDtypeStruct((B, T, RET_WIDTH), f32),
                   jax.ShapeDtypeStruct((B, RET_HEADS, HEAD_DIM, HEAD_DIM), f32)],
        compiler_params=_cparams(("parallel", "arbitrary")),
        name="retention",
    )(ret3, dm, qd, kd, gn, s0)


def _softplus(x):
    return jnp.maximum(x, 0.0) + jnp.log1p(jnp.exp(-jnp.abs(x)))


def _split_bf16(x):
    hi = x.astype(bf16)
    return hi, (x - hi.astype(f32)).astype(bf16)


def _dot_split(a, b):
    d = lambda x, y: jnp.dot(x, y, preferred_element_type=f32)
    return d(a[0], b[0]) + d(a[0], b[1]) + d(a[1], b[0])


def _unit_lower_inverse(n_list, c):
    rr = lax.broadcasted_iota(jnp.int32, (c, c), 0)
    cc = lax.broadcasted_iota(jnp.int32, (c, c), 1)
    eye = (rr == cc).astype(f32)
    ress = [eye - n for n in n_list]
    pws = [_split_bf16(-n) for n in n_list]
    span = 2
    while span < c:
        pws = [_split_bf16(_dot_split(pw, pw)) for pw in pws]
        ress = [res + _dot_split(_split_bf16(res), pw) for res, pw in zip(ress, pws)]
        span *= 2
    return ress


def _gdn_body(x_ref, sm_ref, gg_ref, cb_ref, cw_ref, al_ref, dtb_ref, gnorm_ref, s0_ref,
              o_ref, s_ref, xc_ref, *, c, nchunk):
    tb = c * nchunk

    @pl.when(pl.program_id(1) == 0)
    def _():
        s_ref[...] = s0_ref[...]
        xc_ref[5:8, :] = cb_ref[0]

    xc_ref[8:8 + tb, :] = x_ref[0]
    y = cw_ref[0:1, :] * xc_ref[pl.ds(5, tb), :]
    for j in range(1, CONV_WIDTH):
        y = y + cw_ref[j:j + 1, :] * xc_ref[pl.ds(5 + j, tb), :]
    tail = xc_ref[pl.ds(5 + tb, 3), :]
    xc_ref[5:8, :] = tail
    y = _silu(y)

    sm = sm_ref[0]
    beta_all = jax.nn.sigmoid(sm)
    g_all = -jnp.exp(al_ref[...]) * _softplus(sm + dtb_ref[...])
    rr = lax.broadcasted_iota(jnp.int32, (c, c), 0)
    cc = lax.broadcasted_iota(jnp.int32, (c, c), 1)
    tri = rr >= cc
    strict = rr > cc
    trif = tri.astype(f32)
    triu = (rr <= cc).astype(f32)
    gg = gg_ref[0]
    H = range(GDN_HEADS)
    P = [(ci, h) for ci in range(nchunk) for h in H]

    qs, ks, vs, betas, Gcs, Ls, kbs, kbfs, n_list = {}, {}, {}, {}, {}, {}, {}, {}, []
    for ci in range(nchunk):
        rows = slice(ci * c, (ci + 1) * c)
        gch = g_all[rows]
        G_cols = jnp.dot(trif, gch, preferred_element_type=f32, precision=_HI)
        G_rows = jnp.dot(gch.T, triu, preferred_element_type=f32, precision=_HI)
        for h in H:
            p = (ci, h)
            hs = lambda base: y[rows, base + h * HEAD_DIM: base + (h + 1) * HEAD_DIM]
            q, k = hs(0), hs(GDN_WIDTH)
            qs[p] = q * lax.rsqrt(jnp.sum(q * q, axis=-1, keepdims=True) + EPS) * (HEAD_DIM ** -0.5)
            k = k * lax.rsqrt(jnp.sum(k * k, axis=-1, keepdims=True) + EPS)
            ks[p] = k
            vs[p] = hs(2 * GDN_WIDTH)
            betas[p] = beta_all[rows, BETA_COL + h: BETA_COL + h + 1]
            Gcs[p] = G_cols[:, A_COL + h: A_COL + h + 1]
            Gr = G_rows[A_COL + h: A_COL + h + 1, :]
            Ls[p] = jnp.exp(jnp.where(tri, Gcs[p] - Gr, -jnp.inf))
            kbs[p] = k * betas[p]
            kbfs[p] = k.astype(bf16)
            n_list.append(jnp.where(strict, lax.dot_general(kbs[p].astype(bf16), kbfs[p], _NT,
                                                            preferred_element_type=f32) * Ls[p], 0.0))
    tinvs = dict(zip(P, [t.astype(bf16) for t in _unit_lower_inverse(n_list, c)]))
    egs = {p: jnp.exp(Gcs[p]) for p in P}
    us = {p: jnp.dot(tinvs[p], (vs[p] * betas[p]).astype(bf16), preferred_element_type=f32) for p in P}
    ws = {p: jnp.dot(tinvs[p], (kbs[p] * egs[p]).astype(bf16), preferred_element_type=f32).astype(bf16) for p in P}
    qks = {p: (lax.dot_general(qs[p].astype(bf16), kbfs[p], _NT, preferred_element_type=f32) * Ls[p]).astype(bf16) for p in P}
    qes = {p: (qs[p] * egs[p]).astype(bf16) for p in P}

    Ss = [s_ref[0, h] for h in H]
    for ci in range(nchunk):
        Sbs = [S.astype(bf16) for S in Ss]
        qss = [jnp.dot(qes[ci, h], Sbs[h], preferred_element_type=f32) for h in H]
        vnbs = [(us[ci, h] - jnp.dot(ws[ci, h], Sbs[h], preferred_element_type=f32)).astype(bf16) for h in H]
        os_ = [qss[h] + jnp.dot(qks[ci, h], vnbs[h], preferred_element_type=f32) for h in H]
        new_s = []
        for h in H:
            g_last = Gcs[ci, h][c - 1:c, :]
            kdec = (ks[ci, h] * jnp.exp(g_last - Gcs[ci, h])).astype(bf16)
            new_s.append(Ss[h] * jnp.exp(g_last) + lax.dot_general(kdec, vnbs[h], _TN, preferred_element_type=f32))
        Ss = new_s
        outs = []
        for h in H:
            o = os_[h]
            yo = o * lax.rsqrt(jnp.mean(o * o, axis=-1, keepdims=True) + EPS) * gnorm_ref[...]
            outs.append(yo * _silu(gg[ci * c:(ci + 1) * c, h * HEAD_DIM:(h + 1) * HEAD_DIM]))
        o_ref[0, ci * c:(ci + 1) * c, :] = jnp.concatenate(outs, axis=1)
    for h in H:
        s_ref[0, h] = Ss[h]


def _gdn(gq3, sm3, gg3, conv_buf, conv_w, a_log, dt_bias, gnorm, s0, T):
    B = gq3.shape[0]
    c = math.gcd(T, GDN_CHUNK)
    nchunk = min(T // c, GDN_CHUNKS_PER_STEP)
    tb = c * nchunk
    C = 3 * GDN_WIDTH
    pad = lambda v: jnp.zeros((1, LANES), f32).at[0, A_COL:A_COL + GDN_HEADS].set(v.astype(f32))
    return pl.pallas_call(
        functools.partial(_gdn_body, c=c, nchunk=nchunk),
        grid=(B, T // tb),
        in_specs=[pl.BlockSpec((1, tb, C), lambda b, j: (b, j, 0)),
                  pl.BlockSpec((1, tb, LANES), lambda b, j: (b, j, 0)),
                  pl.BlockSpec((1, tb, GDN_WIDTH), lambda b, j: (b, j, 0)),
                  pl.BlockSpec((1, CONV_WIDTH - 1, C), lambda b, j: (b, 0, 0)),
                  pl.BlockSpec((CONV_WIDTH, C), lambda b, j: (0, 0)),
                  pl.BlockSpec((1, LANES), lambda b, j: (0, 0)),
                  pl.BlockSpec((1, LANES), lambda b, j: (0, 0)),
                  pl.BlockSpec((1, HEAD_DIM), lambda b, j: (0, 0)),
                  pl.BlockSpec((1, GDN_HEADS, HEAD_DIM, HEAD_DIM), lambda b, j: (b, 0, 0, 0))],
        out_specs=[pl.BlockSpec((1, tb, GDN_WIDTH), lambda b, j: (b, j, 0)),
                   pl.BlockSpec((1, GDN_HEADS, HEAD_DIM, HEAD_DIM), lambda b, j: (b, 0, 0, 0))],
        out_shape=[jax.ShapeDtypeStruct((B, T, GDN_WIDTH), f32),
                   jax.ShapeDtypeStruct((B, GDN_HEADS, HEAD_DIM, HEAD_DIM), f32)],
        scratch_shapes=[pltpu.VMEM((8 + tb + 8, C), f32)],
        compiler_params=_cparams(("parallel", "arbitrary")),
        name="gdn",
    )(gq3, sm3, gg3, conv_buf, conv_w, pad(a_log), pad(dt_bias), gnorm, s0)


def _cmp_weights(w1, b1, w2):
    nq = 2 * NSA_KV_HEADS
    kind = np.arange(nq) // NSA_KV_HEADS
    eye = jnp.eye(nq, dtype=f32)
    r = CMP_BLOCK // CMP_STRIDE
    w = w1[kind].reshape(nq, r, CMP_STRIDE, HEAD_DIM, HEAD_DIM)
    w = jnp.transpose(w, (2, 0, 3, 1, 4))
    wc = (w[:, :, :, :, None, :] * eye[None, :, None, None, :, None])
    wc = wc.reshape(CMP_STRIDE, nq * HEAD_DIM, r * nq * HEAD_DIM).astype(bf16)
    b1t = b1[kind].reshape(1, nq * HEAD_DIM)
    w2bd = (w2[kind][:, :, None, :] * eye[:, None, :, None]).reshape(nq * HEAD_DIM, nq * HEAD_DIM).astype(bf16)
    w7 = wc.reshape(CMP_STRIDE // 2, 2, 2, LANES, r, 2, LANES)
    wp = jnp.stack([w7[:, :, pr, :, :, pr, :].reshape(CMP_STRIDE // 2, 2 * LANES, r * LANES) for pr in range(2)])
    return wp, b1t, w2bd


def _segment_perm():
    seg = PAGE_SIZE // CMP_STRIDE
    off = np.arange(PAGE_SIZE)
    return jnp.asarray(((off[None, :] % CMP_STRIDE) * seg + off[None, :] // CMP_STRIDE == off[:, None]).astype(np.float32)).astype(bf16)


def _first_layer(xt_ref, wp_ref, n_row):
    accs = [jnp.zeros((n_row, 4 * HEAD_DIM), f32) for _ in range(2)]
    for s2 in range(CMP_STRIDE // 2):
        x0 = xt_ref[2 * s2]
        x1 = xt_ref[2 * s2 + 1]
        for pr in range(2):
            xs = jnp.concatenate([x0[:, pr * LANES:(pr + 1) * LANES], x1[:, pr * LANES:(pr + 1) * LANES]], axis=1)
            accs[pr] = accs[pr] + jnp.dot(xs, wp_ref[pr, s2], preferred_element_type=f32)
    return jnp.concatenate([accs[0][:, :LANES], accs[1][:, :LANES], accs[0][:, LANES:], accs[1][:, LANES:]], axis=1)


def _cmp_prompt_body(x_ref, perm_ref, wp_ref, b1_ref, w2_ref, o_ref, xt_ref, *, n_seg):
    seg = PAGE_SIZE // CMP_STRIDE
    for i in range(n_seg // seg):
        xp = jnp.dot(perm_ref[...], x_ref[0, i * PAGE_SIZE:(i + 1) * PAGE_SIZE, :].astype(bf16), preferred_element_type=f32)
        for s in range(CMP_STRIDE):
            xt_ref[s, i * seg:(i + 1) * seg, :] = xp[s * seg:(s + 1) * seg, :].astype(bf16)
    acc = _first_layer(xt_ref, wp_ref, n_seg)
    first = acc[:, :4 * HEAD_DIM]
    second = pltpu.roll(acc[:, 4 * HEAD_DIM:], n_seg - 1, 0)
    hid = _silu(b1_ref[...] + first + second)
    out = jnp.dot(hid.astype(bf16), w2_ref[...], preferred_element_type=f32)
    for qi in range(2 * NSA_KV_HEADS):
        o_ref[0, qi] = out[:, qi * HEAD_DIM:(qi + 1) * HEAD_DIM]


def _compress_prompt(kv3, wp, b1t, w2bd):
    B, T, _ = kv3.shape
    assert T % PAGE_SIZE == 0
    n_seg = T // CMP_STRIDE
    return pl.pallas_call(
        functools.partial(_cmp_prompt_body, n_seg=n_seg),
        grid=(B,),
        in_specs=[pl.BlockSpec((1, T, 4 * HEAD_DIM), lambda b: (b, 0, 0)),
                  pl.BlockSpec((PAGE_SIZE, PAGE_SIZE), lambda b: (0, 0)),
                  pl.BlockSpec((2, CMP_STRIDE // 2, 4 * HEAD_DIM, 4 * HEAD_DIM), lambda b: (0, 0, 0, 0)),
                  pl.BlockSpec((1, 4 * HEAD_DIM), lambda b: (0, 0)),
                  pl.BlockSpec((4 * HEAD_DIM, 4 * HEAD_DIM), lambda b: (0, 0))],
        out_specs=pl.BlockSpec((1, 4, n_seg, HEAD_DIM), lambda b: (b, 0, 0, 0)),
        out_shape=jax.ShapeDtypeStruct((B, 4, n_seg, HEAD_DIM), f32),
        scratch_shapes=[pltpu.VMEM((CMP_STRIDE, n_seg, 4 * HEAD_DIM), bf16)],
        compiler_params=_cparams(("parallel",)),
        name="compress_prompt",
    )(kv3, _segment_perm(), wp, b1t, w2bd)


def _topk_rows(score, k):
    n = score.shape[0]
    idx = lax.broadcasted_iota(jnp.int32, score.shape, 0)
    taken = jnp.zeros(score.shape, jnp.bool_)
    for _ in range(k):
        work = jnp.where(taken, -jnp.inf, score)
        m = jnp.max(work, axis=0, keepdims=True)
        cand = jnp.where((work == m) & jnp.logical_not(taken), idx, n)
        first = jnp.min(cand, axis=0, keepdims=True)
        taken = taken | (idx == first)
    return jnp.where(taken, 1.0, 0.0)


def _topk_rows_by_rank(score, k):
    n = score.shape[0]
    idx = lax.broadcasted_iota(jnp.int32, score.shape, 0)
    rank = jnp.zeros(score.shape, jnp.int32)
    for i in range(n):
        row = score[i:i + 1, :]
        ahead = (row > score) | ((row == score) & (idx > i))
        rank = rank + ahead.astype(jnp.int32)
    return jnp.where(rank < k, 1.0, 0.0)


def _softmax_rows(s, mask):
    s = jnp.where(mask, s, -jnp.inf)
    m = jnp.max(s, axis=-1, keepdims=True)
    m = jnp.where(m == -jnp.inf, 0.0, m)
    p = jnp.exp(s - m)
    return p / jnp.maximum(jnp.sum(p, axis=-1, keepdims=True), 1e-30)


def _attend_tile(qs, kbs, vbs, biases, carries, tq):
    n = range(len(qs))
    R = qs[0].shape[0]
    tk = kbs[0].shape[0]
    ss = [lax.dot_general(qs[i], kbs[i], _NT, preferred_element_type=f32) for i in n]
    ss = [(ss[i].reshape(NSA_GROUP, tq, tk) + biases[i][None]).reshape(R, tk) for i in n]
    m_news = [jnp.maximum(carries[i][0], jnp.max(ss[i], axis=-1, keepdims=True)) for i in n]
    m_safes = [jnp.where(m == -jnp.inf, 0.0, m) for m in m_news]
    alphas = [jnp.exp(carries[i][0] - m_safes[i]) for i in n]
    ps = [jnp.exp(ss[i] - m_safes[i]) for i in n]
    ls = [alphas[i] * carries[i][1] + jnp.sum(ps[i], axis=-1, keepdims=True) for i in n]
    accs = [alphas[i] * carries[i][2] + jnp.dot(ps[i].astype(bf16), vbs[i], preferred_element_type=f32) for i in n]
    return [(m_news[i], ls[i], accs[i]) for i in n]


def _flash_init(R):
    return (jnp.full((R, 1), -jnp.inf, f32), jnp.zeros((R, 1), f32), jnp.zeros((R, HEAD_DIM), f32))


def _nsa_prompt_body(q_ref, qr_ref, kc_ref, ks_ref, vs_ref, kw_ref, vw_ref, sm_ref, ov_ref, ex_ref,
                     o_ref, wbias_ref, *, tq, tk, tkw, n_wt, T, n_cmp, n_blk):
    i = pl.program_id(1)
    t0 = i * tq
    scale = HEAD_DIM ** -0.5
    G = range(NSA_KV_HEADS)
    R = NSA_GROUP * tq
    gates = jax.nn.sigmoid(sm_ref[0])
    n_pad = kc_ref.shape[2]
    tpos_r = t0 + lax.broadcasted_iota(jnp.int32, (tq, 1), 0)
    tpos_c = t0 + lax.broadcasted_iota(jnp.int32, (1, tq), 1)
    eye = (lax.broadcasted_iota(jnp.int32, (tq, tq), 0) == lax.broadcasted_iota(jnp.int32, (tq, tq), 1)).astype(bf16)
    heads = lambda x, g: [(x[:, (g * NSA_GROUP + r) * HEAD_DIM:(g * NSA_GROUP + r + 1) * HEAD_DIM] * scale).astype(bf16)
                          for r in range(NSA_GROUP)]

    o_cmps, imps = [], []
    nn = lax.broadcasted_iota(jnp.int32, (1, n_pad), 1)
    cmask = (nn * CMP_STRIDE + (CMP_BLOCK - 1) <= tpos_r) & (nn < n_cmp)
    HH = [(g, r) for g in G for r in range(NSA_GROUP)]
    qhs = {g: heads(q_ref[0], g) for g in G}
    kcs = {g: kc_ref[0, g].astype(bf16) for g in G}
    vcs = {g: kc_ref[0, NSA_KV_HEADS + g].astype(bf16) for g in G}
    ss = {h: jnp.where(cmask, lax.dot_general(qhs[h[0]][h[1]], kcs[h[0]], _NT, preferred_element_type=f32), -jnp.inf)
          for h in HH}
    ms = {h: jnp.max(ss[h], axis=-1, keepdims=True) for h in HH}
    es = {h: jnp.exp(ss[h] - jnp.where(ms[h] == -jnp.inf, 0.0, ms[h])) for h in HH}
    ps = {h: es[h] / jnp.maximum(jnp.sum(es[h], axis=-1, keepdims=True), 1e-30) for h in HH}
    ocs = {h: jnp.dot(ps[h].astype(bf16), vcs[h[0]], preferred_element_type=f32) for h in HH}
    for g in G:
        o_cmps.append([ocs[g, r] for r in range(NSA_GROUP)])
        imps.append(ps[g, 0] + ps[g, 1] + ps[g, 2] + ps[g, 3])

    score = jnp.concatenate([lax.dot_general(ov_ref[...], imps[g], _NT, preferred_element_type=f32, precision=_HI)[:n_blk]
                             for g in G], axis=1)
    jj = lax.broadcasted_iota(jnp.int32, score.shape, 0)
    jt = jnp.concatenate([tpos_c] * NSA_KV_HEADS, axis=1) // SEL_BLOCK
    valid = jj <= jt
    forced = valid & ((jj == 0) | (jj == jt) | (jj == jt - 1))
    score = jnp.where(forced, jnp.inf, jnp.where(valid, score, -jnp.inf))
    sel_t = _topk_rows_by_rank(score, min(N_SELECT, n_blk)).astype(bf16)
    if n_blk < LANES:
        sel_t = jnp.concatenate([sel_t, jnp.zeros((LANES - n_blk, NSA_KV_HEADS * tq), bf16)], axis=0)
    sels = [lax.dot_general(eye, sel_t[:, g * tq:(g + 1) * tq], _NT, preferred_element_type=f32).astype(bf16)
            for g in G]

    w0 = pl.multiple_of(jnp.clip(t0 - WINDOW, 0, T - n_wt * tkw), tq)
    for j in range(n_wt):
        dist = tpos_r - (w0 + j * tkw + lax.broadcasted_iota(jnp.int32, (1, tkw), 1))
        wbias_ref[j] = jnp.where((dist >= 0) & (dist < WINDOW), 0.0, -jnp.inf)

    stack = lambda x, g: (jnp.concatenate([x[:, (g * NSA_GROUP + r) * HEAD_DIM:(g * NSA_GROUP + r + 1) * HEAD_DIM]
                                           for r in range(NSA_GROUP)], axis=0) * scale).astype(bf16)
    qrs = [stack(qr_ref[0], g) for g in G]
    kv_tile = lambda ref, k0, n, g: ref[0, pl.ds(k0, n), g * HEAD_DIM:(g + 1) * HEAD_DIM].astype(bf16)

    def sel_step(kt, carries):
        k0 = pl.multiple_of(kt * tk, tk)
        causal = k0 + lax.broadcasted_iota(jnp.int32, (1, tk), 1) <= tpos_r
        biases = [jnp.where((jnp.dot(sels[g], ex_ref[kt], preferred_element_type=f32) > 0.5) & causal, 0.0, -jnp.inf)
                  for g in G]
        return tuple(_attend_tile(qrs, [kv_tile(ks_ref, k0, tk, g) for g in G], [kv_tile(vs_ref, k0, tk, g) for g in G],
                                  biases, carries, tq))

    sel = lax.fori_loop(0, (t0 + tq + tk - 1) // tk, sel_step, tuple(_flash_init(R) for _ in G))

    win = [_flash_init(R) for _ in G]
    for j in range(n_wt):
        k0 = pl.multiple_of(w0 + j * tkw, tq)
        win = _attend_tile(qrs, [kv_tile(kw_ref, k0, tkw, g) for g in G], [kv_tile(vw_ref, k0, tkw, g) for g in G],
                           [wbias_ref[j]] * NSA_KV_HEADS, win, tq)

    for g in G:
        o_sel = sel[g][2] / jnp.maximum(sel[g][1], 1e-30)
        o_win = win[g][2] / jnp.maximum(win[g][1], 1e-30)
        outs = []
        for r in range(NSA_GROUP):
            hcol = GATE_COL + g * NSA_GROUP + r
            rows = slice(r * tq, (r + 1) * tq)
            outs.append(gates[:, hcol:hcol + 1] * o_cmps[g][r]
                        + gates[:, NSA_HEADS + hcol:NSA_HEADS + hcol + 1] * o_sel[rows]
                        + gates[:, 2 * NSA_HEADS + hcol:2 * NSA_HEADS + hcol + 1] * o_win[rows])
        o_ref[0, :, g * NSA_GROUP * HEAD_DIM:(g + 1) * NSA_GROUP * HEAD_DIM] = jnp.concatenate(outs, axis=1)


def _overlap_t(n_blk_pad, n_cmp, n_pad):
    ci = np.arange(n_pad)[None, :] * CMP_STRIDE
    sj = np.arange(n_blk_pad)[:, None] * SEL_BLOCK
    ov = (ci < sj + SEL_BLOCK) & (ci + CMP_BLOCK > sj) & (np.arange(n_pad)[None, :] < n_cmp)
    return jnp.asarray(ov.astype(np.float32))


def _nsa_prompt(q3, qr3, kc, kv3, win3, sm3):
    B, T, _ = q3.shape
    tq = 128
    tk = 512 if T >= 1024 else 256
    tkw = WINDOW + tq if T >= 1024 else 128
    n_wt = -(-min(T, WINDOW + tq) // tkw)
    assert T % tk == 0 and n_wt * tkw <= T
    n_seg = T // CMP_STRIDE
    n_cmp = n_seg - CMP_BLOCK // CMP_STRIDE + 1
    n_blk = -(-T // SEL_BLOCK)
    ov = _overlap_t(LANES, n_cmp, n_seg)
    key_blk = (np.arange(T // tk)[:, None, None] * tk + np.arange(tk)[None, None, :]) // SEL_BLOCK
    ex = jnp.asarray((np.arange(LANES)[None, :, None] == key_blk).astype(np.float32)).astype(bf16)
    qspec = pl.BlockSpec((1, tq, NSA_WIDTH), lambda b, i: (b, i, 0))
    col = lambda j: pl.BlockSpec((1, T, LANES), lambda b, i: (b, 0, j))
    return pl.pallas_call(
        functools.partial(_nsa_prompt_body, tq=tq, tk=tk, tkw=tkw, n_wt=n_wt, T=T, n_cmp=n_cmp, n_blk=n_blk),
        grid=(B, T // tq),
        in_specs=[qspec, qspec,
                  pl.BlockSpec((1, 4, n_seg, HEAD_DIM), lambda b, i: (b, 0, 0, 0)),
                  col(2), col(3), col(0), col(1),
                  pl.BlockSpec((1, tq, LANES), lambda b, i: (b, i, 0)),
                  pl.BlockSpec((LANES, n_seg), lambda b, i: (0, 0)),
                  pl.BlockSpec((T // tk, LANES, tk), lambda b, i: (0, 0, 0))],
        out_specs=qspec,
        out_shape=jax.ShapeDtypeStruct((B, T, NSA_WIDTH), f32),
        scratch_shapes=[pltpu.VMEM((n_wt, tq, tkw), f32)],
        compiler_params=_cparams(("parallel", "arbitrary")),
        name="nsa_prompt",
    )(q3, qr3, kc, kv3, kv3, win3, win3, sm3, ov, ex)


def _out_body(x_ref, ro_ref, no_ref, ng_ref, go_ref, w_ref, g_ref, y_ref):
    no = no_ref[...] * _silu(ng_ref[...])
    z = (jnp.dot(ro_ref[...].astype(bf16), w_ref[0:RET_WIDTH, :], preferred_element_type=f32)
         + jnp.dot(no.astype(bf16), w_ref[RET_WIDTH:RET_WIDTH + NSA_WIDTH, :], preferred_element_type=f32)
         + jnp.dot(go_ref[...].astype(bf16), w_ref[RET_WIDTH + NSA_WIDTH:, :], preferred_element_type=f32))
    y_ref[...] = x_ref[...] + z * lax.rsqrt(jnp.mean(z * z, axis=-1, keepdims=True) + EPS) * g_ref[...]


def _out(x2d, ro, no, ng, go, w_out_b, gain, tm):
    M = x2d.shape[0]
    row = lambda w: pl.BlockSpec((tm, w), lambda i: (i, 0))
    return pl.pallas_call(
        _out_body,
        grid=(M // tm,),
        in_specs=[row(D_MODEL), row(RET_WIDTH), row(NSA_WIDTH), row(NSA_WIDTH), row(GDN_WIDTH),
                  pl.BlockSpec((D_MODEL, D_MODEL), lambda i: (0, 0)),
                  pl.BlockSpec((1, D_MODEL), lambda i: (0, 0))],
        out_specs=row(D_MODEL),
        out_shape=jax.ShapeDtypeStruct((M, D_MODEL), f32),
        compiler_params=_cparams(("parallel",)),
        name="out",
    )(x2d, ro, no, ng, go, w_out_b, gain)


PAGES_PER_STEP = 64


def _page_specs(l, pg, half):
    return [pl.BlockSpec((None, None, 4 * HEAD_DIM, PAGE_SIZE),
                         lambda b, j, pt, i=i: (l, pt[b, j * pg + i], half, 0)) for i in range(pg)]


def _cmp_sample_body(pt_ref, *refs, pg):
    page_refs, perm_ref, wp_ref, o_ref, xt_ref = refs[:pg], refs[pg], refs[pg + 1], refs[pg + 2], refs[pg + 3]
    seg = PAGE_SIZE // CMP_STRIDE
    for i in range(pg):
        xp = lax.dot_general(perm_ref[...], page_refs[i][...].astype(bf16), _NT, preferred_element_type=f32)
        for s in range(CMP_STRIDE):
            xt_ref[s, i * seg:(i + 1) * seg, :] = xp[s * seg:(s + 1) * seg, :].astype(bf16)
    o_ref[0] = _first_layer(xt_ref, wp_ref, pg * seg)


def _cmp_sample(cache_t, l, page_table, wp):
    B, n_pages = page_table.shape
    pg = math.gcd(n_pages, PAGES_PER_STEP)
    n_row = pg * PAGE_SIZE // CMP_STRIDE
    return pl.pallas_call(
        functools.partial(_cmp_sample_body, pg=pg),
        grid_spec=pltpu.PrefetchScalarGridSpec(
            num_scalar_prefetch=1,
            grid=(B, n_pages // pg),
            in_specs=_page_specs(l, pg, 0) + [pl.BlockSpec((PAGE_SIZE, PAGE_SIZE), lambda b, j, pt: (0, 0)),
                                              pl.BlockSpec((2, CMP_STRIDE // 2, 4 * HEAD_DIM, 4 * HEAD_DIM),
                                                           lambda b, j, pt: (0, 0, 0, 0))],
            out_specs=pl.BlockSpec((1, n_row, 8 * HEAD_DIM), lambda b, j, pt: (b, j, 0)),
            scratch_shapes=[pltpu.VMEM((CMP_STRIDE, n_row, 4 * HEAD_DIM), bf16)]),
        out_shape=jax.ShapeDtypeStruct((B, n_pages * PAGE_SIZE // CMP_STRIDE, 8 * HEAD_DIM), f32),
        compiler_params=_cparams(("parallel", "arbitrary")),
        name="compress_sample",
    )(page_table, *([cache_t] * pg), _segment_perm(), wp)


def _stack_heads(x, g):
    return jnp.concatenate([x[:, (g * NSA_GROUP + r) * HEAD_DIM:(g * NSA_GROUP + r + 1) * HEAD_DIM]
                            for r in range(NSA_GROUP)], axis=0)


def _nsa_sample_mid_body(p_ref, b1_ref, w2_ref, q_ref, qr_ref, win_ref, st_ref, ov_ref,
                         ocmp_ref, owin_ref, selt_ref, *, Ts, n_cmp, past_len, wb):
    P = p_ref[0]
    n_seg = P.shape[0]
    hid = _silu(b1_ref[...] + P[:, :4 * HEAD_DIM] + pltpu.roll(P[:, 4 * HEAD_DIM:], n_seg - 1, 0))
    kcv = jnp.dot(hid.astype(bf16), w2_ref[...], preferred_element_type=f32)
    scale = HEAD_DIM ** -0.5
    R = NSA_GROUP * Ts
    trow = lax.broadcasted_iota(jnp.int32, (R, 1), 0) % Ts
    imps = []
    for g in range(NSA_KV_HEADS):
        qs = (_stack_heads(q_ref[0], g) * scale).astype(bf16)
        kc = kcv[:, g * HEAD_DIM:(g + 1) * HEAD_DIM].astype(bf16)
        vc = kcv[:, (NSA_KV_HEADS + g) * HEAD_DIM:(NSA_KV_HEADS + g + 1) * HEAD_DIM].astype(bf16)
        s = lax.dot_general(qs, kc, _NT, preferred_element_type=f32)
        nn = lax.broadcasted_iota(jnp.int32, (1, n_seg), 1)
        cmask = (nn * CMP_STRIDE + (CMP_BLOCK - 1) <= past_len + trow) & (nn < n_cmp)
        p = _softmax_rows(s, cmask)
        ocmp_ref[0, g] = jnp.dot(p.astype(bf16), vc, preferred_element_type=f32)
        imps.append(jnp.sum(p.reshape(NSA_GROUP, Ts, n_seg), axis=0))
        qrs = (_stack_heads(qr_ref[0], g) * scale).astype(bf16)
        kwt = st_ref[g * HEAD_DIM:(g + 1) * HEAD_DIM, :].astype(bf16)
        vwt = st_ref[(NSA_KV_HEADS + g) * HEAD_DIM:(NSA_KV_HEADS + g + 1) * HEAD_DIM, :].astype(bf16)
        wn = win_ref[0]
        knew = wn[:, g * HEAD_DIM:(g + 1) * HEAD_DIM].astype(bf16)
        vnew = wn[:, (NSA_KV_HEADS + g) * HEAD_DIM:(NSA_KV_HEADS + g + 1) * HEAD_DIM].astype(bf16)
        s_b = jnp.dot(qrs, kwt, preferred_element_type=f32)
        s_n = lax.dot_general(qrs, knew, _NT, preferred_element_type=f32)
        jb = lax.broadcasted_iota(jnp.int32, (1, wb), 1)
        dist_b = trow + wb - jb
        s_b = jnp.where((dist_b >= 0) & (dist_b < WINDOW) & (past_len - wb + jb >= 0), s_b, -jnp.inf)
        dist_n = trow - lax.broadcasted_iota(jnp.int32, (1, Ts), 1)
        s_n = jnp.where((dist_n >= 0) & (dist_n < WINDOW), s_n, -jnp.inf)
        m = jnp.maximum(jnp.max(s_b, axis=-1, keepdims=True), jnp.max(s_n, axis=-1, keepdims=True))
        m = jnp.where(m == -jnp.inf, 0.0, m)
        pb = jnp.exp(s_b - m)
        pn = jnp.exp(s_n - m)
        den = jnp.sum(pb, axis=-1, keepdims=True) + jnp.sum(pn, axis=-1, keepdims=True)
        o = (lax.dot_general(pb.astype(bf16), vwt, _NT, preferred_element_type=f32)
             + jnp.dot(pn.astype(bf16), vnew, preferred_element_type=f32))
        owin_ref[0, g] = o / jnp.maximum(den, 1e-30)

    imp2 = jnp.concatenate(imps, axis=0)
    score = lax.dot_general(ov_ref[...], imp2, _NT, preferred_element_type=f32, precision=_HI)
    jj = lax.broadcasted_iota(jnp.int32, score.shape, 0)
    jt = (past_len + lax.broadcasted_iota(jnp.int32, score.shape, 1) % Ts) // SEL_BLOCK
    valid = jj <= jt
    forced = valid & ((jj == 0) | (jj == jt) | (jj == jt - 1))
    score = jnp.where(forced, jnp.inf, jnp.where(valid, score, -jnp.inf))
    selt_ref[0] = _topk_rows(score, N_SELECT)


def _nsa_sample_mid(P, b1t, w2bd, q3, qr3, win3, st_t, l, past_len):
    B, Ts, _ = q3.shape
    n_seg = P.shape[1]
    n_cmp = n_seg - CMP_BLOCK // CMP_STRIDE + 1
    n_blk = -(-(past_len + Ts) // SEL_BLOCK)
    nb_pad = -(-n_blk // 8) * 8
    wb = st_t.shape[-1]
    ov = _overlap_t(nb_pad, n_cmp, n_seg)
    R = NSA_GROUP * Ts
    tok = lambda w: pl.BlockSpec((1, Ts, w), lambda b: (b, 0, 0))
    return pl.pallas_call(
        functools.partial(_nsa_sample_mid_body, Ts=Ts, n_cmp=n_cmp, past_len=past_len, wb=wb),
        grid=(B,),
        in_specs=[pl.BlockSpec((1, n_seg, 8 * HEAD_DIM), lambda b: (b, 0, 0)),
                  pl.BlockSpec((1, 4 * HEAD_DIM), lambda b: (0, 0)),
                  pl.BlockSpec((4 * HEAD_DIM, 4 * HEAD_DIM), lambda b: (0, 0)),
                  tok(NSA_WIDTH), tok(NSA_WIDTH), tok(4 * HEAD_DIM),
                  pl.BlockSpec((None, None, 4 * HEAD_DIM, wb), lambda b: (l, b, 0, 0)),
                  pl.BlockSpec((nb_pad, n_seg), lambda b: (0, 0))],
        out_specs=[pl.BlockSpec((1, NSA_KV_HEADS, R, HEAD_DIM), lambda b: (b, 0, 0, 0)),
                   pl.BlockSpec((1, NSA_KV_HEADS, R, HEAD_DIM), lambda b: (b, 0, 0, 0)),
                   pl.BlockSpec((1, nb_pad, NSA_KV_HEADS * Ts), lambda b: (b, 0, 0))],
        out_shape=[jax.ShapeDtypeStruct((B, NSA_KV_HEADS, R, HEAD_DIM), f32),
                   jax.ShapeDtypeStruct((B, NSA_KV_HEADS, R, HEAD_DIM), f32),
                   jax.ShapeDtypeStruct((B, nb_pad, NSA_KV_HEADS * Ts), f32)],
        compiler_params=_cparams(("parallel",)),
        name="nsa_sample_mid",
    )(P, b1t, w2bd, q3, qr3, win3, st_t, ov)


def _nsa_sample_sel_body(pt_ref, *refs, pg, Ts, n_steps):
    page_refs = refs[:pg]
    qr_ref, kvn_ref, selt_ref, ex_ref, ocmp_ref, owin_ref, sm_ref, o_ref, m_ref, l_ref, acc_ref = refs[pg:]
    j = pl.program_id(1)
    scale = HEAD_DIM ** -0.5
    R = NSA_GROUP * Ts
    GT = NSA_KV_HEADS * Ts
    nb = 2 * pg
    trow = lax.broadcasted_iota(jnp.int32, (R, 1), 0) % Ts
    eye = (lax.broadcasted_iota(jnp.int32, (GT, GT), 0) == lax.broadcasted_iota(jnp.int32, (GT, GT), 1)).astype(bf16)
    sel_j = selt_ref[0, pl.ds(pl.multiple_of(j * nb, nb), nb), :].astype(bf16)
    if nb < LANES:
        sel_j = jnp.concatenate([sel_j, jnp.zeros((LANES - nb, GT), bf16)], axis=0)
    sel_rows = lax.dot_general(eye, sel_j, _NT, preferred_element_type=f32)
    full = jnp.dot(sel_rows.astype(bf16), ex_ref[...], preferred_element_type=f32)

    G = range(NSA_KV_HEADS)
    qrs = [(_stack_heads(qr_ref[0], g) * scale).astype(bf16) for g in G]

    @pl.when(j == 0)
    def _():
        kvn = kvn_ref[0]
        for g in G:
            knew = kvn[:, (2 * NSA_KV_HEADS + g) * HEAD_DIM:(2 * NSA_KV_HEADS + g + 1) * HEAD_DIM].astype(bf16)
            vnew = kvn[:, (3 * NSA_KV_HEADS + g) * HEAD_DIM:(3 * NSA_KV_HEADS + g + 1) * HEAD_DIM].astype(bf16)
            s_n = lax.dot_general(qrs[g], knew, _NT, preferred_element_type=f32)
            s_n = jnp.where(lax.broadcasted_iota(jnp.int32, (1, Ts), 1) <= trow, s_n, -jnp.inf)
            m0 = jnp.max(s_n, axis=-1, keepdims=True)
            p0 = jnp.exp(s_n - m0)
            m_ref[g] = m0
            l_ref[g] = jnp.sum(p0, axis=-1, keepdims=True)
            acc_ref[g] = jnp.dot(p0.astype(bf16), vnew, preferred_element_type=f32)

    kts = [jnp.concatenate([page_refs[i][g * HEAD_DIM:(g + 1) * HEAD_DIM, :] for i in range(pg)], axis=1).astype(bf16)
           for g in G]
    vts = [jnp.concatenate([page_refs[i][(NSA_KV_HEADS + g) * HEAD_DIM:(NSA_KV_HEADS + g + 1) * HEAD_DIM, :]
                            for i in range(pg)], axis=1).astype(bf16) for g in G]
    ss = [jnp.dot(qrs[g], kts[g], preferred_element_type=f32) for g in G]
    ss = [jnp.where(jnp.concatenate([full[g * Ts:(g + 1) * Ts]] * NSA_GROUP, axis=0) > 0.5, ss[g], -jnp.inf) for g in G]
    m_olds = [m_ref[g] for g in G]
    m_news = [jnp.maximum(m_olds[g], jnp.max(ss[g], axis=-1, keepdims=True)) for g in G]
    alphas = [jnp.exp(m_olds[g] - m_news[g]) for g in G]
    ps = [jnp.exp(ss[g] - m_news[g]) for g in G]
    pvs = [lax.dot_general(ps[g].astype(bf16), vts[g], _NT, preferred_element_type=f32) for g in G]
    for g in G:
        l_ref[g] = alphas[g] * l_ref[g] + jnp.sum(ps[g], axis=-1, keepdims=True)
        acc_ref[g] = alphas[g] * acc_ref[g] + pvs[g]
        m_ref[g] = m_news[g]


    @pl.when(j == n_steps - 1)
    def _():
        gates = jax.nn.sigmoid(sm_ref[0])
        outs = []
        for g in range(NSA_KV_HEADS):
            o_sel = acc_ref[g] / jnp.maximum(l_ref[g], 1e-30)
            o_cmp = ocmp_ref[0, g]
            o_win = owin_ref[0, g]
            for r in range(NSA_GROUP):
                hcol = GATE_COL + g * NSA_GROUP + r
                rows = slice(r * Ts, (r + 1) * Ts)
                outs.append(gates[:, hcol:hcol + 1] * o_cmp[rows]
                            + gates[:, NSA_HEADS + hcol:NSA_HEADS + hcol + 1] * o_sel[rows]
                            + gates[:, 2 * NSA_HEADS + hcol:2 * NSA_HEADS + hcol + 1] * o_win[rows])
        o_ref[0] = jnp.concatenate(outs, axis=1)


def _nsa_sample_sel(cache_t, l, page_table, qr3, kv3, selt, ocmp, owin, sm3):
    B, n_pages = page_table.shape
    Ts = qr3.shape[1]
    pg = math.gcd(n_pages, PAGES_PER_STEP)
    n_steps = n_pages // pg
    R = NSA_GROUP * Ts
    ex = jnp.asarray((np.arange(LANES)[:, None] == (np.arange(pg * PAGE_SIZE)[None, :] // SEL_BLOCK)).astype(np.float32)).astype(bf16)
    tok = lambda w: pl.BlockSpec((1, Ts, w), lambda b, j, pt: (b, 0, 0))
    stk = pl.BlockSpec((1, NSA_KV_HEADS, R, HEAD_DIM), lambda b, j, pt: (b, 0, 0, 0))
    return pl.pallas_call(
        functools.partial(_nsa_sample_sel_body, pg=pg, Ts=Ts, n_steps=n_steps),
        grid_spec=pltpu.PrefetchScalarGridSpec(
            num_scalar_prefetch=1,
            grid=(B, n_steps),
            in_specs=_page_specs(l, pg, 1) + [
                tok(NSA_WIDTH), tok(8 * HEAD_DIM),
                pl.BlockSpec((1, selt.shape[1], selt.shape[2]), lambda b, j, pt: (b, 0, 0)),
                pl.BlockSpec((LANES, pg * PAGE_SIZE), lambda b, j, pt: (0, 0)),
                stk, stk, tok(LANES)],
            out_specs=tok(NSA_WIDTH),
            scratch_shapes=[pltpu.VMEM((NSA_KV_HEADS, R, 1), f32), pltpu.VMEM((NSA_KV_HEADS, R, 1), f32),
                            pltpu.VMEM((NSA_KV_HEADS, R, HEAD_DIM), f32)]),
        out_shape=jax.ShapeDtypeStruct((B, Ts, NSA_WIDTH), f32),
        compiler_params=_cparams(("parallel", "arbitrary")),
        name="nsa_sample_sel",
    )(page_table, *([cache_t] * pg), qr3, kv3, selt, ex, ocmp, owin, sm3)


def _nsa_sample(q3, qr3, kv3, win3, sm3, cache_t, st_t, l, page_table, wp, b1t, w2bd):
    Ts = q3.shape[1]
    n_pages = page_table.shape[1]
    past_len = n_pages * PAGE_SIZE
    assert (past_len + Ts) // CMP_STRIDE * CMP_STRIDE <= past_len
    assert past_len % SEL_BLOCK == 0 and Ts <= SEL_BLOCK
    P = _cmp_sample(cache_t, l, page_table, wp)
    ocmp, owin, selt = _nsa_sample_mid(P, b1t, w2bd, q3, qr3, win3, st_t, l, past_len)
    return _nsa_sample_sel(cache_t, l, page_table, qr3, kv3, selt, ocmp, owin, sm3)


def _rope_tables(pos):
    half = HEAD_DIM // 2
    inv = ROPE_THETA ** (-jnp.arange(half, dtype=f32) / half)
    ang = pos.astype(f32)[:, None] * inv[None, :]
    cos, sin = jnp.cos(ang), jnp.sin(ang)
    return jnp.concatenate([cos, cos, cos, cos], axis=1), jnp.concatenate([-sin, sin, -sin, sin], axis=1)


def _prep_w_in(w):
    cols = [w[:, _OFF[n][0]:_OFF[n][1]] for n in _PAD_ORDER]
    used = sum(c.shape[1] for c in cols)
    cols.append(jnp.zeros((w.shape[0], PROJ_PAD - used), w.dtype))
    return jnp.concatenate(cols, axis=1).astype(bf16)


def _layer(x, pos, prm, l, nsa_fn, ret_state, gdn_state, conv_buf, tm, state_t=None):
    B, T, _ = x.shape
    M = B * T
    tm = min(tm, M)
    cosf, sinf = _rope_tables(pos)
    if tm > T:
        cosf, sinf = jnp.tile(cosf, (tm // T, 1)), jnp.tile(sinf, (tm // T, 1))
    x2d = x.reshape(M, D_MODEL)
    outs = _proj(x2d, prm['norm_pre'][l][None], prm['w_in_p'][l], cosf, sinf, tm, state_t)
    ret, q, qr, kv, win, ng, gq, gg, sm = outs[:9]
    r3 = lambda a: a.reshape(B, T, a.shape[-1])
    ro, ret_new = _retention(r3(ret), ret_state, prm['ret_gn'][l][None], T)
    go, gdn_new = _gdn(r3(gq), r3(sm), r3(gg), conv_buf, prm['gdn_conv'][l], prm['gdn_a_log'][l],
                       prm['gdn_dt_bias'][l], prm['gdn_norm'][l][None], gdn_state, T)
    assert T >= CONV_WIDTH - 1
    conv_new = r3(gq)[:, T - (CONV_WIDTH - 1):]
    no, win_new = nsa_fn(r3(q), r3(qr), r3(kv), r3(win), r3(sm))
    y = _out(x2d, ro.reshape(M, -1), no.reshape(M, -1), ng, go.reshape(M, -1), prm['w_out_b'][l],
             prm['norm_post'][l][None], tm)
    if state_t is not None:
        return y.reshape(B, T, D_MODEL), (outs[9:], ret_new, gdn_new, conv_new)
    rows = kv.reshape(B, T, 4, NSA_KV_HEADS, HEAD_DIM)
    return y.reshape(B, T, D_MODEL), (rows, win_new, ret_new, gdn_new, conv_new)


def kernel(x_prompt, x_sample, cache_nsa_kv, page_table, state_nsa_win, state_ret, state_gdn, state_gdn_conv,
           w_in, w_out, norm_pre, norm_post, ret_gn, gdn_norm, gdn_conv, gdn_a_log, gdn_dt_bias,
           cmp_w1, cmp_b1, cmp_w2):
    Bp, T, _ = x_prompt.shape
    Bs, Ts, _ = x_sample.shape
    depth = w_in.shape[0]
    past_len = page_table.shape[1] * PAGE_SIZE
    pos_p = jnp.arange(T, dtype=jnp.int32)
    pos_s = past_len + jnp.arange(Ts, dtype=jnp.int32)
    prm = {'w_in_p': [_prep_w_in(w_in[l]) for l in range(depth)],
           'w_out_b': [w_out[l].astype(bf16) for l in range(depth)],
           'norm_pre': norm_pre, 'norm_post': norm_post, 'ret_gn': ret_gn, 'gdn_norm': gdn_norm,
           'gdn_conv': gdn_conv, 'gdn_a_log': gdn_a_log, 'gdn_dt_bias': gdn_dt_bias}
    yp, ys = x_prompt, x_sample
    cache_t = jnp.transpose(cache_nsa_kv, (0, 1, 3, 4, 5, 2)).reshape(
        depth, cache_nsa_kv.shape[1], 8 * HEAD_DIM, PAGE_SIZE)
    st_t = jnp.transpose(state_nsa_win, (0, 1, 3, 4, 5, 2)).reshape(
        depth, Bs, 4 * HEAD_DIM, state_nsa_win.shape[2])
    st_p, st_s = [], []
    stacks = None
    for l in range(depth):
        wp, b1t, w2bd = _cmp_weights(cmp_w1[l], cmp_b1[l], cmp_w2[l])

        def nsa_prompt(q3, qr3, kv3, win3, sm3):
            kc = _compress_prompt(kv3, wp, b1t, w2bd)
            return _nsa_prompt(q3, qr3, kc, kv3, win3, sm3), None

        def nsa_sample(q3, qr3, kv3, win3, sm3):
            no = _nsa_sample(q3, qr3, kv3, win3, sm3, cache_t, st_t, l, page_table, wp, b1t, w2bd)
            win_rows = win3.reshape(Bs, Ts, 2, NSA_KV_HEADS, HEAD_DIM)
            keys = jnp.concatenate([state_nsa_win[l], win_rows], axis=1)
            return no, keys[:, -state_nsa_win.shape[2]:]

        yp, sp = _layer(yp, pos_p, prm, l, nsa_prompt,
                        jnp.zeros((Bp, RET_HEADS, HEAD_DIM, HEAD_DIM), f32),
                        jnp.zeros((Bp, GDN_HEADS, HEAD_DIM, HEAD_DIM), f32),
                        jnp.zeros((Bp, CONV_WIDTH - 1, 3 * GDN_WIDTH), f32), min(PROMPT_ROW_TILE, T),
                        state_t=(l, depth, Bp, T, stacks))
        stacks = sp[0]
        ys, ss = _layer(ys, pos_s, prm, l, nsa_sample, state_ret[l], state_gdn[l], state_gdn_conv[l], Bs * Ts)
        st_p.append(sp)
        st_s.append(ss)
    stk = lambda sts, i: jnp.stack([s[i] for s in sts])
    keep = stacks[1].shape[-1]
    kv_p = jnp.transpose(stacks[0].reshape(depth, Bp, 4, NSA_KV_HEADS, HEAD_DIM, T), (0, 1, 5, 2, 3, 4))
    win_p = jnp.transpose(stacks[1].reshape(depth, Bp, 2, NSA_KV_HEADS, HEAD_DIM, keep), (0, 1, 5, 2, 3, 4))
    return (yp, ys, kv_p, stk(st_s, 0), win_p, stk(st_s, 1), stk(st_p, 1), stk(st_s, 2),
            stk(st_p, 2), stk(st_s, 3), stk(st_p, 3), stk(st_s, 4))
```

```python
import functools
import math

import numpy as np
import jax
import jax.numpy as jnp
from jax import lax
from jax.experimental import pallas as pl
from jax.experimental.pallas import tpu as pltpu

f32 = jnp.float32
bf16 = jnp.bfloat16

D_MODEL = 1024
HEAD_DIM = 64
RET_WIDTH = 256
NSA_WIDTH = 512
GDN_WIDTH = 256
RET_HEADS = 4
NSA_HEADS = 8
NSA_KV_HEADS = 2
NSA_GROUP = 4
GDN_HEADS = 4
CMP_BLOCK = 32
CMP_STRIDE = 16
SEL_BLOCK = 64
N_SELECT = 16
WINDOW = 512
RET_CHUNK = 64
GDN_CHUNK = 64
GDN_CHUNKS_PER_STEP = 4
CONV_WIDTH = 4
PAGE_SIZE = 128
ROPE_THETA = 10000.0
EPS = 1e-6

LANES = 128
VMEM_LIMIT = 56 * 1024 * 1024
PROMPT_ROW_TILE = 512

_SPLITS = (('ret', 4 * RET_WIDTH), ('nsa_q', NSA_WIDTH), ('nsa_kv', 6 * NSA_KV_HEADS * HEAD_DIM),
           ('nsa_gate', 3 * NSA_HEADS), ('nsa_g', NSA_WIDTH), ('gdn_qkv', 3 * GDN_WIDTH),
           ('gdn_ba', 2 * GDN_HEADS), ('gdn_g', GDN_WIDTH))
_OFF = {}
_o = 0
for _n, _w in _SPLITS:
    _OFF[_n] = (_o, _o + _w)
    _o += _w
PROJ_WIDTH = _o
_PAD_ORDER = ('ret', 'nsa_q', 'nsa_kv', 'nsa_g', 'gdn_qkv', 'gdn_g', 'nsa_gate', 'gdn_ba')
_POFF = {}
_o = 0
for _n in _PAD_ORDER:
    _w = _OFF[_n][1] - _OFF[_n][0]
    _POFF[_n] = _o
    _o += _w
SMALL_OFF = _POFF['nsa_gate']
PROJ_PAD = SMALL_OFF + LANES
GATE_COL = 0
BETA_COL = 3 * NSA_HEADS
A_COL = BETA_COL + GDN_HEADS

_NT = (((1,), (1,)), ((), ()))
_TN = (((0,), (0,)), ((), ()))
_HI = lax.Precision.HIGHEST


def _silu(x):
    return x * jax.nn.sigmoid(x)


def _cparams(sem):
    return pltpu.CompilerParams(dimension_semantics=sem, vmem_limit_bytes=VMEM_LIMIT)


def _proj_body(x_ref, g_ref, w_ref, cos_ref, sin_ref, *refs, n_t=None, n_keep=None):
    ret_ref, q_ref, qr_ref, kv_ref, win_ref, ng_ref, gq_ref, gg_ref, sm_ref = refs[-11:-2] if n_t else refs
    x = x_ref[...]
    h = x * lax.rsqrt(jnp.mean(x * x, axis=-1, keepdims=True) + EPS) * g_ref[...]
    hb = h.astype(bf16)
    cos = cos_ref[...]
    sin = sin_ref[...]
    lane = lax.broadcasted_iota(jnp.int32, cos.shape, 1)
    low = (lane % HEAD_DIM) < HEAD_DIM // 2

    def mm(c0, width):
        return jnp.dot(hb, w_ref[:, c0:c0 + width], preferred_element_type=f32)

    def rope(v):
        sw = jnp.where(low, pltpu.roll(v, LANES - HEAD_DIM // 2, 1), pltpu.roll(v, HEAD_DIM // 2, 1))
        return v * cos + sw * sin

    r = mm(_POFF['ret'], 4 * RET_WIDTH)
    for j in range(2):
        ret_ref[:, j * LANES:(j + 1) * LANES] = rope(r[:, j * LANES:(j + 1) * LANES])
    for j in range(2, 4):
        ret_ref[:, j * LANES:(j + 1) * LANES] = rope(r[:, j * LANES:(j + 1) * LANES]) * (HEAD_DIM ** -0.5)
    ret_ref[:, 2 * RET_WIDTH:] = r[:, 2 * RET_WIDTH:]

    q = mm(_POFF['nsa_q'], NSA_WIDTH)
    q_ref[...] = q
    for j in range(NSA_WIDTH // LANES):
        qr_ref[:, j * LANES:(j + 1) * LANES] = rope(q[:, j * LANES:(j + 1) * LANES])

    kv = mm(_POFF['nsa_kv'], 6 * LANES)
    kv_ref[:, 0:2 * LANES] = kv[:, 0:2 * LANES]
    kv_ref[:, 2 * LANES:3 * LANES] = rope(kv[:, 2 * LANES:3 * LANES])
    kv_ref[:, 3 * LANES:4 * LANES] = kv[:, 3 * LANES:4 * LANES]
    win_ref[:, 0:LANES] = rope(kv[:, 4 * LANES:5 * LANES])
    win_ref[:, LANES:2 * LANES] = kv[:, 5 * LANES:6 * LANES]

    ng_ref[...] = mm(_POFF['nsa_g'], NSA_WIDTH)
    gq_ref[...] = mm(_POFF['gdn_qkv'], 3 * GDN_WIDTH)
    gg_ref[...] = mm(_POFF['gdn_g'], GDN_WIDTH)
    sm_ref[...] = mm(SMALL_OFF, LANES)

    if n_t:
        kvt_ref, wint_ref = refs[-2:]
        tm = x.shape[0]
        kvt_ref[...] = kv_ref[...].T
        pos_tile = pl.program_id(0) % n_t
        for jj in range(n_keep):
            @pl.when(pos_tile == n_t - n_keep + jj)
            def _():
                wint_ref[:, jj * tm:(jj + 1) * tm] = win_ref[...].T


def _proj(x2d, gain, wpad, cosf, sinf, tm, state_t=None):
    M = x2d.shape[0]
    n_pos = cosf.shape[0] // tm
    widths = (4 * RET_WIDTH, NSA_WIDTH, NSA_WIDTH, 4 * LANES, 2 * LANES, NSA_WIDTH, 3 * GDN_WIDTH, GDN_WIDTH, LANES)
    row = lambda w: pl.BlockSpec((tm, w), lambda i: (i, 0))
    in_specs = [row(D_MODEL),
                pl.BlockSpec((1, D_MODEL), lambda i: (0, 0)),
                pl.BlockSpec((D_MODEL, PROJ_PAD), lambda i: (0, 0)),
                pl.BlockSpec((tm, LANES), lambda i: (i % n_pos, 0)),
                pl.BlockSpec((tm, LANES), lambda i: (i % n_pos, 0))]
    out_specs = [row(w) for w in widths]
    out_shape = [jax.ShapeDtypeStruct((M, w), f32) for w in widths]
    args = [x2d, gain, wpad, cosf, sinf]
    kwargs, aliases = {}, {}
    if state_t is not None:
        l, depth, B, T, stacks = state_t
        keep = min(WINDOW, T)
        assert T % tm == 0 and keep % tm == 0
        n_t = T // tm
        kwargs = dict(n_t=n_t, n_keep=keep // tm)
        out_specs += [pl.BlockSpec((None, None, 4 * LANES, tm), lambda i: (l, i // n_t, 0, i % n_t)),
                      pl.BlockSpec((None, None, 2 * LANES, keep), lambda i: (l, i // n_t, 0, 0))]
        out_shape += [jax.ShapeDtypeStruct((depth, B, 4 * LANES, T), f32),
                      jax.ShapeDtypeStruct((depth, B, 2 * LANES, keep), f32)]
        if stacks is not None:
            in_specs = [pl.BlockSpec(memory_space=pl.ANY)] * 2 + in_specs
            args = list(stacks) + args
            aliases = {0: len(widths), 1: len(widths) + 1}
    body = functools.partial(_proj_body, **kwargs)
    if aliases:
        body = functools.partial(_skip_leading, body, 2)
    return pl.pallas_call(
        body,
        grid=(M // tm,),
        in_specs=in_specs,
        out_specs=out_specs,
        out_shape=out_shape,
        input_output_aliases=aliases,
        compiler_params=_cparams(("arbitrary",)),
        name="proj",
    )(*args)


def _skip_leading(body, n, *refs, **kw):
    return body(*refs[n:], **kw)


def _ret_body(ret_ref, dm_ref, qd_ref, kd_ref, gn_ref, s0_ref, o_ref, s_ref, *, c, nchunk, sdec):
    @pl.when(pl.program_id(1) == 0)
    def _():
        s_ref[...] = s0_ref[...]

    H = range(RET_HEADS)
    hsl = lambda x, h: x[:, h * HEAD_DIM:(h + 1) * HEAD_DIM]
    Ss = [s_ref[0, h] for h in H]
    for ci in range(nchunk):
        blk = ret_ref[0, ci * c:(ci + 1) * c, :]
        q_all = blk[:, 0:RET_WIDTH].astype(bf16)
        k_all = blk[:, RET_WIDTH:2 * RET_WIDTH]
        kb_all = k_all.astype(bf16)
        kd_all = (k_all * kd_ref[...]).astype(bf16)
        v_all = blk[:, 2 * RET_WIDTH:3 * RET_WIDTH].astype(bf16)
        gate = _silu(blk[:, 3 * RET_WIDTH:])
        atts = [(lax.dot_general(hsl(q_all, h), hsl(kb_all, h), _NT, preferred_element_type=f32) * dm_ref[h]).astype(bf16)
                for h in H]
        qss = [jnp.dot(hsl(q_all, h), Ss[h].astype(bf16), preferred_element_type=f32) for h in H]
        attvs = [jnp.dot(atts[h], hsl(v_all, h), preferred_element_type=f32) for h in H]
        kvs = [lax.dot_general(hsl(kd_all, h), hsl(v_all, h), _TN, preferred_element_type=f32) for h in H]
        Ss = [Ss[h] * sdec[h] + kvs[h] for h in H]
        qd = qd_ref[...]
        outs = []
        for h in H:
            o = attvs[h] + qss[h] * hsl(qd, h)
            mu = jnp.mean(o, axis=-1, keepdims=True)
            d = o - mu
            var = jnp.mean(d * d, axis=-1, keepdims=True)
            outs.append(d * lax.rsqrt(var + EPS))
        o_ref[0, ci * c:(ci + 1) * c, :] = jnp.concatenate(outs, axis=1) * gn_ref[...] * gate
    for h in H:
        s_ref[0, h] = Ss[h]


def _retention(ret3, s0, gn, T):
    B = ret3.shape[0]
    c = math.gcd(T, RET_CHUNK)
    nchunk = min(T // c, 4)
    tb = c * nchunk
    hh = jnp.arange(RET_HEADS, dtype=f32)
    lg = jnp.log1p(-jnp.exp2(-5.0 - hh))
    ii = jnp.arange(c, dtype=f32)
    rel = ii[:, None] - ii[None, :]
    dm = jnp.exp(jnp.where(rel[None] >= 0, rel[None] * lg[:, None, None], -jnp.inf))
    qd = jnp.repeat(jnp.exp((ii + 1.0)[:, None] * lg[None, :]), HEAD_DIM, axis=1)
    kd = jnp.repeat(jnp.exp((c - 1.0 - ii)[:, None] * lg[None, :]), HEAD_DIM, axis=1)
    sdec = tuple(float((1.0 - 2.0 ** (-5.0 - h)) ** c) for h in range(RET_HEADS))
    return pl.pallas_call(
        functools.partial(_ret_body, c=c, nchunk=nchunk, sdec=sdec),
        grid=(B, T // tb),
        in_specs=[pl.BlockSpec((1, tb, 4 * RET_WIDTH), lambda b, j: (b, j, 0)),
                  pl.BlockSpec((RET_HEADS, c, c), lambda b, j: (0, 0, 0)),
                  pl.BlockSpec((c, RET_WIDTH), lambda b, j: (0, 0)),
                  pl.BlockSpec((c, RET_WIDTH), lambda b, j: (0, 0)),
                  pl.BlockSpec((1, RET_WIDTH), lambda b, j: (0, 0)),
                  pl.BlockSpec((1, RET_HEADS, HEAD_DIM, HEAD_DIM), lambda b, j: (b, 0, 0, 0))],
        out_specs=[pl.BlockSpec((1, tb, RET_WIDTH), lambda b, j: (b, j, 0)),
                   pl.BlockSpec((1, RET_HEADS, HEAD_DIM, HEAD_DIM), lambda b, j: (b, 0, 0, 0))],
        out_shape=[jax.ShapeDtypeStruct((B, T, RET_WIDTH), f32),
                   jax.ShapeDtypeStruct((B, RET_HEADS, HEAD_DIM, HEAD_DIM), f32)],
        compiler_params=_cparams(("parallel", "arbitrary")),
        name="retention",
    )(ret3, dm, qd, kd, gn, s0)


def _softplus(x):
    return jnp.maximum(x, 0.0) + jnp.log1p(jnp.exp(-jnp.abs(x)))


def _split_bf16(x):
    hi = x.astype(bf16)
    return hi, (x - hi.astype(f32)).astype(bf16)


def _dot_split(a, b):
    d = lambda x, y: jnp.dot(x, y, preferred_element_type=f32)
    return d(a[0], b[0]) + d(a[0], b[1]) + d(a[1], b[0])


def _unit_lower_inverse(n_list, c):
    rr = lax.broadcasted_iota(jnp.int32, (c, c), 0)
    cc = lax.broadcasted_iota(jnp.int32, (c, c), 1)
    eye = (rr == cc).astype(f32)
    ress = [eye - n for n in n_list]
    pws = [_split_bf16(-n) for n in n_list]
    span = 2
    while span < c:
        pws = [_split_bf16(_dot_split(pw, pw)) for pw in pws]
        ress = [res + _dot_split(_split_bf16(res), pw) for res, pw in zip(ress, pws)]
        span *= 2
    return ress


def _gdn_body(x_ref, sm_ref, gg_ref, cb_ref, cw_ref, al_ref, dtb_ref, gnorm_ref, s0_ref,
              o_ref, s_ref, xc_ref, *, c, nchunk):
    tb = c * nchunk

    @pl.when(pl.program_id(1) == 0)
    def _():
        s_ref[...] = s0_ref[...]
        xc_ref[5:8, :] = cb_ref[0]

    xc_ref[8:8 + tb, :] = x_ref[0]
    y = cw_ref[0:1, :] * xc_ref[pl.ds(5, tb), :]
    for j in range(1, CONV_WIDTH):
        y = y + cw_ref[j:j + 1, :] * xc_ref[pl.ds(5 + j, tb), :]
    tail = xc_ref[pl.ds(5 + tb, 3), :]
    xc_ref[5:8, :] = tail
    y = _silu(y)

    sm = sm_ref[0]
    beta_all = jax.nn.sigmoid(sm)
    g_all = -jnp.exp(al_ref[...]) * _softplus(sm + dtb_ref[...])
    rr = lax.broadcasted_iota(jnp.int32, (c, c), 0)
    cc = lax.broadcasted_iota(jnp.int32, (c, c), 1)
    tri = rr >= cc
    strict = rr > cc
    trif = tri.astype(f32)
    triu = (rr <= cc).astype(f32)
    gg = gg_ref[0]
    H = range(GDN_HEADS)
    P = [(ci, h) for ci in range(nchunk) for h in H]

    qs, ks, vs, betas, Gcs, Ls, kbs, kbfs, n_list = {}, {}, {}, {}, {}, {}, {}, {}, []
    for ci in range(nchunk):
        rows = slice(ci * c, (ci + 1) * c)
        gch = g_all[rows]
        G_cols = jnp.dot(trif, gch, preferred_element_type=f32, precision=_HI)
        G_rows = jnp.dot(gch.T, triu, preferred_element_type=f32, precision=_HI)
        for h in H:
            p = (ci, h)
            hs = lambda base: y[rows, base + h * HEAD_DIM: base + (h + 1) * HEAD_DIM]
            q, k = hs(0), hs(GDN_WIDTH)
            qs[p] = q * lax.rsqrt(jnp.sum(q * q, axis=-1, keepdims=True) + EPS) * (HEAD_DIM ** -0.5)
            k = k * lax.rsqrt(jnp.sum(k * k, axis=-1, keepdims=True) + EPS)
            ks[p] = k
            vs[p] = hs(2 * GDN_WIDTH)
            betas[p] = beta_all[rows, BETA_COL + h: BETA_COL + h + 1]
            Gcs[p] = G_cols[:, A_COL + h: A_COL + h + 1]
            Gr = G_rows[A_COL + h: A_COL + h + 1, :]
            Ls[p] = jnp.exp(jnp.where(tri, Gcs[p] - Gr, -jnp.inf))
            kbs[p] = k * betas[p]
            kbfs[p] = k.astype(bf16)
            n_list.append(jnp.where(strict, lax.dot_general(kbs[p].astype(bf16), kbfs[p], _NT,
                                                            preferred_element_type=f32) * Ls[p], 0.0))
    tinvs = dict(zip(P, [t.astype(bf16) for t in _unit_lower_inverse(n_list, c)]))
    egs = {p: jnp.exp(Gcs[p]) for p in P}
    us = {p: jnp.dot(tinvs[p], (vs[p] * betas[p]).astype(bf16), preferred_element_type=f32) for p in P}
    ws = {p: jnp.dot(tinvs[p], (kbs[p] * egs[p]).astype(bf16), preferred_element_type=f32).astype(bf16) for p in P}
    qks = {p: (lax.dot_general(qs[p].astype(bf16), kbfs[p], _NT, preferred_element_type=f32) * Ls[p]).astype(bf16) for p in P}
    qes = {p: (qs[p] * egs[p]).astype(bf16) for p in P}

    Ss = [s_ref[0, h] for h in H]
    for ci in range(nchunk):
        Sbs = [S.astype(bf16) for S in Ss]
        qss = [jnp.dot(qes[ci, h], Sbs[h], preferred_element_type=f32) for h in H]
        vnbs = [(us[ci, h] - jnp.dot(ws[ci, h], Sbs[h], preferred_element_type=f32)).astype(bf16) for h in H]
        os_ = [qss[h] + jnp.dot(qks[ci, h], vnbs[h], preferred_element_type=f32) for h in H]
        new_s = []
        for h in H:
            g_last = Gcs[ci, h][c - 1:c, :]
            kdec = (ks[ci, h] * jnp.exp(g_last - Gcs[ci, h])).astype(bf16)
            new_s.append(Ss[h] * jnp.exp(g_last) + lax.dot_general(kdec, vnbs[h], _TN, preferred_element_type=f32))
        Ss = new_s
        outs = []
        for h in H:
            o = os_[h]
            yo = o * lax.rsqrt(jnp.mean(o * o, axis=-1, keepdims=True) + EPS) * gnorm_ref[...]
            outs.append(yo * _silu(gg[ci * c:(ci + 1) * c, h * HEAD_DIM:(h + 1) * HEAD_DIM]))
        o_ref[0, ci * c:(ci + 1) * c, :] = jnp.concatenate(outs, axis=1)
    for h in H:
        s_ref[0, h] = Ss[h]


def _gdn(gq3, sm3, gg3, conv_buf, conv_w, a_log, dt_bias, gnorm, s0, T):
    B = gq3.shape[0]
    c = math.gcd(T, GDN_CHUNK)
    nchunk = min(T // c, GDN_CHUNKS_PER_STEP)
    tb = c * nchunk
    C = 3 * GDN_WIDTH
    pad = lambda v: jnp.zeros((1, LANES), f32).at[0, A_COL:A_COL + GDN_HEADS].set(v.astype(f32))
    return pl.pallas_call(
        functools.partial(_gdn_body, c=c, nchunk=nchunk),
        grid=(B, T // tb),
        in_specs=[pl.BlockSpec((1, tb, C), lambda b, j: (b, j, 0)),
                  pl.BlockSpec((1, tb, LANES), lambda b, j: (b, j, 0)),
                  pl.BlockSpec((1, tb, GDN_WIDTH), lambda b, j: (b, j, 0)),
                  pl.BlockSpec((1, CONV_WIDTH - 1, C), lambda b, j: (b, 0, 0)),
                  pl.BlockSpec((CONV_WIDTH, C), lambda b, j: (0, 0)),
                  pl.BlockSpec((1, LANES), lambda b, j: (0, 0)),
                  pl.BlockSpec((1, LANES), lambda b, j: (0, 0)),
                  pl.BlockSpec((1, HEAD_DIM), lambda b, j: (0, 0)),
                  pl.BlockSpec((1, GDN_HEADS, HEAD_DIM, HEAD_DIM), lambda b, j: (b, 0, 0, 0))],
        out_specs=[pl.BlockSpec((1, tb, GDN_WIDTH), lambda b, j: (b, j, 0)),
                   pl.BlockSpec((1, GDN_HEADS, HEAD_DIM, HEAD_DIM), lambda b, j: (b, 0, 0, 0))],
        out_shape=[jax.ShapeDtypeStruct((B, T, GDN_WIDTH), f32),
                   jax.ShapeDtypeStruct((B, GDN_HEADS, HEAD_DIM, HEAD_DIM), f32)],
        scratch_shapes=[pltpu.VMEM((8 + tb + 8, C), f32)],
        compiler_params=_cparams(("parallel", "arbitrary")),
        name="gdn",
    )(gq3, sm3, gg3, conv_buf, conv_w, pad(a_log), pad(dt_bias), gnorm, s0)


def _cmp_weights(w1, b1, w2):
    nq = 2 * NSA_KV_HEADS
    kind = np.arange(nq) // NSA_KV_HEADS
    eye = jnp.eye(nq, dtype=f32)
    r = CMP_BLOCK // CMP_STRIDE
    w = w1[kind].reshape(nq, r, CMP_STRIDE, HEAD_DIM, HEAD_DIM)
    w = jnp.transpose(w, (2, 0, 3, 1, 4))
    wc = (w[:, :, :, :, None, :] * eye[None, :, None, None, :, None])
    wc = wc.reshape(CMP_STRIDE, nq * HEAD_DIM, r * nq * HEAD_DIM).astype(bf16)
    b1t = b1[kind].reshape(1, nq * HEAD_DIM)
    w2bd = (w2[kind][:, :, None, :] * eye[:, None, :, None]).reshape(nq * HEAD_DIM, nq * HEAD_DIM).astype(bf16)
    w7 = wc.reshape(CMP_STRIDE // 2, 2, 2, LANES, r, 2, LANES)
    wp = jnp.stack([w7[:, :, pr, :, :, pr, :].reshape(CMP_STRIDE // 2, 2 * LANES, r * LANES) for pr in range(2)])
    return wp, b1t, w2bd


def _segment_perm():
    seg = PAGE_SIZE // CMP_STRIDE
    off = np.arange(PAGE_SIZE)
    return jnp.asarray(((off[None, :] % CMP_STRIDE) * seg + off[None, :] // CMP_STRIDE == off[:, None]).astype(np.float32)).astype(bf16)


def _first_layer(xt_ref, wp_ref, n_row):
    accs = [jnp.zeros((n_row, 4 * HEAD_DIM), f32) for _ in range(2)]
    for s2 in range(CMP_STRIDE // 2):
        x0 = xt_ref[2 * s2]
        x1 = xt_ref[2 * s2 + 1]
        for pr in range(2):
            xs = jnp.concatenate([x0[:, pr * LANES:(pr + 1) * LANES], x1[:, pr * LANES:(pr + 1) * LANES]], axis=1)
            accs[pr] = accs[pr] + jnp.dot(xs, wp_ref[pr, s2], preferred_element_type=f32)
    return jnp.concatenate([accs[0][:, :LANES], accs[1][:, :LANES], accs[0][:, LANES:], accs[1][:, LANES:]], axis=1)


def _cmp_prompt_body(x_ref, perm_ref, wp_ref, b1_ref, w2_ref, o_ref, xt_ref, *, n_seg):
    seg = PAGE_SIZE // CMP_STRIDE
    for i in range(n_seg // seg):
        xp = jnp.dot(perm_ref[...], x_ref[0, i * PAGE_SIZE:(i + 1) * PAGE_SIZE, :].astype(bf16), preferred_element_type=f32)
        for s in range(CMP_STRIDE):
            xt_ref[s, i * seg:(i + 1) * seg, :] = xp[s * seg:(s + 1) * seg, :].astype(bf16)
    acc = _first_layer(xt_ref, wp_ref, n_seg)
    first = acc[:, :4 * HEAD_DIM]
    second = pltpu.roll(acc[:, 4 * HEAD_DIM:], n_seg - 1, 0)
    hid = _silu(b1_ref[...] + first + second)
    out = jnp.dot(hid.astype(bf16), w2_ref[...], preferred_element_type=f32)
    for qi in range(2 * NSA_KV_HEADS):
        o_ref[0, qi] = out[:, qi * HEAD_DIM:(qi + 1) * HEAD_DIM]


def _compress_prompt(kv3, wp, b1t, w2bd):
    B, T, _ = kv3.shape
    assert T % PAGE_SIZE == 0
    n_seg = T // CMP_STRIDE
    return pl.pallas_call(
        functools.partial(_cmp_prompt_body, n_seg=n_seg),
        grid=(B,),
        in_specs=[pl.BlockSpec((1, T, 4 * HEAD_DIM), lambda b: (b, 0, 0)),
                  pl.BlockSpec((PAGE_SIZE, PAGE_SIZE), lambda b: (0, 0)),
                  pl.BlockSpec((2, CMP_STRIDE // 2, 4 * HEAD_DIM, 4 * HEAD_DIM), lambda b: (0, 0, 0, 0)),
                  pl.BlockSpec((1, 4 * HEAD_DIM), lambda b: (0, 0)),
                  pl.BlockSpec((4 * HEAD_DIM, 4 * HEAD_DIM), lambda b: (0, 0))],
        out_specs=pl.BlockSpec((1, 4, n_seg, HEAD_DIM), lambda b: (b, 0, 0, 0)),
        out_shape=jax.ShapeDtypeStruct((B, 4, n_seg, HEAD_DIM), f32),
        scratch_shapes=[pltpu.VMEM((CMP_STRIDE, n_seg, 4 * HEAD_DIM), bf16)],
        compiler_params=_cparams(("parallel",)),
        name="compress_prompt",
    )(kv3, _segment_perm(), wp, b1t, w2bd)


def _topk_rows(score, k):
    n = score.shape[0]
    idx = lax.broadcasted_iota(jnp.int32, score.shape, 0)
    taken = jnp.zeros(score.shape, jnp.bool_)
    for _ in range(k):
        work = jnp.where(taken, -jnp.inf, score)
        m = jnp.max(work, axis=0, keepdims=True)
        cand = jnp.where((work == m) & jnp.logical_not(taken), idx, n)
        first = jnp.min(cand, axis=0, keepdims=True)
        taken = taken | (idx == first)
    return jnp.where(taken, 1.0, 0.0)


def _topk_rows_by_rank(score, k):
    n = score.shape[0]
    idx = lax.broadcasted_iota(jnp.int32, score.shape, 0)
    rank = jnp.zeros(score.shape, jnp.int32)
    for i in range(n):
        row = score[i:i + 1, :]
        ahead = (row > score) | ((row == score) & (idx > i))
        rank = rank + ahead.astype(jnp.int32)
    return jnp.where(rank < k, 1.0, 0.0)


def _softmax_rows(s, mask):
    s = jnp.where(mask, s, -jnp.inf)
    m = jnp.max(s, axis=-1, keepdims=True)
    m = jnp.where(m == -jnp.inf, 0.0, m)
    p = jnp.exp(s - m)
    return p / jnp.maximum(jnp.sum(p, axis=-1, keepdims=True), 1e-30)


def _attend_tile(qs, kbs, vbs, biases_t, carries):
    n = range(len(qs))
    ss = [lax.dot_general(kbs[i], qs[i], _NT, preferred_element_type=f32) for i in n]
    ss = [ss[i] + jnp.concatenate([biases_t[i]] * NSA_GROUP, axis=1) for i in n]
    m_news = [jnp.maximum(carries[i][0], jnp.max(ss[i], axis=0, keepdims=True)) for i in n]
    m_safes = [jnp.where(m == -jnp.inf, 0.0, m) for m in m_news]
    alphas = [jnp.exp(carries[i][0] - m_safes[i]) for i in n]
    ps = [jnp.exp(ss[i] - m_safes[i]) for i in n]
    ls = [alphas[i] * carries[i][1] + jnp.sum(ps[i], axis=0, keepdims=True) for i in n]
    accs = [alphas[i] * carries[i][2] + lax.dot_general(vbs[i], ps[i].astype(bf16), _TN, preferred_element_type=f32)
            for i in n]
    return [(m_news[i], ls[i], accs[i]) for i in n]


def _flash_init(R):
    return (jnp.full((1, R), -jnp.inf, f32), jnp.zeros((1, R), f32), jnp.zeros((HEAD_DIM, R), f32))


def _nsa_prompt_body(q_ref, qr_ref, kc_ref, ks_ref, vs_ref, kw_ref, vw_ref, sm_ref, ov_ref, ext_ref,
                     o_ref, wbias_ref, *, tq, tk, tkw, n_wt, T, n_cmp, n_blk):
    i = pl.program_id(1)
    t0 = i * tq
    scale = HEAD_DIM ** -0.5
    G = range(NSA_KV_HEADS)
    R = NSA_GROUP * tq
    gates = jax.nn.sigmoid(sm_ref[0])
    n_pad = kc_ref.shape[2]
    tpos_r = t0 + lax.broadcasted_iota(jnp.int32, (tq, 1), 0)
    tpos_c = t0 + lax.broadcasted_iota(jnp.int32, (1, tq), 1)
    heads = lambda x, g: [(x[:, (g * NSA_GROUP + r) * HEAD_DIM:(g * NSA_GROUP + r + 1) * HEAD_DIM] * scale).astype(bf16)
                          for r in range(NSA_GROUP)]

    o_cmps, imps = [], []
    nn = lax.broadcasted_iota(jnp.int32, (1, n_pad), 1)
    cmask = (nn * CMP_STRIDE + (CMP_BLOCK - 1) <= tpos_r) & (nn < n_cmp)
    HH = [(g, r) for g in G for r in range(NSA_GROUP)]
    qhs = {g: heads(q_ref[0], g) for g in G}
    kcs = {g: kc_ref[0, g].astype(bf16) for g in G}
    vcs = {g: kc_ref[0, NSA_KV_HEADS + g].astype(bf16) for g in G}
    ss = {h: jnp.where(cmask, lax.dot_general(qhs[h[0]][h[1]], kcs[h[0]], _NT, preferred_element_type=f32), -jnp.inf)
          for h in HH}
    ms = {h: jnp.max(ss[h], axis=-1, keepdims=True) for h in HH}
    es = {h: jnp.exp(ss[h] - jnp.where(ms[h] == -jnp.inf, 0.0, ms[h])) for h in HH}
    ps = {h: es[h] / jnp.maximum(jnp.sum(es[h], axis=-1, keepdims=True), 1e-30) for h in HH}
    ocs = {h: jnp.dot(ps[h].astype(bf16), vcs[h[0]], preferred_element_type=f32) for h in HH}
    for g in G:
        o_cmps.append([ocs[g, r] for r in range(NSA_GROUP)])
        imps.append(ps[g, 0] + ps[g, 1] + ps[g, 2] + ps[g, 3])

    score = jnp.concatenate([lax.dot_general(ov_ref[...], imps[g], _NT, preferred_element_type=f32, precision=_HI)[:n_blk]
                             for g in G], axis=1)
    jj = lax.broadcasted_iota(jnp.int32, score.shape, 0)
    jt = jnp.concatenate([tpos_c] * NSA_KV_HEADS, axis=1) // SEL_BLOCK
    valid = jj <= jt
    forced = valid & ((jj == 0) | (jj == jt) | (jj == jt - 1))
    score = jnp.where(forced, jnp.inf, jnp.where(valid, score, -jnp.inf))
    sel_t = _topk_rows_by_rank(score, min(N_SELECT, n_blk)).astype(bf16)
    if n_blk < LANES:
        sel_t = jnp.concatenate([sel_t, jnp.zeros((LANES - n_blk, NSA_KV_HEADS * tq), bf16)], axis=0)
    w0 = pl.multiple_of(jnp.clip(t0 - WINDOW, 0, T - n_wt * tkw), tq)
    for j in range(n_wt):
        dist = tpos_c - (w0 + j * tkw + lax.broadcasted_iota(jnp.int32, (tkw, 1), 0))
        wbias_ref[j] = jnp.where((dist >= 0) & (dist < WINDOW), 0.0, -jnp.inf)

    stack = lambda x, g: (jnp.concatenate([x[:, (g * NSA_GROUP + r) * HEAD_DIM:(g * NSA_GROUP + r + 1) * HEAD_DIM]
                                           for r in range(NSA_GROUP)], axis=0) * scale).astype(bf16)
    qrs = [stack(qr_ref[0], g) for g in G]
    kv_tile = lambda ref, k0, n, g: ref[0, pl.ds(k0, n), g * HEAD_DIM:(g + 1) * HEAD_DIM].astype(bf16)

    def sel_step(kt, carries):
        k0 = pl.multiple_of(kt * tk, tk)
        causal = k0 + lax.broadcasted_iota(jnp.int32, (tk, 1), 0) <= tpos_c
        biases = [jnp.where((jnp.dot(ext_ref[kt], sel_t[:, g * tq:(g + 1) * tq], preferred_element_type=f32) > 0.5)
                            & causal, 0.0, -jnp.inf) for g in G]
        return tuple(_attend_tile(qrs, [kv_tile(ks_ref, k0, tk, g) for g in G], [kv_tile(vs_ref, k0, tk, g) for g in G],
                                  biases, carries))

    sel = lax.fori_loop(0, (t0 + tq + tk - 1) // tk, sel_step, tuple(_flash_init(R) for _ in G))

    win = [_flash_init(R) for _ in G]
    for j in range(n_wt):
        k0 = pl.multiple_of(w0 + j * tkw, tq)
        win = _attend_tile(qrs, [kv_tile(kw_ref, k0, tkw, g) for g in G], [kv_tile(vw_ref, k0, tkw, g) for g in G],
                           [wbias_ref[j]] * NSA_KV_HEADS, win)

    gates_t = gates.T
    for g in G:
        o_sel_t = sel[g][2] / jnp.maximum(sel[g][1], 1e-30)
        o_win_t = win[g][2] / jnp.maximum(win[g][1], 1e-30)
        outs = []
        for r in range(NSA_GROUP):
            hcol = GATE_COL + g * NSA_GROUP + r
            cols = slice(r * tq, (r + 1) * tq)
            both_t = (gates_t[NSA_HEADS + hcol:NSA_HEADS + hcol + 1, :] * o_sel_t[:, cols]
                      + gates_t[2 * NSA_HEADS + hcol:2 * NSA_HEADS + hcol + 1, :] * o_win_t[:, cols])
            outs.append(gates[:, hcol:hcol + 1] * o_cmps[g][r] + both_t.T)
        o_ref[0, :, g * NSA_GROUP * HEAD_DIM:(g + 1) * NSA_GROUP * HEAD_DIM] = jnp.concatenate(outs, axis=1)


def _overlap_t(n_blk_pad, n_cmp, n_pad):
    ci = np.arange(n_pad)[None, :] * CMP_STRIDE
    sj = np.arange(n_blk_pad)[:, None] * SEL_BLOCK
    ov = (ci < sj + SEL_BLOCK) & (ci + CMP_BLOCK > sj) & (np.arange(n_pad)[None, :] < n_cmp)
    return jnp.asarray(ov.astype(np.float32))


def _nsa_prompt(q3, qr3, kc, kv3, win3, sm3):
    B, T, _ = q3.shape
    tq = 128
    tk = 512 if T >= 1024 else 256
    tkw = WINDOW + tq if T >= 1024 else 128
    n_wt = -(-min(T, WINDOW + tq) // tkw)
    assert T % tk == 0 and n_wt * tkw <= T
    n_seg = T // CMP_STRIDE
    n_cmp = n_seg - CMP_BLOCK // CMP_STRIDE + 1
    n_blk = -(-T // SEL_BLOCK)
    ov = _overlap_t(LANES, n_cmp, n_seg)
    key_blk = (np.arange(T // tk)[:, None, None] * tk + np.arange(tk)[None, :, None]) // SEL_BLOCK
    ex = jnp.asarray((np.arange(LANES)[None, None, :] == key_blk).astype(np.float32)).astype(bf16)
    qspec = pl.BlockSpec((1, tq, NSA_WIDTH), lambda b, i: (b, i, 0))
    col = lambda j: pl.BlockSpec((1, T, LANES), lambda b, i: (b, 0, j))
    return pl.pallas_call(
        functools.partial(_nsa_prompt_body, tq=tq, tk=tk, tkw=tkw, n_wt=n_wt, T=T, n_cmp=n_cmp, n_blk=n_blk),
        grid=(B, T // tq),
        in_specs=[qspec, qspec,
                  pl.BlockSpec((1, 4, n_seg, HEAD_DIM), lambda b, i: (b, 0, 0, 0)),
                  col(2), col(3), col(0), col(1),
                  pl.BlockSpec((1, tq, LANES), lambda b, i: (b, i, 0)),
                  pl.BlockSpec((LANES, n_seg), lambda b, i: (0, 0)),
                  pl.BlockSpec((T // tk, tk, LANES), lambda b, i: (0, 0, 0))],
        out_specs=qspec,
        out_shape=jax.ShapeDtypeStruct((B, T, NSA_WIDTH), f32),
        scratch_shapes=[pltpu.VMEM((n_wt, tkw, tq), f32)],
        compiler_params=_cparams(("parallel", "arbitrary")),
        name="nsa_prompt",
    )(q3, qr3, kc, kv3, kv3, win3, win3, sm3, ov, ex)


def _out_body(x_ref, ro_ref, no_ref, ng_ref, go_ref, w_ref, g_ref, y_ref):
    no = no_ref[...] * _silu(ng_ref[...])
    z = (jnp.dot(ro_ref[...].astype(bf16), w_ref[0:RET_WIDTH, :], preferred_element_type=f32)
         + jnp.dot(no.astype(bf16), w_ref[RET_WIDTH:RET_WIDTH + NSA_WIDTH, :], preferred_element_type=f32)
         + jnp.dot(go_ref[...].astype(bf16), w_ref[RET_WIDTH + NSA_WIDTH:, :], preferred_element_type=f32))
    y_ref[...] = x_ref[...] + z * lax.rsqrt(jnp.mean(z * z, axis=-1, keepdims=True) + EPS) * g_ref[...]


def _out(x2d, ro, no, ng, go, w_out_b, gain, tm):
    M = x2d.shape[0]
    row = lambda w: pl.BlockSpec((tm, w), lambda i: (i, 0))
    return pl.pallas_call(
        _out_body,
        grid=(M // tm,),
        in_specs=[row(D_MODEL), row(RET_WIDTH), row(NSA_WIDTH), row(NSA_WIDTH), row(GDN_WIDTH),
                  pl.BlockSpec((D_MODEL, D_MODEL), lambda i: (0, 0)),
                  pl.BlockSpec((1, D_MODEL), lambda i: (0, 0))],
        out_specs=row(D_MODEL),
        out_shape=jax.ShapeDtypeStruct((M, D_MODEL), f32),
        compiler_params=_cparams(("parallel",)),
        name="out",
    )(x2d, ro, no, ng, go, w_out_b, gain)


PAGES_PER_STEP = 64


def _page_specs(l, pg, half):
    return [pl.BlockSpec((None, None, 4 * HEAD_DIM, PAGE_SIZE),
                         lambda b, j, pt, i=i: (l, pt[b, j * pg + i], half, 0)) for i in range(pg)]


def _cmp_sample_body(pt_ref, *refs, pg):
    page_refs, perm_ref, wp_ref, o_ref, xt_ref = refs[:pg], refs[pg], refs[pg + 1], refs[pg + 2], refs[pg + 3]
    seg = PAGE_SIZE // CMP_STRIDE
    for i in range(pg):
        xp = lax.dot_general(perm_ref[...], page_refs[i][...].astype(bf16), _NT, preferred_element_type=f32)
        for s in range(CMP_STRIDE):
            xt_ref[s, i * seg:(i + 1) * seg, :] = xp[s * seg:(s + 1) * seg, :].astype(bf16)
    o_ref[0] = _first_layer(xt_ref, wp_ref, pg * seg)


def _cmp_sample(cache_t, l, page_table, wp):
    B, n_pages = page_table.shape
    pg = math.gcd(n_pages, PAGES_PER_STEP)
    n_row = pg * PAGE_SIZE // CMP_STRIDE
    return pl.pallas_call(
        functools.partial(_cmp_sample_body, pg=pg),
        grid_spec=pltpu.PrefetchScalarGridSpec(
            num_scalar_prefetch=1,
            grid=(B, n_pages // pg),
            in_specs=_page_specs(l, pg, 0) + [pl.BlockSpec((PAGE_SIZE, PAGE_SIZE), lambda b, j, pt: (0, 0)),
                                              pl.BlockSpec((2, CMP_STRIDE // 2, 4 * HEAD_DIM, 4 * HEAD_DIM),
                                                           lambda b, j, pt: (0, 0, 0, 0))],
            out_specs=pl.BlockSpec((1, n_row, 8 * HEAD_DIM), lambda b, j, pt: (b, j, 0)),
            scratch_shapes=[pltpu.VMEM((CMP_STRIDE, n_row, 4 * HEAD_DIM), bf16)]),
        out_shape=jax.ShapeDtypeStruct((B, n_pages * PAGE_SIZE // CMP_STRIDE, 8 * HEAD_DIM), f32),
        compiler_params=_cparams(("parallel", "arbitrary")),
        name="compress_sample",
    )(page_table, *([cache_t] * pg), _segment_perm(), wp)


def _stack_heads(x, g):
    return jnp.concatenate([x[:, (g * NSA_GROUP + r) * HEAD_DIM:(g * NSA_GROUP + r + 1) * HEAD_DIM]
                            for r in range(NSA_GROUP)], axis=0)


def _nsa_sample_mid_body(p_ref, b1_ref, w2_ref, q_ref, qr_ref, win_ref, st_ref, ov_ref,
                         ocmp_ref, owin_ref, selt_ref, *, Ts, n_cmp, past_len, wb):
    P = p_ref[0]
    n_seg = P.shape[0]
    hid = _silu(b1_ref[...] + P[:, :4 * HEAD_DIM] + pltpu.roll(P[:, 4 * HEAD_DIM:], n_seg - 1, 0))
    kcv = jnp.dot(hid.astype(bf16), w2_ref[...], preferred_element_type=f32)
    scale = HEAD_DIM ** -0.5
    R = NSA_GROUP * Ts
    trow = lax.broadcasted_iota(jnp.int32, (R, 1), 0) % Ts
    imps = []
    for g in range(NSA_KV_HEADS):
        qs = (_stack_heads(q_ref[0], g) * scale).astype(bf16)
        kc = kcv[:, g * HEAD_DIM:(g + 1) * HEAD_DIM].astype(bf16)
        vc = kcv[:, (NSA_KV_HEADS + g) * HEAD_DIM:(NSA_KV_HEADS + g + 1) * HEAD_DIM].astype(bf16)
        s = lax.dot_general(qs, kc, _NT, preferred_element_type=f32)
        nn = lax.broadcasted_iota(jnp.int32, (1, n_seg), 1)
        cmask = (nn * CMP_STRIDE + (CMP_BLOCK - 1) <= past_len + trow) & (nn < n_cmp)
        p = _softmax_rows(s, cmask)
        ocmp_ref[0, g] = jnp.dot(p.astype(bf16), vc, preferred_element_type=f32)
        imps.append(jnp.sum(p.reshape(NSA_GROUP, Ts, n_seg), axis=0))
        qrs = (_stack_heads(qr_ref[0], g) * scale).astype(bf16)
        kwt = st_ref[g * HEAD_DIM:(g + 1) * HEAD_DIM, :].astype(bf16)
        vwt = st_ref[(NSA_KV_HEADS + g) * HEAD_DIM:(NSA_KV_HEADS + g + 1) * HEAD_DIM, :].astype(bf16)
        wn = win_ref[0]
        knew = wn[:, g * HEAD_DIM:(g + 1) * HEAD_DIM].astype(bf16)
        vnew = wn[:, (NSA_KV_HEADS + g) * HEAD_DIM:(NSA_KV_HEADS + g + 1) * HEAD_DIM].astype(bf16)
        s_b = jnp.dot(qrs, kwt, preferred_element_type=f32)
        s_n = lax.dot_general(qrs, knew, _NT, preferred_element_type=f32)
        jb = lax.broadcasted_iota(jnp.int32, (1, wb), 1)
        dist_b = trow + wb - jb
        s_b = jnp.where((dist_b >= 0) & (dist_b < WINDOW) & (past_len - wb + jb >= 0), s_b, -jnp.inf)
        dist_n = trow - lax.broadcasted_iota(jnp.int32, (1, Ts), 1)
        s_n = jnp.where((dist_n >= 0) & (dist_n < WINDOW), s_n, -jnp.inf)
        m = jnp.maximum(jnp.max(s_b, axis=-1, keepdims=True), jnp.max(s_n, axis=-1, keepdims=True))
        m = jnp.where(m == -jnp.inf, 0.0, m)
        pb = jnp.exp(s_b - m)
        pn = jnp.exp(s_n - m)
        den = jnp.sum(pb, axis=-1, keepdims=True) + jnp.sum(pn, axis=-1, keepdims=True)
        o = (lax.dot_general(pb.astype(bf16), vwt, _NT, preferred_element_type=f32)
             + jnp.dot(pn.astype(bf16), vnew, preferred_element_type=f32))
        owin_ref[0, g] = o / jnp.maximum(den, 1e-30)

    imp2 = jnp.concatenate(imps, axis=0)
    score = lax.dot_general(ov_ref[...], imp2, _NT, preferred_element_type=f32, precision=_HI)
    jj = lax.broadcasted_iota(jnp.int32, score.shape, 0)
    jt = (past_len + lax.broadcasted_iota(jnp.int32, score.shape, 1) % Ts) // SEL_BLOCK
    valid = jj <= jt
    forced = valid & ((jj == 0) | (jj == jt) | (jj == jt - 1))
    score = jnp.where(forced, jnp.inf, jnp.where(valid, score, -jnp.inf))
    selt_ref[0] = _topk_rows(score, N_SELECT)


def _nsa_sample_mid(P, b1t, w2bd, q3, qr3, win3, st_t, l, past_len):
    B, Ts, _ = q3.shape
    n_seg = P.shape[1]
    n_cmp = n_seg - CMP_BLOCK // CMP_STRIDE + 1
    n_blk = -(-(past_len + Ts) // SEL_BLOCK)
    nb_pad = -(-n_blk // 8) * 8
    wb = st_t.shape[-1]
    ov = _overlap_t(nb_pad, n_cmp, n_seg)
    R = NSA_GROUP * Ts
    tok = lambda w: pl.BlockSpec((1, Ts, w), lambda b: (b, 0, 0))
    return pl.pallas_call(
        functools.partial(_nsa_sample_mid_body, Ts=Ts, n_cmp=n_cmp, past_len=past_len, wb=wb),
        grid=(B,),
        in_specs=[pl.BlockSpec((1, n_seg, 8 * HEAD_DIM), lambda b: (b, 0, 0)),
                  pl.BlockSpec((1, 4 * HEAD_DIM), lambda b: (0, 0)),
                  pl.BlockSpec((4 * HEAD_DIM, 4 * HEAD_DIM), lambda b: (0, 0)),
                  tok(NSA_WIDTH), tok(NSA_WIDTH), tok(4 * HEAD_DIM),
                  pl.BlockSpec((None, None, 4 * HEAD_DIM, wb), lambda b: (l, b, 0, 0)),
                  pl.BlockSpec((nb_pad, n_seg), lambda b: (0, 0))],
        out_specs=[pl.BlockSpec((1, NSA_KV_HEADS, R, HEAD_DIM), lambda b: (b, 0, 0, 0)),
                   pl.BlockSpec((1, NSA_KV_HEADS, R, HEAD_DIM), lambda b: (b, 0, 0, 0)),
                   pl.BlockSpec((1, nb_pad, NSA_KV_HEADS * Ts), lambda b: (b, 0, 0))],
        out_shape=[jax.ShapeDtypeStruct((B, NSA_KV_HEADS, R, HEAD_DIM), f32),
                   jax.ShapeDtypeStruct((B, NSA_KV_HEADS, R, HEAD_DIM), f32),
                   jax.ShapeDtypeStruct((B, nb_pad, NSA_KV_HEADS * Ts), f32)],
        compiler_params=_cparams(("parallel",)),
        name="nsa_sample_mid",
    )(P, b1t, w2bd, q3, qr3, win3, st_t, ov)


def _nsa_sample_sel_body(pt_ref, *refs, pg, Ts, n_steps):
    page_refs = refs[:pg]
    qr_ref, kvn_ref, selt_ref, ex_ref, ocmp_ref, owin_ref, sm_ref, o_ref, m_ref, l_ref, acc_ref = refs[pg:]
    j = pl.program_id(1)
    scale = HEAD_DIM ** -0.5
    R = NSA_GROUP * Ts
    GT = NSA_KV_HEADS * Ts
    nb = 2 * pg
    trow = lax.broadcasted_iota(jnp.int32, (R, 1), 0) % Ts
    eye = (lax.broadcasted_iota(jnp.int32, (GT, GT), 0) == lax.broadcasted_iota(jnp.int32, (GT, GT), 1)).astype(bf16)
    sel_j = selt_ref[0, pl.ds(pl.multiple_of(j * nb, nb), nb), :].astype(bf16)
    if nb < LANES:
        sel_j = jnp.concatenate([sel_j, jnp.zeros((LANES - nb, GT), bf16)], axis=0)
    sel_rows = lax.dot_general(eye, sel_j, _NT, preferred_element_type=f32)
    full = jnp.dot(sel_rows.astype(bf16), ex_ref[...], preferred_element_type=f32)

    G = range(NSA_KV_HEADS)
    qrs = [(_stack_heads(qr_ref[0], g) * scale).astype(bf16) for g in G]

    @pl.when(j == 0)
    def _():
        kvn = kvn_ref[0]
        for g in G:
            knew = kvn[:, (2 * NSA_KV_HEADS + g) * HEAD_DIM:(2 * NSA_KV_HEADS + g + 1) * HEAD_DIM].astype(bf16)
            vnew = kvn[:, (3 * NSA_KV_HEADS + g) * HEAD_DIM:(3 * NSA_KV_HEADS + g + 1) * HEAD_DIM].astype(bf16)
            s_n = lax.dot_general(qrs[g], knew, _NT, preferred_element_type=f32)
            s_n = jnp.where(lax.broadcasted_iota(jnp.int32, (1, Ts), 1) <= trow, s_n, -jnp.inf)
            m0 = jnp.max(s_n, axis=-1, keepdims=True)
            p0 = jnp.exp(s_n - m0)
            m_ref[g] = m0
            l_ref[g] = jnp.sum(p0, axis=-1, keepdims=True)
            acc_ref[g] = jnp.dot(p0.astype(bf16), vnew, preferred_element_type=f32)

    kts = [jnp.concatenate([page_refs[i][g * HEAD_DIM:(g + 1) * HEAD_DIM, :] for i in range(pg)], axis=1).astype(bf16)
           for g in G]
    vts = [jnp.concatenate([page_refs[i][(NSA_KV_HEADS + g) * HEAD_DIM:(NSA_KV_HEADS + g + 1) * HEAD_DIM, :]
                            for i in range(pg)], axis=1).astype(bf16) for g in G]
    ss = [jnp.dot(qrs[g], kts[g], preferred_element_type=f32) for g in G]
    ss = [jnp.where(jnp.concatenate([full[g * Ts:(g + 1) * Ts]] * NSA_GROUP, axis=0) > 0.5, ss[g], -jnp.inf) for g in G]
    m_olds = [m_ref[g] for g in G]
    m_news = [jnp.maximum(m_olds[g], jnp.max(ss[g], axis=-1, keepdims=True)) for g in G]
    alphas = [jnp.exp(m_olds[g] - m_news[g]) for g in G]
    ps = [jnp.exp(ss[g] - m_news[g]) for g in G]
    pvs = [lax.dot_general(ps[g].astype(bf16), vts[g], _NT, preferred_element_type=f32) for g in G]
    for g in G:
        l_ref[g] = alphas[g] * l_ref[g] + jnp.sum(ps[g], axis=-1, keepdims=True)
        acc_ref[g] = alphas[g] * acc_ref[g] + pvs[g]
        m_ref[g] = m_news[g]


    @pl.when(j == n_steps - 1)
    def _():
        gates = jax.nn.sigmoid(sm_ref[0])
        outs = []
        for g in range(NSA_KV_HEADS):
            o_sel = acc_ref[g] / jnp.maximum(l_ref[g], 1e-30)
            o_cmp = ocmp_ref[0, g]
            o_win = owin_ref[0, g]
            for r in range(NSA_GROUP):
                hcol = GATE_COL + g * NSA_GROUP + r
                rows = slice(r * Ts, (r + 1) * Ts)
                outs.append(gates[:, hcol:hcol + 1] * o_cmp[rows]
                            + gates[:, NSA_HEADS + hcol:NSA_HEADS + hcol + 1] * o_sel[rows]
                            + gates[:, 2 * NSA_HEADS + hcol:2 * NSA_HEADS + hcol + 1] * o_win[rows])
        o_ref[0] = jnp.concatenate(outs, axis=1)


def _nsa_sample_sel(cache_t, l, page_table, qr3, kv3, selt, ocmp, owin, sm3):
    B, n_pages = page_table.shape
    Ts = qr3.shape[1]
    pg = math.gcd(n_pages, PAGES_PER_STEP)
    n_steps = n_pages // pg
    R = NSA_GROUP * Ts
    ex = jnp.asarray((np.arange(LANES)[:, None] == (np.arange(pg * PAGE_SIZE)[None, :] // SEL_BLOCK)).astype(np.float32)).astype(bf16)
    tok = lambda w: pl.BlockSpec((1, Ts, w), lambda b, j, pt: (b, 0, 0))
    stk = pl.BlockSpec((1, NSA_KV_HEADS, R, HEAD_DIM), lambda b, j, pt: (b, 0, 0, 0))
    return pl.pallas_call(
        functools.partial(_nsa_sample_sel_body, pg=pg, Ts=Ts, n_steps=n_steps),
        grid_spec=pltpu.PrefetchScalarGridSpec(
            num_scalar_prefetch=1,
            grid=(B, n_steps),
            in_specs=_page_specs(l, pg, 1) + [
                tok(NSA_WIDTH), tok(8 * HEAD_DIM),
                pl.BlockSpec((1, selt.shape[1], selt.shape[2]), lambda b, j, pt: (b, 0, 0)),
                pl.BlockSpec((LANES, pg * PAGE_SIZE), lambda b, j, pt: (0, 0)),
                stk, stk, tok(LANES)],
            out_specs=tok(NSA_WIDTH),
            scratch_shapes=[pltpu.VMEM((NSA_KV_HEADS, R, 1), f32), pltpu.VMEM((NSA_KV_HEADS, R, 1), f32),
                            pltpu.VMEM((NSA_KV_HEADS, R, HEAD_DIM), f32)]),
        out_shape=jax.ShapeDtypeStruct((B, Ts, NSA_WIDTH), f32),
        compiler_params=_cparams(("parallel", "arbitrary")),
        name="nsa_sample_sel",
    )(page_table, *([cache_t] * pg), qr3, kv3, selt, ex, ocmp, owin, sm3)


def _nsa_sample(q3, qr3, kv3, win3, sm3, cache_t, st_t, l, page_table, wp, b1t, w2bd):
    Ts = q3.shape[1]
    n_pages = page_table.shape[1]
    past_len = n_pages * PAGE_SIZE
    assert (past_len + Ts) // CMP_STRIDE * CMP_STRIDE <= past_len
    assert past_len % SEL_BLOCK == 0 and Ts <= SEL_BLOCK
    P = _cmp_sample(cache_t, l, page_table, wp)
    ocmp, owin, selt = _nsa_sample_mid(P, b1t, w2bd, q3, qr3, win3, st_t, l, past_len)
    return _nsa_sample_sel(cache_t, l, page_table, qr3, kv3, selt, ocmp, owin, sm3)


def _rope_tables(pos):
    half = HEAD_DIM // 2
    inv = ROPE_THETA ** (-jnp.arange(half, dtype=f32) / half)
    ang = pos.astype(f32)[:, None] * inv[None, :]
    cos, sin = jnp.cos(ang), jnp.sin(ang)
    return jnp.concatenate([cos, cos, cos, cos], axis=1), jnp.concatenate([-sin, sin, -sin, sin], axis=1)


def _prep_w_in(w):
    cols = [w[:, _OFF[n][0]:_OFF[n][1]] for n in _PAD_ORDER]
    used = sum(c.shape[1] for c in cols)
    cols.append(jnp.zeros((w.shape[0], PROJ_PAD - used), w.dtype))
    return jnp.concatenate(cols, axis=1).astype(bf16)


def _layer(x, pos, prm, l, nsa_fn, ret_state, gdn_state, conv_buf, tm, state_t=None):
    B, T, _ = x.shape
    M = B * T
    tm = min(tm, M)
    cosf, sinf = _rope_tables(pos)
    if tm > T:
        cosf, sinf = jnp.tile(cosf, (tm // T, 1)), jnp.tile(sinf, (tm // T, 1))
    x2d = x.reshape(M, D_MODEL)
    outs = _proj(x2d, prm['norm_pre'][l][None], prm['w_in_p'][l], cosf, sinf, tm, state_t)
    ret, q, qr, kv, win, ng, gq, gg, sm = outs[:9]
    r3 = lambda a: a.reshape(B, T, a.shape[-1])
    ro, ret_new = _retention(r3(ret), ret_state, prm['ret_gn'][l][None], T)
    go, gdn_new = _gdn(r3(gq), r3(sm), r3(gg), conv_buf, prm['gdn_conv'][l], prm['gdn_a_log'][l],
                       prm['gdn_dt_bias'][l], prm['gdn_norm'][l][None], gdn_state, T)
    assert T >= CONV_WIDTH - 1
    conv_new = r3(gq)[:, T - (CONV_WIDTH - 1):]
    no, win_new = nsa_fn(r3(q), r3(qr), r3(kv), r3(win), r3(sm))
    y = _out(x2d, ro.reshape(M, -1), no.reshape(M, -1), ng, go.reshape(M, -1), prm['w_out_b'][l],
             prm['norm_post'][l][None], tm)
    if state_t is not None:
        return y.reshape(B, T, D_MODEL), (outs[9:], ret_new, gdn_new, conv_new)
    rows = kv.reshape(B, T, 4, NSA_KV_HEADS, HEAD_DIM)
    return y.reshape(B, T, D_MODEL), (rows, win_new, ret_new, gdn_new, conv_new)


def kernel(x_prompt, x_sample, cache_nsa_kv, page_table, state_nsa_win, state_ret, state_gdn, state_gdn_conv,
           w_in, w_out, norm_pre, norm_post, ret_gn, gdn_norm, gdn_conv, gdn_a_log, gdn_dt_bias,
           cmp_w1, cmp_b1, cmp_w2):
    Bp, T, _ = x_prompt.shape
    Bs, Ts, _ = x_sample.shape
    depth = w_in.shape[0]
    past_len = page_table.shape[1] * PAGE_SIZE
    pos_p = jnp.arange(T, dtype=jnp.int32)
    pos_s = past_len + jnp.arange(Ts, dtype=jnp.int32)
    prm = {'w_in_p': [_prep_w_in(w_in[l]) for l in range(depth)],
           'w_out_b': [w_out[l].astype(bf16) for l in range(depth)],
           'norm_pre': norm_pre, 'norm_post': norm_post, 'ret_gn': ret_gn, 'gdn_norm': gdn_norm,
           'gdn_conv': gdn_conv, 'gdn_a_log': gdn_a_log, 'gdn_dt_bias': gdn_dt_bias}
    yp, ys = x_prompt, x_sample
    cache_t = jnp.transpose(cache_nsa_kv, (0, 1, 3, 4, 5, 2)).reshape(
        depth, cache_nsa_kv.shape[1], 8 * HEAD_DIM, PAGE_SIZE)
    st_t = jnp.transpose(state_nsa_win, (0, 1, 3, 4, 5, 2)).reshape(
        depth, Bs, 4 * HEAD_DIM, state_nsa_win.shape[2])
    st_p, st_s = [], []
    stacks = None
    for l in range(depth):
        wp, b1t, w2bd = _cmp_weights(cmp_w1[l], cmp_b1[l], cmp_w2[l])

        def nsa_prompt(q3, qr3, kv3, win3, sm3):
            kc = _compress_prompt(kv3, wp, b1t, w2bd)
            return _nsa_prompt(q3, qr3, kc, kv3, win3, sm3), None

        def nsa_sample(q3, qr3, kv3, win3, sm3):
            no = _nsa_sample(q3, qr3, kv3, win3, sm3, cache_t, st_t, l, page_table, wp, b1t, w2bd)
            win_rows = win3.reshape(Bs, Ts, 2, NSA_KV_HEADS, HEAD_DIM)
            keys = jnp.concatenate([state_nsa_win[l], win_rows], axis=1)
            return no, keys[:, -state_nsa_win.shape[2]:]

        yp, sp = _layer(yp, pos_p, prm, l, nsa_prompt,
                        jnp.zeros((Bp, RET_HEADS, HEAD_DIM, HEAD_DIM), f32),
                        jnp.zeros((Bp, GDN_HEADS, HEAD_DIM, HEAD_DIM), f32),
                        jnp.zeros((Bp, CONV_WIDTH - 1, 3 * GDN_WIDTH), f32), min(PROMPT_ROW_TILE, T),
                        state_t=(l, depth, Bp, T, stacks))
        stacks = sp[0]
        ys, ss = _layer(ys, pos_s, prm, l, nsa_sample, state_ret[l], state_gdn[l], state_gdn_conv[l], Bs * Ts)
        st_p.append(sp)
        st_s.append(ss)
    stk = lambda sts, i: jnp.stack([s[i] for s in sts])
    keep = stacks[1].shape[-1]
    kv_p = jnp.transpose(stacks[0].reshape(depth, Bp, 4, NSA_KV_HEADS, HEAD_DIM, T), (0, 1, 5, 2, 3, 4))
    win_p = jnp.transpose(stacks[1].reshape(depth, Bp, 2, NSA_KV_HEADS, HEAD_DIM, keep), (0, 1, 5, 2, 3, 4))
    return (yp, ys, kv_p, stk(st_s, 0), win_p, stk(st_s, 1), stk(st_p, 1), stk(st_s, 2),
            stk(st_p, 2), stk(st_s, 3), stk(st_p, 3), stk(st_s, 4))
```

```python
import functools
import math

import numpy as np
import jax
import jax.numpy as jnp
from jax import lax
from jax.experimental import pallas as pl
from jax.experimental.pallas import tpu as pltpu

f32 = jnp.float32
bf16 = jnp.bfloat16

D_MODEL = 1024
HEAD_DIM = 64
RET_WIDTH = 256
NSA_WIDTH = 512
GDN_WIDTH = 256
RET_HEADS = 4
NSA_HEADS = 8
NSA_KV_HEADS = 2
NSA_GROUP = 4
GDN_HEADS = 4
CMP_BLOCK = 32
CMP_STRIDE = 16
SEL_BLOCK = 64
N_SELECT = 16
WINDOW = 512
RET_CHUNK = 64
GDN_CHUNK = 64
GDN_CHUNKS_PER_STEP = 4
CONV_WIDTH = 4
PAGE_SIZE = 128
ROPE_THETA = 10000.0
EPS = 1e-6

LANES = 128
VMEM_LIMIT = 56 * 1024 * 1024
PROMPT_ROW_TILE = 512

_SPLITS = (('ret', 4 * RET_WIDTH), ('nsa_q', NSA_WIDTH), ('nsa_kv', 6 * NSA_KV_HEADS * HEAD_DIM),
           ('nsa_gate', 3 * NSA_HEADS), ('nsa_g', NSA_WIDTH), ('gdn_qkv', 3 * GDN_WIDTH),
           ('gdn_ba', 2 * GDN_HEADS), ('gdn_g', GDN_WIDTH))
_OFF = {}
_o = 0
for _n, _w in _SPLITS:
    _OFF[_n] = (_o, _o + _w)
    _o += _w
PROJ_WIDTH = _o
_PAD_ORDER = ('ret', 'nsa_q', 'nsa_kv', 'nsa_g', 'gdn_qkv', 'gdn_g', 'nsa_gate', 'gdn_ba')
_POFF = {}
_o = 0
for _n in _PAD_ORDER:
    _w = _OFF[_n][1] - _OFF[_n][0]
    _POFF[_n] = _o
    _o += _w
SMALL_OFF = _POFF['nsa_gate']
PROJ_PAD = SMALL_OFF + LANES
GATE_COL = 0
BETA_COL = 3 * NSA_HEADS
A_COL = BETA_COL + GDN_HEADS

_NT = (((1,), (1,)), ((), ()))
_TN = (((0,), (0,)), ((), ()))
_HI = lax.Precision.HIGHEST


def _silu(x):
    return x * jax.nn.sigmoid(x)


def _cparams(sem):
    return pltpu.CompilerParams(dimension_semantics=sem, vmem_limit_bytes=VMEM_LIMIT)


def _proj_body(x_ref, g_ref, w_ref, cos_ref, sin_ref, *refs, n_t=None, n_keep=None):
    ret_ref, q_ref, qr_ref, kv_ref, win_ref, ng_ref, gq_ref, gg_ref, sm_ref = refs[-11:-2] if n_t else refs
    x = x_ref[...]
    h = x * lax.rsqrt(jnp.mean(x * x, axis=-1, keepdims=True) + EPS) * g_ref[...]
    hb = h.astype(bf16)
    cos = cos_ref[...]
    sin = sin_ref[...]
    lane = lax.broadcasted_iota(jnp.int32, cos.shape, 1)
    low = (lane % HEAD_DIM) < HEAD_DIM // 2

    def mm(c0, width):
        return jnp.dot(hb, w_ref[:, c0:c0 + width], preferred_element_type=f32)

    def rope(v):
        sw = jnp.where(low, pltpu.roll(v, LANES - HEAD_DIM // 2, 1), pltpu.roll(v, HEAD_DIM // 2, 1))
        return v * cos + sw * sin

    r = mm(_POFF['ret'], 4 * RET_WIDTH)
    for j in range(2):
        ret_ref[:, j * LANES:(j + 1) * LANES] = rope(r[:, j * LANES:(j + 1) * LANES])
    for j in range(2, 4):
        ret_ref[:, j * LANES:(j + 1) * LANES] = rope(r[:, j * LANES:(j + 1) * LANES]) * (HEAD_DIM ** -0.5)
    ret_ref[:, 2 * RET_WIDTH:] = r[:, 2 * RET_WIDTH:]

    q = mm(_POFF['nsa_q'], NSA_WIDTH)
    q_ref[...] = q
    for j in range(NSA_WIDTH // LANES):
        qr_ref[:, j * LANES:(j + 1) * LANES] = rope(q[:, j * LANES:(j + 1) * LANES])

    kv = mm(_POFF['nsa_kv'], 6 * LANES)
    kv_ref[:, 0:2 * LANES] = kv[:, 0:2 * LANES]
    kv_ref[:, 2 * LANES:3 * LANES] = rope(kv[:, 2 * LANES:3 * LANES])
    kv_ref[:, 3 * LANES:4 * LANES] = kv[:, 3 * LANES:4 * LANES]
    win_ref[:, 0:LANES] = rope(kv[:, 4 * LANES:5 * LANES])
    win_ref[:, LANES:2 * LANES] = kv[:, 5 * LANES:6 * LANES]

    ng_ref[...] = mm(_POFF['nsa_g'], NSA_WIDTH)
    gq_ref[...] = mm(_POFF['gdn_qkv'], 3 * GDN_WIDTH)
    gg_ref[...] = mm(_POFF['gdn_g'], GDN_WIDTH)
    sm_ref[...] = mm(SMALL_OFF, LANES)

    if n_t:
        kvt_ref, wint_ref = refs[-2:]
        tm = x.shape[0]
        kvt_ref[...] = kv_ref[...].T
        pos_tile = pl.program_id(0) % n_t
        for jj in range(n_keep):
            @pl.when(pos_tile == n_t - n_keep + jj)
            def _():
                wint_ref[:, jj * tm:(jj + 1) * tm] = win_ref[...].T


def _proj(x2d, gain, wpad, cosf, sinf, tm, state_t=None):
    M = x2d.shape[0]
    n_pos = cosf.shape[0] // tm
    widths = (4 * RET_WIDTH, NSA_WIDTH, NSA_WIDTH, 4 * LANES, 2 * LANES, NSA_WIDTH, 3 * GDN_WIDTH, GDN_WIDTH, LANES)
    row = lambda w: pl.BlockSpec((tm, w), lambda i: (i, 0))
    in_specs = [row(D_MODEL),
                pl.BlockSpec((1, D_MODEL), lambda i: (0, 0)),
                pl.BlockSpec((D_MODEL, PROJ_PAD), lambda i: (0, 0)),
                pl.BlockSpec((tm, LANES), lambda i: (i % n_pos, 0)),
                pl.BlockSpec((tm, LANES), lambda i: (i % n_pos, 0))]
    out_specs = [row(w) for w in widths]
    out_shape = [jax.ShapeDtypeStruct((M, w), f32) for w in widths]
    args = [x2d, gain, wpad, cosf, sinf]
    kwargs, aliases = {}, {}
    if state_t is not None:
        l, depth, B, T, stacks = state_t
        keep = min(WINDOW, T)
        assert T % tm == 0 and keep % tm == 0
        n_t = T // tm
        kwargs = dict(n_t=n_t, n_keep=keep // tm)
        out_specs += [pl.BlockSpec((None, None, 4 * LANES, tm), lambda i: (l, i // n_t, 0, i % n_t)),
                      pl.BlockSpec((None, None, 2 * LANES, keep), lambda i: (l, i // n_t, 0, 0))]
        out_shape += [jax.ShapeDtypeStruct((depth, B, 4 * LANES, T), f32),
                      jax.ShapeDtypeStruct((depth, B, 2 * LANES, keep), f32)]
        if stacks is not None:
            in_specs = [pl.BlockSpec(memory_space=pl.ANY)] * 2 + in_specs
            args = list(stacks) + args
            aliases = {0: len(widths), 1: len(widths) + 1}
    body = functools.partial(_proj_body, **kwargs)
    if aliases:
        body = functools.partial(_skip_leading, body, 2)
    return pl.pallas_call(
        body,
        grid=(M // tm,),
        in_specs=in_specs,
        out_specs=out_specs,
        out_shape=out_shape,
        input_output_aliases=aliases,
        compiler_params=_cparams(("arbitrary",)),
        name="proj",
    )(*args)


def _skip_leading(body, n, *refs, **kw):
    return body(*refs[n:], **kw)


def _ret_body(ret_ref, dm_ref, qd_ref, kd_ref, gn_ref, s0_ref, o_ref, s_ref, *, c, nchunk, sdec):
    @pl.when(pl.program_id(1) == 0)
    def _():
        s_ref[...] = s0_ref[...]

    H = range(RET_HEADS)
    hsl = lambda x, h: x[:, h * HEAD_DIM:(h + 1) * HEAD_DIM]
    Ss = [s_ref[0, h] for h in H]
    for ci in range(nchunk):
        blk = ret_ref[0, ci * c:(ci + 1) * c, :]
        q_all = blk[:, 0:RET_WIDTH].astype(bf16)
        k_all = blk[:, RET_WIDTH:2 * RET_WIDTH]
        kb_all = k_all.astype(bf16)
        kd_all = (k_all * kd_ref[...]).astype(bf16)
        v_all = blk[:, 2 * RET_WIDTH:3 * RET_WIDTH].astype(bf16)
        gate = _silu(blk[:, 3 * RET_WIDTH:])
        atts = [(lax.dot_general(hsl(q_all, h), hsl(kb_all, h), _NT, preferred_element_type=f32) * dm_ref[h]).astype(bf16)
                for h in H]
        qss = [jnp.dot(hsl(q_all, h), Ss[h].astype(bf16), preferred_element_type=f32) for h in H]
        attvs = [jnp.dot(atts[h], hsl(v_all, h), preferred_element_type=f32) for h in H]
        kvs = [lax.dot_general(hsl(kd_all, h), hsl(v_all, h), _TN, preferred_element_type=f32) for h in H]
        Ss = [Ss[h] * sdec[h] + kvs[h] for h in H]
        qd = qd_ref[...]
        outs = []
        for h in H:
            o = attvs[h] + qss[h] * hsl(qd, h)
            mu = jnp.mean(o, axis=-1, keepdims=True)
            d = o - mu
            var = jnp.mean(d * d, axis=-1, keepdims=True)
            outs.append(d * lax.rsqrt(var + EPS))
        o_ref[0, ci * c:(ci + 1) * c, :] = jnp.concatenate(outs, axis=1) * gn_ref[...] * gate
    for h in H:
        s_ref[0, h] = Ss[h]


def _retention(ret3, s0, gn, T):
    B = ret3.shape[0]
    c = math.gcd(T, RET_CHUNK)
    nchunk = min(T // c, 4)
    tb = c * nchunk
    hh = jnp.arange(RET_HEADS, dtype=f32)
    lg = jnp.log1p(-jnp.exp2(-5.0 - hh))
    ii = jnp.arange(c, dtype=f32)
    rel = ii[:, None] - ii[None, :]
    dm = jnp.exp(jnp.where(rel[None] >= 0, rel[None] * lg[:, None, None], -jnp.inf))
    qd = jnp.repeat(jnp.exp((ii + 1.0)[:, None] * lg[None, :]), HEAD_DIM, axis=1)
    kd = jnp.repeat(jnp.exp((c - 1.0 - ii)[:, None] * lg[None, :]), HEAD_DIM, axis=1)
    sdec = tuple(float((1.0 - 2.0 ** (-5.0 - h)) ** c) for h in range(RET_HEADS))
    return pl.pallas_call(
        functools.partial(_ret_body, c=c, nchunk=nchunk, sdec=sdec),
        grid=(B, T // tb),
        in_specs=[pl.BlockSpec((1, tb, 4 * RET_WIDTH), lambda b, j: (b, j, 0)),
                  pl.BlockSpec((RET_HEADS, c, c), lambda b, j: (0, 0, 0)),
                  pl.BlockSpec((c, RET_WIDTH), lambda b, j: (0, 0)),
                  pl.BlockSpec((c, RET_WIDTH), lambda b, j: (0, 0)),
                  pl.BlockSpec((1, RET_WIDTH), lambda b, j: (0, 0)),
                  pl.BlockSpec((1, RET_HEADS, HEAD_DIM, HEAD_DIM), lambda b, j: (b, 0, 0, 0))],
        out_specs=[pl.BlockSpec((1, tb, RET_WIDTH), lambda b, j: (b, j, 0)),
                   pl.BlockSpec((1, RET_HEADS, HEAD_DIM, HEAD_DIM), lambda b, j: (b, 0, 0, 0))],
        out_shape=[jax.ShapeDtypeStruct((B, T, RET_WIDTH), f32),
                   jax.ShapeDtypeStruct((B, RET_HEADS, HEAD_DIM, HEAD_DIM), f32)],
        compiler_params=_cparams(("parallel", "arbitrary")),
        name="retention",
    )(ret3, dm, qd, kd, gn, s0)


def _softplus(x):
    return jnp.maximum(x, 0.0) + jnp.log1p(jnp.exp(-jnp.abs(x)))


def _split_bf16(x):
    hi = x.astype(bf16)
    return hi, (x - hi.astype(f32)).astype(bf16)


def _dot_split(a, b):
    d = lambda x, y: jnp.dot(x, y, preferred_element_type=f32)
    return d(a[0], b[0]) + d(a[0], b[1]) + d(a[1], b[0])


def _unit_lower_inverse(n_list, c):
    rr = lax.broadcasted_iota(jnp.int32, (c, c), 0)
    cc = lax.broadcasted_iota(jnp.int32, (c, c), 1)
    eye = (rr == cc).astype(f32)
    ress = [eye - n for n in n_list]
    pws = [_split_bf16(-n) for n in n_list]
    span = 2
    while span < c:
        pws = [_split_bf16(_dot_split(pw, pw)) for pw in pws]
        ress = [res + _dot_split(_split_bf16(res), pw) for res, pw in zip(ress, pws)]
        span *= 2
    return ress


def _gdn_body(x_ref, sm_ref, gg_ref, cb_ref, cw_ref, al_ref, dtb_ref, gnorm_ref, s0_ref,
              o_ref, s_ref, xc_ref, *, c, nchunk):
    tb = c * nchunk

    @pl.when(pl.program_id(1) == 0)
    def _():
        s_ref[...] = s0_ref[...]
        xc_ref[5:8, :] = cb_ref[0]

    xc_ref[8:8 + tb, :] = x_ref[0]
    y = cw_ref[0:1, :] * xc_ref[pl.ds(5, tb), :]
    for j in range(1, CONV_WIDTH):
        y = y + cw_ref[j:j + 1, :] * xc_ref[pl.ds(5 + j, tb), :]
    tail = xc_ref[pl.ds(5 + tb, 3), :]
    xc_ref[5:8, :] = tail
    y = _silu(y)

    sm = sm_ref[0]
    beta_all = jax.nn.sigmoid(sm)
    g_all = -jnp.exp(al_ref[...]) * _softplus(sm + dtb_ref[...])
    rr = lax.broadcasted_iota(jnp.int32, (c, c), 0)
    cc = lax.broadcasted_iota(jnp.int32, (c, c), 1)
    tri = rr >= cc
    strict = rr > cc
    trif = tri.astype(f32)
    triu = (rr <= cc).astype(f32)
    gg = gg_ref[0]
    H = range(GDN_HEADS)
    P = [(ci, h) for ci in range(nchunk) for h in H]

    qs, ks, vs, betas, Gcs, Ls, kbs, kbfs, n_list = {}, {}, {}, {}, {}, {}, {}, {}, []
    for ci in range(nchunk):
        rows = slice(ci * c, (ci + 1) * c)
        gch = g_all[rows]
        G_cols = jnp.dot(trif, gch, preferred_element_type=f32, precision=_HI)
        G_rows = jnp.dot(gch.T, triu, preferred_element_type=f32, precision=_HI)
        for h in H:
            p = (ci, h)
            hs = lambda base: y[rows, base + h * HEAD_DIM: base + (h + 1) * HEAD_DIM]
            q, k = hs(0), hs(GDN_WIDTH)
            qs[p] = q * lax.rsqrt(jnp.sum(q * q, axis=-1, keepdims=True) + EPS) * (HEAD_DIM ** -0.5)
            k = k * lax.rsqrt(jnp.sum(k * k, axis=-1, keepdims=True) + EPS)
            ks[p] = k
            vs[p] = hs(2 * GDN_WIDTH)
            betas[p] = beta_all[rows, BETA_COL + h: BETA_COL + h + 1]
            Gcs[p] = G_cols[:, A_COL + h: A_COL + h + 1]
            Gr = G_rows[A_COL + h: A_COL + h + 1, :]
            Ls[p] = jnp.exp(jnp.where(tri, Gcs[p] - Gr, -jnp.inf))
            kbs[p] = k * betas[p]
            kbfs[p] = k.astype(bf16)
            n_list.append(jnp.where(strict, lax.dot_general(kbs[p].astype(bf16), kbfs[p], _NT,
                                                            preferred_element_type=f32) * Ls[p], 0.0))
    tinvs = dict(zip(P, [t.astype(bf16) for t in _unit_lower_inverse(n_list, c)]))
    egs = {p: jnp.exp(Gcs[p]) for p in P}
    us = {p: jnp.dot(tinvs[p], (vs[p] * betas[p]).astype(bf16), preferred_element_type=f32) for p in P}
    ws = {p: jnp.dot(tinvs[p], (kbs[p] * egs[p]).astype(bf16), preferred_element_type=f32).astype(bf16) for p in P}
    qks = {p: (lax.dot_general(qs[p].astype(bf16), kbfs[p], _NT, preferred_element_type=f32) * Ls[p]).astype(bf16) for p in P}
    qes = {p: (qs[p] * egs[p]).astype(bf16) for p in P}

    Ss = [s_ref[0, h] for h in H]
    for ci in range(nchunk):
        Sbs = [S.astype(bf16) for S in Ss]
        qss = [jnp.dot(qes[ci, h], Sbs[h], preferred_element_type=f32) for h in H]
        vnbs = [(us[ci, h] - jnp.dot(ws[ci, h], Sbs[h], preferred_element_type=f32)).astype(bf16) for h in H]
        os_ = [qss[h] + jnp.dot(qks[ci, h], vnbs[h], preferred_element_type=f32) for h in H]
        new_s = []
        for h in H:
            g_last = Gcs[ci, h][c - 1:c, :]
            kdec = (ks[ci, h] * jnp.exp(g_last - Gcs[ci, h])).astype(bf16)
            new_s.append(Ss[h] * jnp.exp(g_last) + lax.dot_general(kdec, vnbs[h], _TN, preferred_element_type=f32))
        Ss = new_s
        outs = []
        for h in H:
            o = os_[h]
            yo = o * lax.rsqrt(jnp.mean(o * o, axis=-1, keepdims=True) + EPS) * gnorm_ref[...]
            outs.append(yo * _silu(gg[ci * c:(ci + 1) * c, h * HEAD_DIM:(h + 1) * HEAD_DIM]))
        o_ref[0, ci * c:(ci + 1) * c, :] = jnp.concatenate(outs, axis=1)
    for h in H:
        s_ref[0, h] = Ss[h]


def _gdn(gq3, sm3, gg3, conv_buf, conv_w, a_log, dt_bias, gnorm, s0, T):
    B = gq3.shape[0]
    c = math.gcd(T, GDN_CHUNK)
    nchunk = min(T // c, GDN_CHUNKS_PER_STEP)
    tb = c * nchunk
    C = 3 * GDN_WIDTH
    pad = lambda v: jnp.zeros((1, LANES), f32).at[0, A_COL:A_COL + GDN_HEADS].set(v.astype(f32))
    return pl.pallas_call(
        functools.partial(_gdn_body, c=c, nchunk=nchunk),
        grid=(B, T // tb),
        in_specs=[pl.BlockSpec((1, tb, C), lambda b, j: (b, j, 0)),
                  pl.BlockSpec((1, tb, LANES), lambda b, j: (b, j, 0)),
                  pl.BlockSpec((1, tb, GDN_WIDTH), lambda b, j: (b, j, 0)),
                  pl.BlockSpec((1, CONV_WIDTH - 1, C), lambda b, j: (b, 0, 0)),
                  pl.BlockSpec((CONV_WIDTH, C), lambda b, j: (0, 0)),
                  pl.BlockSpec((1, LANES), lambda b, j: (0, 0)),
                  pl.BlockSpec((1, LANES), lambda b, j: (0, 0)),
                  pl.BlockSpec((1, HEAD_DIM), lambda b, j: (0, 0)),
                  pl.BlockSpec((1, GDN_HEADS, HEAD_DIM, HEAD_DIM), lambda b, j: (b, 0, 0, 0))],
        out_specs=[pl.BlockSpec((1, tb, GDN_WIDTH), lambda b, j: (b, j, 0)),
                   pl.BlockSpec((1, GDN_HEADS, HEAD_DIM, HEAD_DIM), lambda b, j: (b, 0, 0, 0))],
        out_shape=[jax.ShapeDtypeStruct((B, T, GDN_WIDTH), f32),
                   jax.ShapeDtypeStruct((B, GDN_HEADS, HEAD_DIM, HEAD_DIM), f32)],
        scratch_shapes=[pltpu.VMEM((8 + tb + 8, C), f32)],
        compiler_params=_cparams(("parallel", "arbitrary")),
        name="gdn",
    )(gq3, sm3, gg3, conv_buf, conv_w, pad(a_log), pad(dt_bias), gnorm, s0)


def _cmp_weights(w1, b1, w2):
    nq = 2 * NSA_KV_HEADS
    kind = np.arange(nq) // NSA_KV_HEADS
    eye = jnp.eye(nq, dtype=f32)
    r = CMP_BLOCK // CMP_STRIDE
    w = w1[kind].reshape(nq, r, CMP_STRIDE, HEAD_DIM, HEAD_DIM)
    w = jnp.transpose(w, (2, 0, 3, 1, 4))
    wc = (w[:, :, :, :, None, :] * eye[None, :, None, None, :, None])
    wc = wc.reshape(CMP_STRIDE, nq * HEAD_DIM, r * nq * HEAD_DIM).astype(bf16)
    b1t = b1[kind].reshape(1, nq * HEAD_DIM)
    w2bd = (w2[kind][:, :, None, :] * eye[:, None, :, None]).reshape(nq * HEAD_DIM, nq * HEAD_DIM).astype(bf16)
    w7 = wc.reshape(CMP_STRIDE // 2, 2, 2, LANES, r, 2, LANES)
    wp = jnp.stack([w7[:, :, pr, :, :, pr, :].reshape(CMP_STRIDE // 2, 2 * LANES, r * LANES) for pr in range(2)])
    return wp, b1t, w2bd


def _segment_perm():
    seg = PAGE_SIZE // CMP_STRIDE
    off = np.arange(PAGE_SIZE)
    return jnp.asarray(((off[None, :] % CMP_STRIDE) * seg + off[None, :] // CMP_STRIDE == off[:, None]).astype(np.float32)).astype(bf16)


def _first_layer(xt_ref, wp_ref, n_row):
    accs = [jnp.zeros((n_row, 4 * HEAD_DIM), f32) for _ in range(2)]
    for s2 in range(CMP_STRIDE // 2):
        x0 = xt_ref[2 * s2]
        x1 = xt_ref[2 * s2 + 1]
        for pr in range(2):
            xs = jnp.concatenate([x0[:, pr * LANES:(pr + 1) * LANES], x1[:, pr * LANES:(pr + 1) * LANES]], axis=1)
            accs[pr] = accs[pr] + jnp.dot(xs, wp_ref[pr, s2], preferred_element_type=f32)
    return jnp.concatenate([accs[0][:, :LANES], accs[1][:, :LANES], accs[0][:, LANES:], accs[1][:, LANES:]], axis=1)


def _cmp_prompt_body(x_ref, perm_ref, wp_ref, b1_ref, w2_ref, o_ref, xt_ref, *, n_seg):
    seg = PAGE_SIZE // CMP_STRIDE
    for i in range(n_seg // seg):
        xp = jnp.dot(perm_ref[...], x_ref[0, i * PAGE_SIZE:(i + 1) * PAGE_SIZE, :].astype(bf16), preferred_element_type=f32)
        for s in range(CMP_STRIDE):
            xt_ref[s, i * seg:(i + 1) * seg, :] = xp[s * seg:(s + 1) * seg, :].astype(bf16)
    acc = _first_layer(xt_ref, wp_ref, n_seg)
    first = acc[:, :4 * HEAD_DIM]
    second = pltpu.roll(acc[:, 4 * HEAD_DIM:], n_seg - 1, 0)
    hid = _silu(b1_ref[...] + first + second)
    out = jnp.dot(hid.astype(bf16), w2_ref[...], preferred_element_type=f32)
    for qi in range(2 * NSA_KV_HEADS):
        o_ref[0, qi] = out[:, qi * HEAD_DIM:(qi + 1) * HEAD_DIM]


def _compress_prompt(kv3, wp, b1t, w2bd):
    B, T, _ = kv3.shape
    assert T % PAGE_SIZE == 0
    n_seg = T // CMP_STRIDE
    return pl.pallas_call(
        functools.partial(_cmp_prompt_body, n_seg=n_seg),
        grid=(B,),
        in_specs=[pl.BlockSpec((1, T, 4 * HEAD_DIM), lambda b: (b, 0, 0)),
                  pl.BlockSpec((PAGE_SIZE, PAGE_SIZE), lambda b: (0, 0)),
                  pl.BlockSpec((2, CMP_STRIDE // 2, 4 * HEAD_DIM, 4 * HEAD_DIM), lambda b: (0, 0, 0, 0)),
                  pl.BlockSpec((1, 4 * HEAD_DIM), lambda b: (0, 0)),
                  pl.BlockSpec((4 * HEAD_DIM, 4 * HEAD_DIM), lambda b: (0, 0))],
        out_specs=pl.BlockSpec((1, 4, n_seg, HEAD_DIM), lambda b: (b, 0, 0, 0)),
        out_shape=jax.ShapeDtypeStruct((B, 4, n_seg, HEAD_DIM), f32),
        scratch_shapes=[pltpu.VMEM((CMP_STRIDE, n_seg, 4 * HEAD_DIM), bf16)],
        compiler_params=_cparams(("parallel",)),
        name="compress_prompt",
    )(kv3, _segment_perm(), wp, b1t, w2bd)


def _topk_rows(score, k):
    n = score.shape[0]
    idx = lax.broadcasted_iota(jnp.int32, score.shape, 0)
    taken = jnp.zeros(score.shape, jnp.bool_)
    for _ in range(k):
        work = jnp.where(taken, -jnp.inf, score)
        m = jnp.max(work, axis=0, keepdims=True)
        cand = jnp.where((work == m) & jnp.logical_not(taken), idx, n)
        first = jnp.min(cand, axis=0, keepdims=True)
        taken = taken | (idx == first)
    return jnp.where(taken, 1.0, 0.0)


def _topk_rows_by_rank(score, k):
    n = score.shape[0]
    idx = lax.broadcasted_iota(jnp.int32, score.shape, 0)
    rank = jnp.zeros(score.shape, jnp.int32)
    for i in range(n):
        row = score[i:i + 1, :]
        ahead = (row > score) | ((row == score) & (idx > i))
        rank = rank + ahead.astype(jnp.int32)
    return jnp.where(rank < k, 1.0, 0.0)


def _softmax_rows(s, mask):
    s = jnp.where(mask, s, -jnp.inf)
    m = jnp.max(s, axis=-1, keepdims=True)
    m = jnp.where(m == -jnp.inf, 0.0, m)
    p = jnp.exp(s - m)
    return p / jnp.maximum(jnp.sum(p, axis=-1, keepdims=True), 1e-30)


def _attend_tile(qs, kbs, vbs, biases_t, carries):
    n = range(len(qs))
    ss = [lax.dot_general(kbs[i], qs[i], _NT, preferred_element_type=f32) for i in n]
    ss = [ss[i] + jnp.concatenate([biases_t[i]] * NSA_GROUP, axis=1) for i in n]
    m_news = [jnp.maximum(carries[i][0], jnp.max(ss[i], axis=0, keepdims=True)) for i in n]
    m_safes = [jnp.where(m == -jnp.inf, 0.0, m) for m in m_news]
    alphas = [jnp.exp(carries[i][0] - m_safes[i]) for i in n]
    ps = [jnp.exp(ss[i] - m_safes[i]) for i in n]
    ls = [alphas[i] * carries[i][1] + jnp.sum(ps[i], axis=0, keepdims=True) for i in n]
    accs = [alphas[i] * carries[i][2] + lax.dot_general(vbs[i], ps[i].astype(bf16), _TN, preferred_element_type=f32)
            for i in n]
    return [(m_news[i], ls[i], accs[i]) for i in n]


def _flash_init(R):
    return (jnp.full((1, R), -jnp.inf, f32), jnp.zeros((1, R), f32), jnp.zeros((HEAD_DIM, R), f32))


def _nsa_prompt_body(q_ref, qr_ref, kc_ref, ks_ref, vs_ref, kw_ref, vw_ref, sm_ref, ov_ref, ext_ref,
                     o_ref, wbias_ref, *, tq, tk, tkw, n_wt, T, n_cmp, n_blk):
    i = pl.program_id(1)
    t0 = i * tq
    scale = HEAD_DIM ** -0.5
    G = range(NSA_KV_HEADS)
    R = NSA_GROUP * tq
    gates = jax.nn.sigmoid(sm_ref[0])
    n_pad = kc_ref.shape[2]
    tpos_r = t0 + lax.broadcasted_iota(jnp.int32, (tq, 1), 0)
    tpos_c = t0 + lax.broadcasted_iota(jnp.int32, (1, tq), 1)
    heads = lambda x, g: [(x[:, (g * NSA_GROUP + r) * HEAD_DIM:(g * NSA_GROUP + r + 1) * HEAD_DIM] * scale).astype(bf16)
                          for r in range(NSA_GROUP)]

    nn = lax.broadcasted_iota(jnp.int32, (n_pad, 1), 0)
    cmask = (nn * CMP_STRIDE + (CMP_BLOCK - 1) <= tpos_c) & (nn < n_cmp)
    HH = [(g, r) for g in G for r in range(NSA_GROUP)]
    qhs = {g: heads(q_ref[0], g) for g in G}
    kcs = {g: kc_ref[0, g].astype(bf16) for g in G}
    vcs = {g: kc_ref[0, NSA_KV_HEADS + g].astype(bf16) for g in G}
    ss = {h: jnp.where(cmask, lax.dot_general(kcs[h[0]], qhs[h[0]][h[1]], _NT, preferred_element_type=f32), -jnp.inf)
          for h in HH}
    ms = {h: jnp.max(ss[h], axis=0, keepdims=True) for h in HH}
    es = {h: jnp.exp(ss[h] - jnp.where(ms[h] == -jnp.inf, 0.0, ms[h])) for h in HH}
    ps = {h: es[h] / jnp.maximum(jnp.sum(es[h], axis=0, keepdims=True), 1e-30) for h in HH}
    ocs_t = {h: lax.dot_general(vcs[h[0]], ps[h].astype(bf16), _TN, preferred_element_type=f32) for h in HH}
    imps_t = [ps[g, 0] + ps[g, 1] + ps[g, 2] + ps[g, 3] for g in G]

    score = jnp.concatenate([jnp.dot(ov_ref[...], imps_t[g], preferred_element_type=f32, precision=_HI)[:n_blk]
                             for g in G], axis=1)
    jj = lax.broadcasted_iota(jnp.int32, score.shape, 0)
    jt = jnp.concatenate([tpos_c] * NSA_KV_HEADS, axis=1) // SEL_BLOCK
    valid = jj <= jt
    forced = valid & ((jj == 0) | (jj == jt) | (jj == jt - 1))
    score = jnp.where(forced, jnp.inf, jnp.where(valid, score, -jnp.inf))
    sel_t = _topk_rows_by_rank(score, min(N_SELECT, n_blk)).astype(bf16)
    if n_blk < LANES:
        sel_t = jnp.concatenate([sel_t, jnp.zeros((LANES - n_blk, NSA_KV_HEADS * tq), bf16)], axis=0)
    w0 = pl.multiple_of(jnp.clip(t0 - WINDOW, 0, T - n_wt * tkw), tq)
    for j in range(n_wt):
        dist = tpos_c - (w0 + j * tkw + lax.broadcasted_iota(jnp.int32, (tkw, 1), 0))
        wbias_ref[j] = jnp.where((dist >= 0) & (dist < WINDOW), 0.0, -jnp.inf)

    stack = lambda x, g: (jnp.concatenate([x[:, (g * NSA_GROUP + r) * HEAD_DIM:(g * NSA_GROUP + r + 1) * HEAD_DIM]
                                           for r in range(NSA_GROUP)], axis=0) * scale).astype(bf16)
    qrs = [stack(qr_ref[0], g) for g in G]
    kv_tile = lambda ref, k0, n, g: ref[0, pl.ds(k0, n), g * HEAD_DIM:(g + 1) * HEAD_DIM].astype(bf16)

    def sel_step(kt, carries):
        k0 = pl.multiple_of(kt * tk, tk)
        causal = k0 + lax.broadcasted_iota(jnp.int32, (tk, 1), 0) <= tpos_c
        biases = [jnp.where((jnp.dot(ext_ref[kt], sel_t[:, g * tq:(g + 1) * tq], preferred_element_type=f32) > 0.5)
                            & causal, 0.0, -jnp.inf) for g in G]
        return tuple(_attend_tile(qrs, [kv_tile(ks_ref, k0, tk, g) for g in G], [kv_tile(vs_ref, k0, tk, g) for g in G],
                                  biases, carries))

    sel = lax.fori_loop(0, (t0 + tq + tk - 1) // tk, sel_step, tuple(_flash_init(R) for _ in G))

    win = [_flash_init(R) for _ in G]
    for j in range(n_wt):
        k0 = pl.multiple_of(w0 + j * tkw, tq)
        win = _attend_tile(qrs, [kv_tile(kw_ref, k0, tkw, g) for g in G], [kv_tile(vw_ref, k0, tkw, g) for g in G],
                           [wbias_ref[j]] * NSA_KV_HEADS, win)

    gates_t = gates.T
    for g in G:
        o_sel_t = sel[g][2] / jnp.maximum(sel[g][1], 1e-30)
        o_win_t = win[g][2] / jnp.maximum(win[g][1], 1e-30)
        outs = []
        for r in range(NSA_GROUP):
            hcol = GATE_COL + g * NSA_GROUP + r
            cols = slice(r * tq, (r + 1) * tq)
            mix_t = (gates_t[hcol:hcol + 1, :] * ocs_t[g, r]
                     + gates_t[NSA_HEADS + hcol:NSA_HEADS + hcol + 1, :] * o_sel_t[:, cols]
                     + gates_t[2 * NSA_HEADS + hcol:2 * NSA_HEADS + hcol + 1, :] * o_win_t[:, cols])
            outs.append(mix_t.T)
        o_ref[0, :, g * NSA_GROUP * HEAD_DIM:(g + 1) * NSA_GROUP * HEAD_DIM] = jnp.concatenate(outs, axis=1)


def _overlap_t(n_blk_pad, n_cmp, n_pad):
    ci = np.arange(n_pad)[None, :] * CMP_STRIDE
    sj = np.arange(n_blk_pad)[:, None] * SEL_BLOCK
    ov = (ci < sj + SEL_BLOCK) & (ci + CMP_BLOCK > sj) & (np.arange(n_pad)[None, :] < n_cmp)
    return jnp.asarray(ov.astype(np.float32))


def _nsa_prompt(q3, qr3, kc, kv3, win3, sm3):
    B, T, _ = q3.shape
    tq = 128
    tk = 512 if T >= 1024 else 256
    tkw = WINDOW + tq if T >= 1024 else 128
    n_wt = -(-min(T, WINDOW + tq) // tkw)
    assert T % tk == 0 and n_wt * tkw <= T
    n_seg = T // CMP_STRIDE
    n_cmp = n_seg - CMP_BLOCK // CMP_STRIDE + 1
    n_blk = -(-T // SEL_BLOCK)
    ov = _overlap_t(LANES, n_cmp, n_seg)
    key_blk = (np.arange(T // tk)[:, None, None] * tk + np.arange(tk)[None, :, None]) // SEL_BLOCK
    ex = jnp.asarray((np.arange(LANES)[None, None, :] == key_blk).astype(np.float32)).astype(bf16)
    qspec = pl.BlockSpec((1, tq, NSA_WIDTH), lambda b, i: (b, i, 0))
    col = lambda j: pl.BlockSpec((1, T, LANES), lambda b, i: (b, 0, j))
    return pl.pallas_call(
        functools.partial(_nsa_prompt_body, tq=tq, tk=tk, tkw=tkw, n_wt=n_wt, T=T, n_cmp=n_cmp, n_blk=n_blk),
        grid=(B, T // tq),
        in_specs=[qspec, qspec,
                  pl.BlockSpec((1, 4, n_seg, HEAD_DIM), lambda b, i: (b, 0, 0, 0)),
                  col(2), col(3), col(0), col(1),
                  pl.BlockSpec((1, tq, LANES), lambda b, i: (b, i, 0)),
                  pl.BlockSpec((LANES, n_seg), lambda b, i: (0, 0)),
                  pl.BlockSpec((T // tk, tk, LANES), lambda b, i: (0, 0, 0))],
        out_specs=qspec,
        out_shape=jax.ShapeDtypeStruct((B, T, NSA_WIDTH), f32),
        scratch_shapes=[pltpu.VMEM((n_wt, tkw, tq), f32)],
        compiler_params=_cparams(("parallel", "arbitrary")),
        name="nsa_prompt",
    )(q3, qr3, kc, kv3, kv3, win3, win3, sm3, ov, ex)


def _out_body(x_ref, ro_ref, no_ref, ng_ref, go_ref, w_ref, g_ref, y_ref):
    no = no_ref[...] * _silu(ng_ref[...])
    z = (jnp.dot(ro_ref[...].astype(bf16), w_ref[0:RET_WIDTH, :], preferred_element_type=f32)
         + jnp.dot(no.astype(bf16), w_ref[RET_WIDTH:RET_WIDTH + NSA_WIDTH, :], preferred_element_type=f32)
         + jnp.dot(go_ref[...].astype(bf16), w_ref[RET_WIDTH + NSA_WIDTH:, :], preferred_element_type=f32))
    y_ref[...] = x_ref[...] + z * lax.rsqrt(jnp.mean(z * z, axis=-1, keepdims=True) + EPS) * g_ref[...]


def _out(x2d, ro, no, ng, go, w_out_b, gain, tm):
    M = x2d.shape[0]
    row = lambda w: pl.BlockSpec((tm, w), lambda i: (i, 0))
    return pl.pallas_call(
        _out_body,
        grid=(M // tm,),
        in_specs=[row(D_MODEL), row(RET_WIDTH), row(NSA_WIDTH), row(NSA_WIDTH), row(GDN_WIDTH),
                  pl.BlockSpec((D_MODEL, D_MODEL), lambda i: (0, 0)),
                  pl.BlockSpec((1, D_MODEL), lambda i: (0, 0))],
        out_specs=row(D_MODEL),
        out_shape=jax.ShapeDtypeStruct((M, D_MODEL), f32),
        compiler_params=_cparams(("parallel",)),
        name="out",
    )(x2d, ro, no, ng, go, w_out_b, gain)


PAGES_PER_STEP = 64


def _page_specs(l, pg, half):
    return [pl.BlockSpec((None, None, 4 * HEAD_DIM, PAGE_SIZE),
                         lambda b, j, pt, i=i: (l, pt[b, j * pg + i], half, 0)) for i in range(pg)]


def _cmp_sample_body(pt_ref, *refs, pg):
    page_refs, perm_ref, wp_ref, o_ref, xt_ref = refs[:pg], refs[pg], refs[pg + 1], refs[pg + 2], refs[pg + 3]
    seg = PAGE_SIZE // CMP_STRIDE
    for i in range(pg):
        xp = lax.dot_general(perm_ref[...], page_refs[i][...].astype(bf16), _NT, preferred_element_type=f32)
        for s in range(CMP_STRIDE):
            xt_ref[s, i * seg:(i + 1) * seg, :] = xp[s * seg:(s + 1) * seg, :].astype(bf16)
    o_ref[0] = _first_layer(xt_ref, wp_ref, pg * seg)


def _cmp_sample(cache_t, l, page_table, wp):
    B, n_pages = page_table.shape
    pg = math.gcd(n_pages, PAGES_PER_STEP)
    n_row = pg * PAGE_SIZE // CMP_STRIDE
    return pl.pallas_call(
        functools.partial(_cmp_sample_body, pg=pg),
        grid_spec=pltpu.PrefetchScalarGridSpec(
            num_scalar_prefetch=1,
            grid=(B, n_pages // pg),
            in_specs=_page_specs(l, pg, 0) + [pl.BlockSpec((PAGE_SIZE, PAGE_SIZE), lambda b, j, pt: (0, 0)),
                                              pl.BlockSpec((2, CMP_STRIDE // 2, 4 * HEAD_DIM, 4 * HEAD_DIM),
                                                           lambda b, j, pt: (0, 0, 0, 0))],
            out_specs=pl.BlockSpec((1, n_row, 8 * HEAD_DIM), lambda b, j, pt: (b, j, 0)),
            scratch_shapes=[pltpu.VMEM((CMP_STRIDE, n_row, 4 * HEAD_DIM), bf16)]),
        out_shape=jax.ShapeDtypeStruct((B, n_pages * PAGE_SIZE // CMP_STRIDE, 8 * HEAD_DIM), f32),
        compiler_params=_cparams(("parallel", "arbitrary")),
        name="compress_sample",
    )(page_table, *([cache_t] * pg), _segment_perm(), wp)


def _stack_heads(x, g):
    return jnp.concatenate([x[:, (g * NSA_GROUP + r) * HEAD_DIM:(g * NSA_GROUP + r + 1) * HEAD_DIM]
                            for r in range(NSA_GROUP)], axis=0)


def _nsa_sample_mid_body(p_ref, b1_ref, w2_ref, q_ref, qr_ref, win_ref, st_ref, ov_ref,
                         ocmp_ref, owin_ref, selt_ref, *, Ts, n_cmp, past_len, wb):
    P = p_ref[0]
    n_seg = P.shape[0]
    hid = _silu(b1_ref[...] + P[:, :4 * HEAD_DIM] + pltpu.roll(P[:, 4 * HEAD_DIM:], n_seg - 1, 0))
    kcv = jnp.dot(hid.astype(bf16), w2_ref[...], preferred_element_type=f32)
    scale = HEAD_DIM ** -0.5
    R = NSA_GROUP * Ts
    trow = lax.broadcasted_iota(jnp.int32, (R, 1), 0) % Ts
    imps = []
    for g in range(NSA_KV_HEADS):
        qs = (_stack_heads(q_ref[0], g) * scale).astype(bf16)
        kc = kcv[:, g * HEAD_DIM:(g + 1) * HEAD_DIM].astype(bf16)
        vc = kcv[:, (NSA_KV_HEADS + g) * HEAD_DIM:(NSA_KV_HEADS + g + 1) * HEAD_DIM].astype(bf16)
        s = lax.dot_general(qs, kc, _NT, preferred_element_type=f32)
        nn = lax.broadcasted_iota(jnp.int32, (1, n_seg), 1)
        cmask = (nn * CMP_STRIDE + (CMP_BLOCK - 1) <= past_len + trow) & (nn < n_cmp)
        p = _softmax_rows(s, cmask)
        ocmp_ref[0, g] = jnp.dot(p.astype(bf16), vc, preferred_element_type=f32)
        imps.append(jnp.sum(p.reshape(NSA_GROUP, Ts, n_seg), axis=0))
        qrs = (_stack_heads(qr_ref[0], g) * scale).astype(bf16)
        kwt = st_ref[g * HEAD_DIM:(g + 1) * HEAD_DIM, :].astype(bf16)
        vwt = st_ref[(NSA_KV_HEADS + g) * HEAD_DIM:(NSA_KV_HEADS + g + 1) * HEAD_DIM, :].astype(bf16)
        wn = win_ref[0]
        knew = wn[:, g * HEAD_DIM:(g + 1) * HEAD_DIM].astype(bf16)
        vnew = wn[:, (NSA_KV_HEADS + g) * HEAD_DIM:(NSA_KV_HEADS + g + 1) * HEAD_DIM].astype(bf16)
        s_b = jnp.dot(qrs, kwt, preferred_element_type=f32)
        s_n = lax.dot_general(qrs, knew, _NT, preferred_element_type=f32)
        jb = lax.broadcasted_iota(jnp.int32, (1, wb), 1)
        dist_b = trow + wb - jb
        s_b = jnp.where((dist_b >= 0) & (dist_b < WINDOW) & (past_len - wb + jb >= 0), s_b, -jnp.inf)
        dist_n = trow - lax.broadcasted_iota(jnp.int32, (1, Ts), 1)
        s_n = jnp.where((dist_n >= 0) & (dist_n < WINDOW), s_n, -jnp.inf)
        m = jnp.maximum(jnp.max(s_b, axis=-1, keepdims=True), jnp.max(s_n, axis=-1, keepdims=True))
        m = jnp.where(m == -jnp.inf, 0.0, m)
        pb = jnp.exp(s_b - m)
        pn = jnp.exp(s_n - m)
        den = jnp.sum(pb, axis=-1, keepdims=True) + jnp.sum(pn, axis=-1, keepdims=True)
        o = (lax.dot_general(pb.astype(bf16), vwt, _NT, preferred_element_type=f32)
             + jnp.dot(pn.astype(bf16), vnew, preferred_element_type=f32))
        owin_ref[0, g] = o / jnp.maximum(den, 1e-30)

    imp2 = jnp.concatenate(imps, axis=0)
    score = lax.dot_general(ov_ref[...], imp2, _NT, preferred_element_type=f32, precision=_HI)
    jj = lax.broadcasted_iota(jnp.int32, score.shape, 0)
    jt = (past_len + lax.broadcasted_iota(jnp.int32, score.shape, 1) % Ts) // SEL_BLOCK
    valid = jj <= jt
    forced = valid & ((jj == 0) | (jj == jt) | (jj == jt - 1))
    score = jnp.where(forced, jnp.inf, jnp.where(valid, score, -jnp.inf))
    selt_ref[0] = _topk_rows(score, N_SELECT)


def _nsa_sample_mid(P, b1t, w2bd, q3, qr3, win3, st_t, l, past_len):
    B, Ts, _ = q3.shape
    n_seg = P.shape[1]
    n_cmp = n_seg - CMP_BLOCK // CMP_STRIDE + 1
    n_blk = -(-(past_len + Ts) // SEL_BLOCK)
    nb_pad = -(-n_blk // 8) * 8
    wb = st_t.shape[-1]
    ov = _overlap_t(nb_pad, n_cmp, n_seg)
    R = NSA_GROUP * Ts
    tok = lambda w: pl.BlockSpec((1, Ts, w), lambda b: (b, 0, 0))
    return pl.pallas_call(
        functools.partial(_nsa_sample_mid_body, Ts=Ts, n_cmp=n_cmp, past_len=past_len, wb=wb),
        grid=(B,),
        in_specs=[pl.BlockSpec((1, n_seg, 8 * HEAD_DIM), lambda b: (b, 0, 0)),
                  pl.BlockSpec((1, 4 * HEAD_DIM), lambda b: (0, 0)),
                  pl.BlockSpec((4 * HEAD_DIM, 4 * HEAD_DIM), lambda b: (0, 0)),
                  tok(NSA_WIDTH), tok(NSA_WIDTH), tok(4 * HEAD_DIM),
                  pl.BlockSpec((None, None, 4 * HEAD_DIM, wb), lambda b: (l, b, 0, 0)),
                  pl.BlockSpec((nb_pad, n_seg), lambda b: (0, 0))],
        out_specs=[pl.BlockSpec((1, NSA_KV_HEADS, R, HEAD_DIM), lambda b: (b, 0, 0, 0)),
                   pl.BlockSpec((1, NSA_KV_HEADS, R, HEAD_DIM), lambda b: (b, 0, 0, 0)),
                   pl.BlockSpec((1, nb_pad, NSA_KV_HEADS * Ts), lambda b: (b, 0, 0))],
        out_shape=[jax.ShapeDtypeStruct((B, NSA_KV_HEADS, R, HEAD_DIM), f32),
                   jax.ShapeDtypeStruct((B, NSA_KV_HEADS, R, HEAD_DIM), f32),
                   jax.ShapeDtypeStruct((B, nb_pad, NSA_KV_HEADS * Ts), f32)],
        compiler_params=_cparams(("parallel",)),
        name="nsa_sample_mid",
    )(P, b1t, w2bd, q3, qr3, win3, st_t, ov)


def _nsa_sample_sel_body(pt_ref, *refs, pg, Ts, n_steps):
    page_refs = refs[:pg]
    qr_ref, kvn_ref, selt_ref, ex_ref, ocmp_ref, owin_ref, sm_ref, o_ref, m_ref, l_ref, acc_ref = refs[pg:]
    j = pl.program_id(1)
    scale = HEAD_DIM ** -0.5
    R = NSA_GROUP * Ts
    GT = NSA_KV_HEADS * Ts
    nb = 2 * pg
    trow = lax.broadcasted_iota(jnp.int32, (R, 1), 0) % Ts
    eye = (lax.broadcasted_iota(jnp.int32, (GT, GT), 0) == lax.broadcasted_iota(jnp.int32, (GT, GT), 1)).astype(bf16)
    sel_j = selt_ref[0, pl.ds(pl.multiple_of(j * nb, nb), nb), :].astype(bf16)
    if nb < LANES:
        sel_j = jnp.concatenate([sel_j, jnp.zeros((LANES - nb, GT), bf16)], axis=0)
    sel_rows = lax.dot_general(eye, sel_j, _NT, preferred_element_type=f32)
    full = jnp.dot(sel_rows.astype(bf16), ex_ref[...], preferred_element_type=f32)

    G = range(NSA_KV_HEADS)
    qrs = [(_stack_heads(qr_ref[0], g) * scale).astype(bf16) for g in G]

    @pl.when(j == 0)
    def _():
        kvn = kvn_ref[0]
        for g in G:
            knew = kvn[:, (2 * NSA_KV_HEADS + g) * HEAD_DIM:(2 * NSA_KV_HEADS + g + 1) * HEAD_DIM].astype(bf16)
            vnew = kvn[:, (3 * NSA_KV_HEADS + g) * HEAD_DIM:(3 * NSA_KV_HEADS + g + 1) * HEAD_DIM].astype(bf16)
            s_n = lax.dot_general(qrs[g], knew, _NT, preferred_element_type=f32)
            s_n = jnp.where(lax.broadcasted_iota(jnp.int32, (1, Ts), 1) <= trow, s_n, -jnp.inf)
            m0 = jnp.max(s_n, axis=-1, keepdims=True)
            p0 = jnp.exp(s_n - m0)
            m_ref[g] = m0
            l_ref[g] = jnp.sum(p0, axis=-1, keepdims=True)
            acc_ref[g] = jnp.dot(p0.astype(bf16), vnew, preferred_element_type=f32)

    kts = [jnp.concatenate([page_refs[i][g * HEAD_DIM:(g + 1) * HEAD_DIM, :] for i in range(pg)], axis=1).astype(bf16)
           for g in G]
    vts = [jnp.concatenate([page_refs[i][(NSA_KV_HEADS + g) * HEAD_DIM:(NSA_KV_HEADS + g + 1) * HEAD_DIM, :]
                            for i in range(pg)], axis=1).astype(bf16) for g in G]
    ss = [jnp.dot(qrs[g], kts[g], preferred_element_type=f32) for g in G]
    ss = [jnp.where(jnp.concatenate([full[g * Ts:(g + 1) * Ts]] * NSA_GROUP, axis=0) > 0.5, ss[g], -jnp.inf) for g in G]
    m_olds = [m_ref[g] for g in G]
    m_news = [jnp.maximum(m_olds[g], jnp.max(ss[g], axis=-1, keepdims=True)) for g in G]
    alphas = [jnp.exp(m_olds[g] - m_news[g]) for g in G]
    ps = [jnp.exp(ss[g] - m_news[g]) for g in G]
    pvs = [lax.dot_general(ps[g].astype(bf16), vts[g], _NT, preferred_element_type=f32) for g in G]
    for g in G:
        l_ref[g] = alphas[g] * l_ref[g] + jnp.sum(ps[g], axis=-1, keepdims=True)
        acc_ref[g] = alphas[g] * acc_ref[g] + pvs[g]
        m_ref[g] = m_news[g]


    @pl.when(j == n_steps - 1)
    def _():
        gates = jax.nn.sigmoid(sm_ref[0])
        outs = []
        for g in range(NSA_KV_HEADS):
            o_sel = acc_ref[g] / jnp.maximum(l_ref[g], 1e-30)
            o_cmp = ocmp_ref[0, g]
            o_win = owin_ref[0, g]
            for r in range(NSA_GROUP):
                hcol = GATE_COL + g * NSA_GROUP + r
                rows = slice(r * Ts, (r + 1) * Ts)
                outs.append(gates[:, hcol:hcol + 1] * o_cmp[rows]
                            + gates[:, NSA_HEADS + hcol:NSA_HEADS + hcol + 1] * o_sel[rows]
                            + gates[:, 2 * NSA_HEADS + hcol:2 * NSA_HEADS + hcol + 1] * o_win[rows])
        o_ref[0] = jnp.concatenate(outs, axis=1)


def _nsa_sample_sel(cache_t, l, page_table, qr3, kv3, selt, ocmp, owin, sm3):
    B, n_pages = page_table.shape
    Ts = qr3.shape[1]
    pg = math.gcd(n_pages, PAGES_PER_STEP)
    n_steps = n_pages // pg
    R = NSA_GROUP * Ts
    ex = jnp.asarray((np.arange(LANES)[:, None] == (np.arange(pg * PAGE_SIZE)[None, :] // SEL_BLOCK)).astype(np.float32)).astype(bf16)
    tok = lambda w: pl.BlockSpec((1, Ts, w), lambda b, j, pt: (b, 0, 0))
    stk = pl.BlockSpec((1, NSA_KV_HEADS, R, HEAD_DIM), lambda b, j, pt: (b, 0, 0, 0))
    return pl.pallas_call(
        functools.partial(_nsa_sample_sel_body, pg=pg, Ts=Ts, n_steps=n_steps),
        grid_spec=pltpu.PrefetchScalarGridSpec(
            num_scalar_prefetch=1,
            grid=(B, n_steps),
            in_specs=_page_specs(l, pg, 1) + [
                tok(NSA_WIDTH), tok(8 * HEAD_DIM),
                pl.BlockSpec((1, selt.shape[1], selt.shape[2]), lambda b, j, pt: (b, 0, 0)),
                pl.BlockSpec((LANES, pg * PAGE_SIZE), lambda b, j, pt: (0, 0)),
                stk, stk, tok(LANES)],
            out_specs=tok(NSA_WIDTH),
            scratch_shapes=[pltpu.VMEM((NSA_KV_HEADS, R, 1), f32), pltpu.VMEM((NSA_KV_HEADS, R, 1), f32),
                            pltpu.VMEM((NSA_KV_HEADS, R, HEAD_DIM), f32)]),
        out_shape=jax.ShapeDtypeStruct((B, Ts, NSA_WIDTH), f32),
        compiler_params=_cparams(("parallel", "arbitrary")),
        name="nsa_sample_sel",
    )(page_table, *([cache_t] * pg), qr3, kv3, selt, ex, ocmp, owin, sm3)


def _nsa_sample(q3, qr3, kv3, win3, sm3, cache_t, st_t, l, page_table, wp, b1t, w2bd):
    Ts = q3.shape[1]
    n_pages = page_table.shape[1]
    past_len = n_pages * PAGE_SIZE
    assert (past_len + Ts) // CMP_STRIDE * CMP_STRIDE <= past_len
    assert past_len % SEL_BLOCK == 0 and Ts <= SEL_BLOCK
    P = _cmp_sample(cache_t, l, page_table, wp)
    ocmp, owin, selt = _nsa_sample_mid(P, b1t, w2bd, q3, qr3, win3, st_t, l, past_len)
    return _nsa_sample_sel(cache_t, l, page_table, qr3, kv3, selt, ocmp, owin, sm3)


def _rope_tables(pos):
    half = HEAD_DIM // 2
    inv = ROPE_THETA ** (-jnp.arange(half, dtype=f32) / half)
    ang = pos.astype(f32)[:, None] * inv[None, :]
    cos, sin = jnp.cos(ang), jnp.sin(ang)
    return jnp.concatenate([cos, cos, cos, cos], axis=1), jnp.concatenate([-sin, sin, -sin, sin], axis=1)


def _prep_w_in(w):
    cols = [w[:, _OFF[n][0]:_OFF[n][1]] for n in _PAD_ORDER]
    used = sum(c.shape[1] for c in cols)
    cols.append(jnp.zeros((w.shape[0], PROJ_PAD - used), w.dtype))
    return jnp.concatenate(cols, axis=1).astype(bf16)


def _layer(x, pos, prm, l, nsa_fn, ret_state, gdn_state, conv_buf, tm, state_t=None):
    B, T, _ = x.shape
    M = B * T
    tm = min(tm, M)
    cosf, sinf = _rope_tables(pos)
    if tm > T:
        cosf, sinf = jnp.tile(cosf, (tm // T, 1)), jnp.tile(sinf, (tm // T, 1))
    x2d = x.reshape(M, D_MODEL)
    outs = _proj(x2d, prm['norm_pre'][l][None], prm['w_in_p'][l], cosf, sinf, tm, state_t)
    ret, q, qr, kv, win, ng, gq, gg, sm = outs[:9]
    r3 = lambda a: a.reshape(B, T, a.shape[-1])
    ro, ret_new = _retention(r3(ret), ret_state, prm['ret_gn'][l][None], T)
    go, gdn_new = _gdn(r3(gq), r3(sm), r3(gg), conv_buf, prm['gdn_conv'][l], prm['gdn_a_log'][l],
                       prm['gdn_dt_bias'][l], prm['gdn_norm'][l][None], gdn_state, T)
    assert T >= CONV_WIDTH - 1
    conv_new = r3(gq)[:, T - (CONV_WIDTH - 1):]
    no, win_new = nsa_fn(r3(q), r3(qr), r3(kv), r3(win), r3(sm))
    y = _out(x2d, ro.reshape(M, -1), no.reshape(M, -1), ng, go.reshape(M, -1), prm['w_out_b'][l],
             prm['norm_post'][l][None], tm)
    if state_t is not None:
        return y.reshape(B, T, D_MODEL), (outs[9:], ret_new, gdn_new, conv_new)
    rows = kv.reshape(B, T, 4, NSA_KV_HEADS, HEAD_DIM)
    return y.reshape(B, T, D_MODEL), (rows, win_new, ret_new, gdn_new, conv_new)


def kernel(x_prompt, x_sample, cache_nsa_kv, page_table, state_nsa_win, state_ret, state_gdn, state_gdn_conv,
           w_in, w_out, norm_pre, norm_post, ret_gn, gdn_norm, gdn_conv, gdn_a_log, gdn_dt_bias,
           cmp_w1, cmp_b1, cmp_w2):
    Bp, T, _ = x_prompt.shape
    Bs, Ts, _ = x_sample.shape
    depth = w_in.shape[0]
    past_len = page_table.shape[1] * PAGE_SIZE
    pos_p = jnp.arange(T, dtype=jnp.int32)
    pos_s = past_len + jnp.arange(Ts, dtype=jnp.int32)
    prm = {'w_in_p': [_prep_w_in(w_in[l]) for l in range(depth)],
           'w_out_b': [w_out[l].astype(bf16) for l in range(depth)],
           'norm_pre': norm_pre, 'norm_post': norm_post, 'ret_gn': ret_gn, 'gdn_norm': gdn_norm,
           'gdn_conv': gdn_conv, 'gdn_a_log': gdn_a_log, 'gdn_dt_bias': gdn_dt_bias}
    yp, ys = x_prompt, x_sample
    cache_t = jnp.transpose(cache_nsa_kv, (0, 1, 3, 4, 5, 2)).reshape(
        depth, cache_nsa_kv.shape[1], 8 * HEAD_DIM, PAGE_SIZE)
    st_t = jnp.transpose(state_nsa_win, (0, 1, 3, 4, 5, 2)).reshape(
        depth, Bs, 4 * HEAD_DIM, state_nsa_win.shape[2])
    st_p, st_s = [], []
    stacks = None
    for l in range(depth):
        wp, b1t, w2bd = _cmp_weights(cmp_w1[l], cmp_b1[l], cmp_w2[l])

        def nsa_prompt(q3, qr3, kv3, win3, sm3):
            kc = _compress_prompt(kv3, wp, b1t, w2bd)
            return _nsa_prompt(q3, qr3, kc, kv3, win3, sm3), None

        def nsa_sample(q3, qr3, kv3, win3, sm3):
            no = _nsa_sample(q3, qr3, kv3, win3, sm3, cache_t, st_t, l, page_table, wp, b1t, w2bd)
            win_rows = win3.reshape(Bs, Ts, 2, NSA_KV_HEADS, HEAD_DIM)
            keys = jnp.concatenate([state_nsa_win[l], win_rows], axis=1)
            return no, keys[:, -state_nsa_win.shape[2]:]

        yp, sp = _layer(yp, pos_p, prm, l, nsa_prompt,
                        jnp.zeros((Bp, RET_HEADS, HEAD_DIM, HEAD_DIM), f32),
                        jnp.zeros((Bp, GDN_HEADS, HEAD_DIM, HEAD_DIM), f32),
                        jnp.zeros((Bp, CONV_WIDTH - 1, 3 * GDN_WIDTH), f32), min(PROMPT_ROW_TILE, T),
                        state_t=(l, depth, Bp, T, stacks))
        stacks = sp[0]
        ys, ss = _layer(ys, pos_s, prm, l, nsa_sample, state_ret[l], state_gdn[l], state_gdn_conv[l], Bs * Ts)
        st_p.append(sp)
        st_s.append(ss)
    stk = lambda sts, i: jnp.stack([s[i] for s in sts])
    keep = stacks[1].shape[-1]
    kv_p = jnp.transpose(stacks[0].reshape(depth, Bp, 4, NSA_KV_HEADS, HEAD_DIM, T), (0, 1, 5, 2, 3, 4))
    win_p = jnp.transpose(stacks[1].reshape(depth, Bp, 2, NSA_KV_HEADS, HEAD_DIM, keep), (0, 1, 5, 2, 3, 4))
    return (yp, ys, kv_p, stk(st_s, 0), win_p, stk(st_s, 1), stk(st_p, 1), stk(st_s, 2),
            stk(st_p, 2), stk(st_s, 3), stk(st_p, 3), stk(st_s, 4))
```
